```python
import jax, jax.numpy as jnp
from jax import lax
import numpy as np

D_MODEL = 2048
BATCH = 8
SEQ = 2048
DEPTH = 1

HEAD_DIM = 128
N_HEADS_SB = 8
N_HEADS_FOX = 8
D_SB = N_HEADS_SB * HEAD_DIM
D_FOX = N_HEADS_FOX * HEAD_DIM
D_FF = -(-8 * D_MODEL // (3 * 256)) * 256
Q_BLOCK = 128
RMS_EPS = 1e-6
SPLIT_SIZES = (D_SB, D_SB, D_SB, D_FOX, D_FOX, D_FOX, N_HEADS_FOX, D_MODEL, D_MODEL)
SPLIT_POINTS = (D_SB, 2 * D_SB, 3 * D_SB,
                3 * D_SB + D_FOX, 3 * D_SB + 2 * D_FOX, 3 * D_SB + 3 * D_FOX,
                3 * D_SB + 3 * D_FOX + N_HEADS_FOX,
                3 * D_SB + 3 * D_FOX + N_HEADS_FOX + D_MODEL)
D_IN = 3 * D_SB + 3 * D_FOX + N_HEADS_FOX + 2 * D_MODEL

kernel_name = "hybrid_stickbreak_forgetting_gated_block"


def rms_norm(x, g):
    xf = x.astype(jnp.float32)
    y = xf * lax.rsqrt(jnp.mean(xf * xf, axis=-1, keepdims=True) + RMS_EPS)
    return (y * g.astype(jnp.float32)).astype(x.dtype)


def split_heads(t, n_heads):
    b, s, _ = t.shape
    return t.reshape(b, s, n_heads, HEAD_DIM).transpose(0, 2, 1, 3)


def merge_heads(t):
    b, h, s, d = t.shape
    return t.transpose(0, 2, 1, 3).reshape(b, s, h * d)


def stick_breaking_block(q_blk, k, v, q_start):
    tq, tk = q_blk.shape[2], k.shape[2]
    z = jnp.einsum("bhqd,bhkd->bhqk", q_blk, k).astype(jnp.float32) * (HEAD_DIM ** -0.5)
    t_idx = q_start + jnp.arange(tq)[:, None]
    s_idx = jnp.arange(tk)[None, :]
    mask = s_idx < t_idx
    log_keep = jnp.where(mask, jax.nn.log_sigmoid(-z), 0.0)
    between = lax.cumsum(log_keep, axis=3, reverse=True) - log_keep
    w = jnp.where(mask, jnp.exp(jax.nn.log_sigmoid(z) + between), 0.0)
    return jnp.einsum("bhqk,bhkd->bhqd", w.astype(v.dtype), v)


def forgetting_block(q_blk, k, v, cum_q, cum_k, q_start):
    tq, tk = q_blk.shape[2], k.shape[2]
    logits = jnp.einsum("bhqd,bhkd->bhqk", q_blk, k).astype(jnp.float32) * (HEAD_DIM ** -0.5)
    logits = logits + cum_q[..., :, None] - cum_k[..., None, :]
    t_idx = q_start + jnp.arange(tq)[:, None]
    s_idx = jnp.arange(tk)[None, :]
    logits = jnp.where(s_idx <= t_idx, logits, -jnp.inf)
    p = jax.nn.softmax(logits, axis=-1)
    return jnp.einsum("bhqk,bhkd->bhqd", p.astype(v.dtype), v)


def token_mixer(u, w_in, b_forget, w_branch_sb, w_branch_fox, w_out):
    s = u.shape[1]
    proj = u @ w_in
    q_sb, k_sb, v_sb, q_fx, k_fx, v_fx, f_logit, g_sb, g_fx = jnp.split(proj, SPLIT_POINTS, axis=-1)
    q_sb, k_sb, v_sb = (split_heads(t, N_HEADS_SB) for t in (q_sb, k_sb, v_sb))
    q_fx, k_fx, v_fx = (split_heads(t, N_HEADS_FOX) for t in (q_fx, k_fx, v_fx))
    log_f = jax.nn.log_sigmoid((f_logit + b_forget).astype(jnp.float32))
    cum = lax.cumsum(log_f.transpose(0, 2, 1), axis=2)

    outs_sb, outs_fx = [], []
    for i in range(s // Q_BLOCK):
        q0, q1 = i * Q_BLOCK, (i + 1) * Q_BLOCK
        outs_sb.append(stick_breaking_block(q_sb[:, :, q0:q1], k_sb[:, :, :q1], v_sb[:, :, :q1], q0))
        outs_fx.append(forgetting_block(q_fx[:, :, q0:q1], k_fx[:, :, :q1], v_fx[:, :, :q1],
                                        cum[:, :, q0:q1], cum[:, :, :q1], q0))
    o_sb = merge_heads(jnp.concatenate(outs_sb, axis=2))
    o_fx = merge_heads(jnp.concatenate(outs_fx, axis=2))

    merged = jax.nn.sigmoid(g_sb) * (o_sb @ w_branch_sb) + jax.nn.sigmoid(g_fx) * (o_fx @ w_branch_fox)
    return merged @ w_out


def swiglu(u, w_gate, w_up, w_down):
    return (jax.nn.silu(u @ w_gate) * (u @ w_up)) @ w_down


def setup_inputs(seed: int = 0) -> dict:
    key = jax.random.key(seed)
    ks = jax.random.split(key, 14)
    f32 = jnp.float32

    def dense(k, fan_in, fan_out):
        return jax.random.normal(k, (DEPTH, fan_in, fan_out), f32) * (fan_in ** -0.5)

    def gain(k):
        return 1.0 + 0.02 * jax.random.normal(k, (DEPTH, D_MODEL), f32)

    return {
        "x": jax.random.normal(ks[0], (BATCH, SEQ, D_MODEL), f32),
        "norm_mix_pre": gain(ks[1]),
        "norm_mix_post": gain(ks[2]),
        "w_in": dense(ks[3], D_MODEL, D_IN),
        "b_forget": 3.0 + 0.1 * jax.random.normal(ks[4], (DEPTH, N_HEADS_FOX), f32),
        "w_branch_sb": dense(ks[5], D_SB, D_MODEL),
        "w_branch_fox": dense(ks[6], D_FOX, D_MODEL),
        "w_out": dense(ks[7], D_MODEL, D_MODEL),
        "norm_ffn_pre": gain(ks[8]),
        "norm_ffn_post": gain(ks[9]),
        "w_ffn_gate": dense(ks[10], D_MODEL, D_FF),
        "w_ffn_up": dense(ks[11], D_MODEL, D_FF),
        "w_ffn_down": dense(ks[12], D_FF, D_MODEL),
    }


def reference(x, norm_mix_pre, norm_mix_post, w_in, b_forget, w_branch_sb, w_branch_fox, w_out,
              norm_ffn_pre, norm_ffn_post, w_ffn_gate, w_ffn_up, w_ffn_down):
    h = x
    for l in range(DEPTH):
        mix = token_mixer(rms_norm(h, norm_mix_pre[l]), w_in[l], b_forget[l],
                          w_branch_sb[l], w_branch_fox[l], w_out[l])
        h = h + rms_norm(mix, norm_mix_post[l])
        ff = swiglu(rms_norm(h, norm_ffn_pre[l]), w_ffn_gate[l], w_ffn_up[l], w_ffn_down[l])
        h = h + rms_norm(ff, norm_ffn_post[l])
    return h
```

```python
import functools

import jax
import jax.numpy as jnp
from jax import lax
from jax.experimental import pallas as pl
from jax.experimental.pallas import tpu as pltpu

F32 = jnp.float32
BF16 = jnp.bfloat16

D_MODEL = 2048
HEAD_DIM = 128
N_HEADS = 8
D_BRANCH = N_HEADS * HEAD_DIM
D_FF = 5632
RMS_EPS = 1e-6
ATTN_SCALE = HEAD_DIM ** -0.5
NEG_BIG = -1e30

LANES = 128
MXU_DIM = 256
VMEM_LIMIT = 56 * 1024 * 1024

TM_PROJ = 1024
TN_PROJ = 1024
TQ = 256
TK = 256
TM_MIX = 256
TM_FFN = 512
TF_FFN = 512
CUM_CHUNK = 256


def _params(semantics):
    return pltpu.CompilerParams(dimension_semantics=semantics, vmem_limit_bytes=VMEM_LIMIT)


def _rms_norm(x, g):
    ms = jnp.mean(x * x, axis=-1, keepdims=True)
    return (x * lax.rsqrt(ms + RMS_EPS)) * g


def _log_sigmoid(x):
    return jnp.minimum(x, 0.0) - jnp.log1p(jnp.exp(-jnp.abs(x)))


def _sigmoid(x):
    return 1.0 / (1.0 + jnp.exp(-x))


def _split3(x):
    hi = x.astype(BF16)
    r1 = x - hi.astype(F32)
    mid = r1.astype(BF16)
    lo = (r1 - mid.astype(F32)).astype(BF16)
    return hi, mid, lo


def _qkv_kernel(x_ref, g_ref, w_ref, o_ref, u_ref):
    @pl.when(pl.program_id(1) == 0)
    def _():
        u_ref[...] = _rms_norm(x_ref[...], g_ref[...]).astype(BF16)

    o_ref[...] = jnp.dot(u_ref[...], w_ref[...], preferred_element_type=F32).astype(o_ref.dtype)


def _qkv_proj(x2, g, w):
    t, d = x2.shape
    n = w.shape[1]
    return pl.pallas_call(
        _qkv_kernel,
        grid=(t // TM_PROJ, n // TN_PROJ),
        in_specs=[
            pl.BlockSpec((TM_PROJ, d), lambda i, j: (i, 0)),
            pl.BlockSpec((1, d), lambda i, j: (0, 0)),
            pl.BlockSpec((d, TN_PROJ), lambda i, j: (0, j)),
        ],
        out_specs=pl.BlockSpec((TM_PROJ, TN_PROJ), lambda i, j: (i, j)),
        out_shape=jax.ShapeDtypeStruct((t, n), BF16),
        scratch_shapes=[pltpu.VMEM((TM_PROJ, d), BF16)],
        compiler_params=_params(("arbitrary", "arbitrary")),
        name="qkv_proj",
    )(x2, g, w)


def _gates_kernel(x_ref, g_ref, w_ref, wf_ref, bf_ref, o_ref, cumc_ref, cumr_ref,
                  u_ref, carry_ref, *, tiles_per_seq):
    i = pl.program_id(0)

    @pl.when(pl.program_id(1) == 0)
    def _():
        u_ref[...] = _rms_norm(x_ref[...], g_ref[...]).astype(BF16)
        f = jnp.dot(u_ref[...], wf_ref[...], preferred_element_type=F32)
        log_f = _log_sigmoid(f + bf_ref[...])

        @pl.when(i % tiles_per_seq == 0)
        def _():
            carry_ref[...] = jnp.zeros_like(carry_ref)

        r = lax.broadcasted_iota(jnp.int32, (CUM_CHUNK, CUM_CHUNK), 0)
        c = lax.broadcasted_iota(jnp.int32, (CUM_CHUNK, CUM_CHUNK), 1)
        lower = (c <= r).astype(BF16)
        carry = carry_ref[...]
        for ch in range(TM_PROJ // CUM_CHUNK):
            blk = log_f[ch * CUM_CHUNK:(ch + 1) * CUM_CHUNK]
            hi, mid, lo = _split3(blk)
            cs = (jnp.dot(lower, hi, preferred_element_type=F32)
                  + jnp.dot(lower, mid, preferred_element_type=F32)
                  + jnp.dot(lower, lo, preferred_element_type=F32)) + carry
            cumc_ref[ch * CUM_CHUNK:(ch + 1) * CUM_CHUNK, :] = cs
            carry = cs[CUM_CHUNK - 1:CUM_CHUNK, :]
        carry_ref[...] = carry
        cumr_ref[0] = cumc_ref[...].T[:N_HEADS, :]

    acc = jnp.dot(u_ref[...], w_ref[...], preferred_element_type=F32)
    o_ref[...] = _sigmoid(acc).astype(o_ref.dtype)


def _gates_proj(x2, g, w, wf, bf, seq):
    t, d = x2.shape
    n = w.shape[1]
    tiles_per_seq = seq // TM_PROJ
    return pl.pallas_call(
        functools.partial(_gates_kernel, tiles_per_seq=tiles_per_seq),
        grid=(t // TM_PROJ, n // TN_PROJ),
        in_specs=[
            pl.BlockSpec((TM_PROJ, d), lambda i, j: (i, 0)),
            pl.BlockSpec((1, d), lambda i, j: (0, 0)),
            pl.BlockSpec((d, TN_PROJ), lambda i, j: (0, j)),
            pl.BlockSpec((d, LANES), lambda i, j: (0, 0)),
            pl.BlockSpec((1, LANES), lambda i, j: (0, 0)),
        ],
        out_specs=[
            pl.BlockSpec((TM_PROJ, TN_PROJ), lambda i, j: (i, j)),
            pl.BlockSpec((TM_PROJ, LANES), lambda i, j: (i, 0)),
            pl.BlockSpec((1, N_HEADS, TM_PROJ),
                         lambda i, j: (i // tiles_per_seq, 0, i % tiles_per_seq)),
        ],
        out_shape=[
            jax.ShapeDtypeStruct((t, n), BF16),
            jax.ShapeDtypeStruct((t, LANES), F32),
            jax.ShapeDtypeStruct((t // seq, N_HEADS, seq), F32),
        ],
        scratch_shapes=[pltpu.VMEM((TM_PROJ, d), BF16), pltpu.VMEM((1, LANES), F32)],
        compiler_params=_params(("arbitrary", "arbitrary")),
        name="gates_proj",
    )(x2, g, w, wf, bf)


def _qk(q, k):
    return lax.dot_general(q, k, (((1,), (1,)), ((), ())), preferred_element_type=F32)


def _sb_kernel(q_ref, k_ref, v_ref, o_ref):
    qi = pl.program_id(2)
    q = q_ref[0]
    n_diag = TQ // TK
    r = lax.broadcasted_iota(jnp.int32, (TK, TK), 0)
    c = lax.broadcasted_iota(jnp.int32, (TK, TK), 1)
    later = (r > c).astype(BF16)
    t_idx = qi * TQ + lax.broadcasted_iota(jnp.int32, (TQ, TK), 0)
    s_loc = lax.broadcasted_iota(jnp.int32, (TQ, TK), 1)

    def block(kb, carry, acc, masked):
        k0 = pl.multiple_of(kb * TK, TK)
        k = k_ref[0, pl.ds(k0, TK), :]
        v = v_ref[0, pl.ds(k0, TK), :]
        z = _qk(q, k) * ATTN_SCALE
        sp = jnp.log1p(jnp.exp(-jnp.abs(z)))
        log_beta = jnp.minimum(z, 0.0) - sp
        log_keep = -jnp.maximum(z, 0.0) - sp
        if masked:
            valid = (k0 + s_loc) < t_idx
            log_keep = jnp.where(valid, log_keep, 0.0)
        hi = log_keep.astype(BF16)
        lo = (log_keep - hi.astype(F32)).astype(BF16)
        between = (jnp.dot(hi, later, preferred_element_type=F32)
                   + jnp.dot(lo, later, preferred_element_type=F32))
        w = jnp.exp(log_beta + between + carry)
        if masked:
            w = jnp.where(valid, w, 0.0)
        acc = acc + jnp.dot(w.astype(BF16), v, preferred_element_type=F32)
        carry = carry + jnp.sum(log_keep, axis=1, keepdims=True)
        return carry, acc

    carry = jnp.zeros((TQ, 1), F32)
    acc = jnp.zeros((TQ, HEAD_DIM), F32)
    for d in range(n_diag):
        carry, acc = block(qi * n_diag + (n_diag - 1 - d), carry, acc, True)

    n_full = qi * n_diag

    def body(it, state):
        return block(n_full - 1 - it, state[0], state[1], False)

    carry, acc = lax.fori_loop(0, n_full, body, (carry, acc))
    o_ref[0] = acc.astype(o_ref.dtype)


def _fox_kernel(q_ref, k_ref, v_ref, cq_ref, ck_ref, o_ref):
    h = pl.program_id(1)
    qi = pl.program_id(2)
    q = q_ref[0]
    n_diag = TQ // TK
    lane = lax.broadcasted_iota(jnp.int32, (TQ, LANES), 1)
    cq = jnp.sum(jnp.where(lane == h, cq_ref[0], 0.0), axis=1, keepdims=True)
    t_idx = qi * TQ + lax.broadcasted_iota(jnp.int32, (TQ, TK), 0)
    s_loc = lax.broadcasted_iota(jnp.int32, (TQ, TK), 1)

    def block(kb, m, l, acc, masked):
        k0 = pl.multiple_of(kb * TK, TK)
        k = k_ref[0, pl.ds(k0, TK), :]
        v = v_ref[0, pl.ds(k0, TK), :]
        ck = ck_ref[0, pl.ds(h, 1), pl.ds(k0, TK)]
        s = _qk(q, k) * ATTN_SCALE + (cq - ck)
        if masked:
            s = jnp.where((k0 + s_loc) <= t_idx, s, NEG_BIG)
        m_new = jnp.maximum(m, jnp.max(s, axis=1, keepdims=True))
        alpha = jnp.exp(m - m_new)
        p = jnp.exp(s - m_new)
        l = alpha * l + jnp.sum(p, axis=1, keepdims=True)
        acc = alpha * acc + jnp.dot(p.astype(BF16), v, preferred_element_type=F32)
        return m_new, l, acc

    n_full = qi * n_diag

    def body(kb, state):
        return block(kb, state[0], state[1], state[2], False)

    state = (jnp.full((TQ, 1), NEG_BIG, F32), jnp.zeros((TQ, 1), F32),
             jnp.zeros((TQ, HEAD_DIM), F32))
    m, l, acc = lax.fori_loop(0, n_full, body, state)
    for d in range(n_diag):
        m, l, acc = block(n_full + d, m, l, acc, True)
    o_ref[0] = (acc / l).astype(o_ref.dtype)


def _attention(qkv, cum_col, cum_row, batch, seq):
    grid = (batch, N_HEADS, seq // TQ)

    def q_spec(col0):
        return pl.BlockSpec((1, TQ, HEAD_DIM), lambda b, h, i: (b, i, col0 + h))

    def kv_spec(col0):
        return pl.BlockSpec((1, seq, HEAD_DIM), lambda b, h, i: (b, 0, col0 + h))

    out_spec = pl.BlockSpec((1, TQ, HEAD_DIM), lambda b, h, i: (b, i, h))
    out_shape = jax.ShapeDtypeStruct((batch, seq, D_BRANCH), BF16)
    sem = ("arbitrary", "arbitrary", "arbitrary")

    o_sb = pl.pallas_call(
        _sb_kernel,
        grid=grid,
        in_specs=[q_spec(0), kv_spec(N_HEADS), kv_spec(2 * N_HEADS)],
        out_specs=out_spec,
        out_shape=out_shape,
        compiler_params=_params(sem),
        name="stickbreak_attn",
    )(qkv, qkv, qkv)

    o_fx = pl.pallas_call(
        _fox_kernel,
        grid=grid,
        in_specs=[
            q_spec(3 * N_HEADS), kv_spec(4 * N_HEADS), kv_spec(5 * N_HEADS),
            pl.BlockSpec((1, TQ, LANES), lambda b, h, i: (b, i, 0)),
            pl.BlockSpec((1, N_HEADS, seq), lambda b, h, i: (b, 0, 0)),
        ],
        out_specs=out_spec,
        out_shape=out_shape,
        compiler_params=_params(sem),
        name="forgetting_attn",
    )(qkv, qkv, qkv, cum_col, cum_row)
    return o_sb, o_fx


def _mix_kernel(osb_ref, ofx_ref, gsb_ref, gfx_ref, x_ref, wsb_ref, wfx_ref, wout_ref,
                npost_ref, npre_ref, h_ref, u_ref):
    a = jnp.dot(osb_ref[...], wsb_ref[...], preferred_element_type=F32)
    b = jnp.dot(ofx_ref[...], wfx_ref[...], preferred_element_type=F32)
    merged = gsb_ref[...].astype(F32) * a + gfx_ref[...].astype(F32) * b
    mix = jnp.dot(merged.astype(BF16), wout_ref[...], preferred_element_type=F32)
    h = x_ref[...] + _rms_norm(mix, npost_ref[...])
    h_ref[...] = h
    u_ref[...] = _rms_norm(h, npre_ref[...]).astype(BF16)


def _const_spec(shape):
    return pl.BlockSpec(shape, lambda i: (0,) * len(shape), pipeline_mode=pl.Buffered(1))


def _mix(o_sb, o_fx, gates, x2, w_sb, w_fx, w_out, n_post, n_ffn_pre):
    t, d = x2.shape
    tm = TM_MIX
    return pl.pallas_call(
        _mix_kernel,
        grid=(t // tm,),
        in_specs=[
            pl.BlockSpec((tm, D_BRANCH), lambda i: (i, 0)),
            pl.BlockSpec((tm, D_BRANCH), lambda i: (i, 0)),
            pl.BlockSpec((tm, d), lambda i: (i, 0)),
            pl.BlockSpec((tm, d), lambda i: (i, 1)),
            pl.BlockSpec((tm, d), lambda i: (i, 0)),
            _const_spec((D_BRANCH, d)),
            _const_spec((D_BRANCH, d)),
            _const_spec((d, d)),
            _const_spec((1, d)),
            _const_spec((1, d)),
        ],
        out_specs=[
            pl.BlockSpec((tm, d), lambda i: (i, 0)),
            pl.BlockSpec((tm, d), lambda i: (i, 0)),
        ],
        out_shape=[
            jax.ShapeDtypeStruct((t, d), F32),
            jax.ShapeDtypeStruct((t, d), BF16),
        ],
        compiler_params=_params(("arbitrary",)),
        name="merge_out_proj",
    )(o_sb, o_fx, gates, gates, x2, w_sb, w_fx, w_out, n_post, n_ffn_pre)


def _ffn_kernel(u_ref, h_ref, wg_ref, wu_ref, wd_ref, npost_ref, o_ref, acc_ref):
    j = pl.program_id(1)

    @pl.when(j == 0)
    def _():
        acc_ref[...] = jnp.zeros_like(acc_ref)

    u = u_ref[...]
    g = jnp.dot(u, wg_ref[...], preferred_element_type=F32)
    up = jnp.dot(u, wu_ref[...], preferred_element_type=F32)
    hidden = (g * _sigmoid(g)) * up
    acc_ref[...] += jnp.dot(hidden.astype(BF16), wd_ref[...], preferred_element_type=F32)

    @pl.when(j == pl.num_programs(1) - 1)
    def _():
        o_ref[...] = h_ref[...] + _rms_norm(acc_ref[...], npost_ref[...])


def _ffn(u2, h1, w_gate, w_up, w_down, n_post):
    t, d = h1.shape
    f = w_gate.shape[1]
    tm, tf = TM_FFN, TF_FFN
    return pl.pallas_call(
        _ffn_kernel,
        grid=(t // tm, f // tf),
        in_specs=[
            pl.BlockSpec((tm, d), lambda i, j: (i, 0)),
            pl.BlockSpec((tm, d), lambda i, j: (i, 0)),
            pl.BlockSpec((d, tf), lambda i, j: (0, j)),
            pl.BlockSpec((d, tf), lambda i, j: (0, j)),
            pl.BlockSpec((tf, d), lambda i, j: (j, 0)),
            pl.BlockSpec((1, d), lambda i, j: (0, 0)),
        ],
        out_specs=pl.BlockSpec((tm, d), lambda i, j: (i, 0)),
        out_shape=jax.ShapeDtypeStruct((t, d), F32),
        scratch_shapes=[pltpu.VMEM((tm, d), F32)],
        compiler_params=_params(("arbitrary", "arbitrary")),
        name="swiglu_ffn",
    )(u2, h1, w_gate, w_up, w_down, n_post)


def kernel(x, norm_mix_pre, norm_mix_post, w_in, b_forget, w_branch_sb, w_branch_fox, w_out,
           norm_ffn_pre, norm_ffn_post, w_ffn_gate, w_ffn_up, w_ffn_down):
    batch, seq, d = x.shape
    depth = w_in.shape[0]
    n_qkv = 6 * D_BRANCH
    h = x.reshape(batch * seq, d)
    for l in range(depth):
        w_l = w_in[l]
        w_qkv = w_l[:, :n_qkv].astype(BF16)
        w_gates = w_l[:, n_qkv + N_HEADS:].astype(BF16)
        w_f = jnp.pad(w_l[:, n_qkv:n_qkv + N_HEADS], ((0, 0), (0, LANES - N_HEADS))).astype(BF16)
        b_f = jnp.pad(b_forget[l], (0, LANES - N_HEADS)).reshape(1, LANES)
        g_pre = norm_mix_pre[l].reshape(1, d)

        qkv = _qkv_proj(h, g_pre, w_qkv)
        gates, cum_col, cum_row = _gates_proj(h, g_pre, w_gates, w_f, b_f, seq)
        o_sb, o_fx = _attention(qkv.reshape(batch, seq, n_qkv),
                                cum_col.reshape(batch, seq, LANES), cum_row, batch, seq)
        h1, u2 = _mix(o_sb.reshape(batch * seq, D_BRANCH), o_fx.reshape(batch * seq, D_BRANCH),
                      gates, h,
                      w_branch_sb[l].astype(BF16), w_branch_fox[l].astype(BF16),
                      w_out[l].astype(BF16),
                      norm_mix_post[l].reshape(1, d), norm_ffn_pre[l].reshape(1, d))
        h = _ffn(u2, h1, w_ffn_gate[l].astype(BF16), w_ffn_up[l].astype(BF16),
                 w_ffn_down[l].astype(BF16), norm_ffn_post[l].reshape(1, d))
    return h.reshape(batch, seq, d)
```

```python
import functools

import jax
import jax.numpy as jnp
from jax import lax
from jax.experimental import pallas as pl
from jax.experimental.pallas import tpu as pltpu

F32 = jnp.float32
BF16 = jnp.bfloat16

D_MODEL = 2048
HEAD_DIM = 128
N_HEADS = 8
D_BRANCH = N_HEADS * HEAD_DIM
D_FF = 5632
RMS_EPS = 1e-6
ATTN_SCALE = HEAD_DIM ** -0.5
NEG_BIG = -1e30
LOG2E = 1.4426950408889634

LANES = 128
MXU_DIM = 256
VMEM_LIMIT = 56 * 1024 * 1024

TM_PROJ = 1024
TN_PROJ = 1024
TQ = 2048
TK = 256
TM_MIX = 256
TM_FFN = 512
TF_FFN = 512
CUM_CHUNK = 256


def _params(semantics):
    return pltpu.CompilerParams(dimension_semantics=semantics, vmem_limit_bytes=VMEM_LIMIT)


def _rms_norm(x, g):
    ms = jnp.mean(x * x, axis=-1, keepdims=True)
    return (x * lax.rsqrt(ms + RMS_EPS)) * g


def _log_sigmoid(x):
    return jnp.minimum(x, 0.0) - jnp.log1p(jnp.exp(-jnp.abs(x)))


def _sigmoid(x):
    return 1.0 / (1.0 + jnp.exp(-x))


def _split3(x):
    hi = x.astype(BF16)
    r1 = x - hi.astype(F32)
    mid = r1.astype(BF16)
    lo = (r1 - mid.astype(F32)).astype(BF16)
    return hi, mid, lo


def _qkv_kernel(x_ref, g_ref, w_ref, cs_ref, o_ref, u_ref):
    @pl.when(pl.program_id(1) == 0)
    def _():
        u_ref[...] = _rms_norm(x_ref[...], g_ref[...]).astype(BF16)

    acc = jnp.dot(u_ref[...], w_ref[...], preferred_element_type=F32)
    o_ref[...] = (acc * cs_ref[...]).astype(o_ref.dtype)


def _qkv_proj(x2, g, w, col_scale):
    t, d = x2.shape
    n = w.shape[1]
    return pl.pallas_call(
        _qkv_kernel,
        grid=(t // TM_PROJ, n // TN_PROJ),
        in_specs=[
            pl.BlockSpec((TM_PROJ, d), lambda i, j: (i, 0)),
            pl.BlockSpec((1, d), lambda i, j: (0, 0)),
            pl.BlockSpec((d, TN_PROJ), lambda i, j: (0, j)),
            pl.BlockSpec((1, TN_PROJ), lambda i, j: (0, j)),
        ],
        out_specs=pl.BlockSpec((TM_PROJ, TN_PROJ), lambda i, j: (i, j)),
        out_shape=jax.ShapeDtypeStruct((t, n), BF16),
        scratch_shapes=[pltpu.VMEM((TM_PROJ, d), BF16)],
        compiler_params=_params(("arbitrary", "arbitrary")),
        name="qkv_proj",
    )(x2, g, w, col_scale)


def _gates_kernel(x_ref, g_ref, w_ref, wf_ref, bf_ref, o_ref, cumc_ref, cumr_ref,
                  u_ref, carry_ref, *, tiles_per_seq):
    i = pl.program_id(0)

    @pl.when(pl.program_id(1) == 0)
    def _():
        u_ref[...] = _rms_norm(x_ref[...], g_ref[...]).astype(BF16)
        f = jnp.dot(u_ref[...], wf_ref[...], preferred_element_type=F32)
        log_f = _log_sigmoid(f + bf_ref[...])

        @pl.when(i % tiles_per_seq == 0)
        def _():
            carry_ref[...] = jnp.zeros_like(carry_ref)

        r = lax.broadcasted_iota(jnp.int32, (CUM_CHUNK, CUM_CHUNK), 0)
        c = lax.broadcasted_iota(jnp.int32, (CUM_CHUNK, CUM_CHUNK), 1)
        lower = (c <= r).astype(BF16)
        carry = carry_ref[...]
        for ch in range(TM_PROJ // CUM_CHUNK):
            blk = log_f[ch * CUM_CHUNK:(ch + 1) * CUM_CHUNK]
            hi, mid, lo = _split3(blk)
            cs = (jnp.dot(lower, hi, preferred_element_type=F32)
                  + jnp.dot(lower, mid, preferred_element_type=F32)
                  + jnp.dot(lower, lo, preferred_element_type=F32)) + carry
            cumc_ref[ch * CUM_CHUNK:(ch + 1) * CUM_CHUNK, :] = cs
            carry = cs[CUM_CHUNK - 1:CUM_CHUNK, :]
        carry_ref[...] = carry
        cumr_ref[0] = cumc_ref[...].T[:N_HEADS, :]

    acc = jnp.dot(u_ref[...], w_ref[...], preferred_element_type=F32)
    o_ref[...] = _sigmoid(acc).astype(o_ref.dtype)


def _gates_proj(x2, g, w, wf, bf, seq):
    t, d = x2.shape
    n = w.shape[1]
    tiles_per_seq = seq // TM_PROJ
    return pl.pallas_call(
        functools.partial(_gates_kernel, tiles_per_seq=tiles_per_seq),
        grid=(t // TM_PROJ, n // TN_PROJ),
        in_specs=[
            pl.BlockSpec((TM_PROJ, d), lambda i, j: (i, 0)),
            pl.BlockSpec((1, d), lambda i, j: (0, 0)),
            pl.BlockSpec((d, TN_PROJ), lambda i, j: (0, j)),
            pl.BlockSpec((d, LANES), lambda i, j: (0, 0)),
            pl.BlockSpec((1, LANES), lambda i, j: (0, 0)),
        ],
        out_specs=[
            pl.BlockSpec((TM_PROJ, TN_PROJ), lambda i, j: (i, j)),
            pl.BlockSpec((TM_PROJ, LANES), lambda i, j: (i, 0)),
            pl.BlockSpec((1, N_HEADS, TM_PROJ),
                         lambda i, j: (i // tiles_per_seq, 0, i % tiles_per_seq)),
        ],
        out_shape=[
            jax.ShapeDtypeStruct((t, n), BF16),
            jax.ShapeDtypeStruct((t, LANES), F32),
            jax.ShapeDtypeStruct((t // seq, N_HEADS, seq), F32),
        ],
        scratch_shapes=[pltpu.VMEM((TM_PROJ, d), BF16), pltpu.VMEM((1, LANES), F32)],
        compiler_params=_params(("arbitrary", "arbitrary")),
        name="gates_proj",
    )(x2, g, w, wf, bf)


def _qk(q, k):
    return lax.dot_general(q, k, (((1,), (1,)), ((), ())), preferred_element_type=F32)


def _set_rows_from(full, r0, new):
    return new if r0 == 0 else jnp.concatenate([full[:r0], new], axis=0)


def _set_cols_from(full, c0, new):
    return new if c0 == 0 else jnp.concatenate([full[:, :c0], new], axis=1)


def _sb_kernel(q_ref, k_ref, v_ref, o_ref, *, single_tile):
    qi = 0 if single_tile else pl.program_id(2)
    n_diag = TQ // TK
    r = lax.broadcasted_iota(jnp.int32, (TK, TK), 0)
    c = lax.broadcasted_iota(jnp.int32, (TK, TK), 1)
    later = (r > c).astype(BF16)
    later2 = jnp.concatenate([later, later], axis=0)

    def block(kb, r0, carry, acc, masked):
        rows = TQ - r0
        k0 = kb * TK if isinstance(kb, int) else pl.multiple_of(kb * TK, TK)
        q = q_ref[0, r0:, :]
        k = k_ref[0, pl.ds(k0, TK), :]
        v = v_ref[0, pl.ds(k0, TK), :]
        z = _qk(q, k)
        sp = jnp.log2(1.0 + jnp.exp2(-jnp.abs(z)))
        log_beta = jnp.minimum(z, 0.0) - sp
        log_keep = log_beta - z
        if masked:
            valid = (lax.broadcasted_iota(jnp.int32, (rows, TK), 1)
                     < lax.broadcasted_iota(jnp.int32, (rows, TK), 0))
            log_keep = jnp.where(valid, log_keep, 0.0)
        hi = log_keep.astype(BF16)
        lo = (log_keep - hi.astype(F32)).astype(BF16)
        between = jnp.dot(jnp.concatenate([hi, lo], axis=1), later2,
                          preferred_element_type=F32)
        w = jnp.exp2(log_beta + between + carry)
        if masked:
            w = jnp.where(valid, w, 0.0)
        acc = acc + jnp.dot(w.astype(BF16), v, preferred_element_type=F32)
        carry = carry + jnp.sum(log_keep, axis=1, keepdims=True)
        return carry, acc

    carry = jnp.zeros((TQ, 1), F32)
    acc = jnp.zeros((TQ, HEAD_DIM), F32)
    for d in reversed(range(n_diag)):
        r0 = d * TK
        c_new, a_new = block(qi * n_diag + d, r0, carry[r0:], acc[r0:], True)
        carry = _set_rows_from(carry, r0, c_new)
        acc = _set_rows_from(acc, r0, a_new)

    n_full = qi * n_diag

    def body(it, state):
        return block(n_full - 1 - it, 0, state[0], state[1], False)

    if not single_tile:
        carry, acc = lax.fori_loop(0, n_full, body, (carry, acc))
    o_ref[0] = acc.astype(o_ref.dtype)


def _fox_kernel(q_ref, k_ref, v_ref, cq_ref, ck_ref, o_ref, *, single_tile):
    h = pl.program_id(1)
    qi = 0 if single_tile else pl.program_id(2)
    n_diag = TQ // TK
    lane = lax.broadcasted_iota(jnp.int32, (TK, LANES), 1)
    cq = cq_ref[0, pl.ds(h, 1), :] * LOG2E

    def block(kb, c0, m, l, acc, masked):
        cols = TQ - c0
        k0 = kb * TK if isinstance(kb, int) else pl.multiple_of(kb * TK, TK)
        q = q_ref[0, c0:, :]
        k = k_ref[0, pl.ds(k0, TK), :]
        v = v_ref[0, pl.ds(k0, TK), :]
        ck = jnp.sum(jnp.where(lane == h, ck_ref[0, pl.ds(k0, TK), :], 0.0),
                     axis=1, keepdims=True) * LOG2E
        y = _qk(k, q) - ck
        if masked:
            valid = (lax.broadcasted_iota(jnp.int32, (TK, cols), 0)
                     <= lax.broadcasted_iota(jnp.int32, (TK, cols), 1))
            y = jnp.where(valid, y, NEG_BIG)
        cq_c = cq[:, c0:]
        m_new = jnp.maximum(m, cq_c + jnp.max(y, axis=0, keepdims=True))
        alpha = jnp.exp2(m - m_new)
        p = jnp.exp2(y - (m_new - cq_c))
        l = alpha * l + jnp.sum(p, axis=0, keepdims=True)
        pv = lax.dot_general(v, p.astype(BF16), (((0,), (0,)), ((), ())),
                             preferred_element_type=F32)
        return m_new, l, alpha * acc + pv

    n_full = qi * n_diag

    def body(kb, state):
        return block(kb, 0, state[0], state[1], state[2], False)

    state = (jnp.full((1, TQ), NEG_BIG, F32), jnp.zeros((1, TQ), F32),
             jnp.zeros((HEAD_DIM, TQ), F32))
    m, l, acc = state if single_tile else lax.fori_loop(0, n_full, body, state)
    for d in range(n_diag):
        c0 = d * TK
        m_new, l_new, a_new = block(n_full + d, c0, m[:, c0:], l[:, c0:], acc[:, c0:], True)
        m = _set_cols_from(m, c0, m_new)
        l = _set_cols_from(l, c0, l_new)
        acc = _set_cols_from(acc, c0, a_new)
    o_ref[0] = (acc / l).T.astype(o_ref.dtype)


def _attention(qkv, cum_col, cum_row, batch, seq):
    grid = (batch, N_HEADS, seq // TQ)

    def q_spec(col0):
        return pl.BlockSpec((1, TQ, HEAD_DIM), lambda b, h, i: (b, i, col0 + h))

    def kv_spec(col0):
        return pl.BlockSpec((1, seq, HEAD_DIM), lambda b, h, i: (b, 0, col0 + h))

    out_spec = pl.BlockSpec((1, TQ, HEAD_DIM), lambda b, h, i: (b, i, h))
    out_shape = jax.ShapeDtypeStruct((batch, seq, D_BRANCH), BF16)
    sem = ("arbitrary", "arbitrary", "arbitrary")

    o_sb = pl.pallas_call(
        functools.partial(_sb_kernel, single_tile=seq == TQ),
        grid=grid,
        in_specs=[q_spec(0), kv_spec(N_HEADS), kv_spec(2 * N_HEADS)],
        out_specs=out_spec,
        out_shape=out_shape,
        compiler_params=_params(sem),
        name="stickbreak_attn",
    )(qkv, qkv, qkv)

    o_fx = pl.pallas_call(
        functools.partial(_fox_kernel, single_tile=seq == TQ),
        grid=grid,
        in_specs=[
            q_spec(3 * N_HEADS), kv_spec(4 * N_HEADS), kv_spec(5 * N_HEADS),
            pl.BlockSpec((1, N_HEADS, TQ), lambda b, h, i: (b, 0, i)),
            pl.BlockSpec((1, seq, LANES), lambda b, h, i: (b, 0, 0)),
        ],
        out_specs=out_spec,
        out_shape=out_shape,
        compiler_params=_params(sem),
        name="forgetting_attn",
    )(qkv, qkv, qkv, cum_row, cum_col)
    return o_sb, o_fx


def _mix_kernel(osb_ref, ofx_ref, gsb_ref, gfx_ref, x_ref, wsb_ref, wfx_ref, wout_ref,
                npost_ref, npre_ref, h_ref, u_ref):
    a = jnp.dot(osb_ref[...], wsb_ref[...], preferred_element_type=F32)
    b = jnp.dot(ofx_ref[...], wfx_ref[...], preferred_element_type=F32)
    merged = gsb_ref[...].astype(F32) * a + gfx_ref[...].astype(F32) * b
    mix = jnp.dot(merged.astype(BF16), wout_ref[...], preferred_element_type=F32)
    h = x_ref[...] + _rms_norm(mix, npost_ref[...])
    h_ref[...] = h
    u_ref[...] = _rms_norm(h, npre_ref[...]).astype(BF16)


def _const_spec(shape):
    return pl.BlockSpec(shape, lambda i: (0,) * len(shape), pipeline_mode=pl.Buffered(1))


def _mix(o_sb, o_fx, gates, x2, w_sb, w_fx, w_out, n_post, n_ffn_pre):
    t, d = x2.shape
    tm = TM_MIX
    return pl.pallas_call(
        _mix_kernel,
        grid=(t // tm,),
        in_specs=[
            pl.BlockSpec((tm, D_BRANCH), lambda i: (i, 0)),
            pl.BlockSpec((tm, D_BRANCH), lambda i: (i, 0)),
            pl.BlockSpec((tm, d), lambda i: (i, 0)),
            pl.BlockSpec((tm, d), lambda i: (i, 1)),
            pl.BlockSpec((tm, d), lambda i: (i, 0)),
            _const_spec((D_BRANCH, d)),
            _const_spec((D_BRANCH, d)),
            _const_spec((d, d)),
            _const_spec((1, d)),
            _const_spec((1, d)),
        ],
        out_specs=[
            pl.BlockSpec((tm, d), lambda i: (i, 0)),
            pl.BlockSpec((tm, d), lambda i: (i, 0)),
        ],
        out_shape=[
            jax.ShapeDtypeStruct((t, d), F32),
            jax.ShapeDtypeStruct((t, d), BF16),
        ],
        compiler_params=_params(("arbitrary",)),
        name="merge_out_proj",
    )(o_sb, o_fx, gates, gates, x2, w_sb, w_fx, w_out, n_post, n_ffn_pre)


def _ffn_kernel(u_ref, h_ref, wg_ref, wu_ref, wd_ref, npost_ref, o_ref, acc_ref):
    j = pl.program_id(1)

    @pl.when(j == 0)
    def _():
        acc_ref[...] = jnp.zeros_like(acc_ref)

    u = u_ref[...]
    g = jnp.dot(u, wg_ref[...], preferred_element_type=F32)
    up = jnp.dot(u, wu_ref[...], preferred_element_type=F32)
    hidden = (g * _sigmoid(g)) * up
    acc_ref[...] += jnp.dot(hidden.astype(BF16), wd_ref[...], preferred_element_type=F32)

    @pl.when(j == pl.num_programs(1) - 1)
    def _():
        o_ref[...] = h_ref[...] + _rms_norm(acc_ref[...], npost_ref[...])


def _ffn(u2, h1, w_gate, w_up, w_down, n_post):
    t, d = h1.shape
    f = w_gate.shape[1]
    tm, tf = TM_FFN, TF_FFN
    return pl.pallas_call(
        _ffn_kernel,
        grid=(t // tm, f // tf),
        in_specs=[
            pl.BlockSpec((tm, d), lambda i, j: (i, 0)),
            pl.BlockSpec((tm, d), lambda i, j: (i, 0)),
            pl.BlockSpec((d, tf), lambda i, j: (0, j)),
            pl.BlockSpec((d, tf), lambda i, j: (0, j)),
            pl.BlockSpec((tf, d), lambda i, j: (j, 0)),
            pl.BlockSpec((1, d), lambda i, j: (0, 0)),
        ],
        out_specs=pl.BlockSpec((tm, d), lambda i, j: (i, 0)),
        out_shape=jax.ShapeDtypeStruct((t, d), F32),
        scratch_shapes=[pltpu.VMEM((tm, d), F32)],
        compiler_params=_params(("arbitrary", "arbitrary")),
        name="swiglu_ffn",
    )(u2, h1, w_gate, w_up, w_down, n_post)


def kernel(x, norm_mix_pre, norm_mix_post, w_in, b_forget, w_branch_sb, w_branch_fox, w_out,
           norm_ffn_pre, norm_ffn_post, w_ffn_gate, w_ffn_up, w_ffn_down):
    batch, seq, d = x.shape
    depth = w_in.shape[0]
    n_qkv = 6 * D_BRANCH
    h = x.reshape(batch * seq, d)
    for l in range(depth):
        w_l = w_in[l]
        w_qkv = w_l[:, :n_qkv].astype(BF16)
        w_gates = w_l[:, n_qkv + N_HEADS:].astype(BF16)
        w_f = jnp.pad(w_l[:, n_qkv:n_qkv + N_HEADS], ((0, 0), (0, LANES - N_HEADS))).astype(BF16)
        b_f = jnp.pad(b_forget[l], (0, LANES - N_HEADS)).reshape(1, LANES)
        g_pre = norm_mix_pre[l].reshape(1, d)

        q_scale = jnp.full((D_BRANCH,), ATTN_SCALE * LOG2E, F32)
        ones = jnp.ones((D_BRANCH,), F32)
        col_scale = jnp.concatenate([q_scale, ones, ones, q_scale, ones, ones]).reshape(1, n_qkv)
        qkv = _qkv_proj(h, g_pre, w_qkv, col_scale)
        gates, cum_col, cum_row = _gates_proj(h, g_pre, w_gates, w_f, b_f, seq)
        o_sb, o_fx = _attention(qkv.reshape(batch, seq, n_qkv),
                                cum_col.reshape(batch, seq, LANES), cum_row, batch, seq)
        h1, u2 = _mix(o_sb.reshape(batch * seq, D_BRANCH), o_fx.reshape(batch * seq, D_BRANCH),
                      gates, h,
                      w_branch_sb[l].astype(BF16), w_branch_fox[l].astype(BF16),
                      w_out[l].astype(BF16),
                      norm_mix_post[l].reshape(1, d), norm_ffn_pre[l].reshape(1, d))
        h = _ffn(u2, h1, w_ffn_gate[l].astype(BF16), w_ffn_up[l].astype(BF16),
                 w_ffn_down[l].astype(BF16), norm_ffn_post[l].reshape(1, d))
    return h.reshape(batch, seq, d)
```

```python
import functools

import jax
import jax.numpy as jnp
from jax import lax
from jax.experimental import pallas as pl
from jax.experimental.pallas import tpu as pltpu

F32 = jnp.float32
BF16 = jnp.bfloat16

D_MODEL = 2048
HEAD_DIM = 128
N_HEADS = 8
D_BRANCH = N_HEADS * HEAD_DIM
D_FF = 5632
RMS_EPS = 1e-6
ATTN_SCALE = HEAD_DIM ** -0.5
NEG_BIG = -1e30
LOG2E = 1.4426950408889634

LANES = 128
MXU_DIM = 256
VMEM_LIMIT = 56 * 1024 * 1024

TM_PROJ = 1024
TN_PROJ = 1024
TK = 256
QC = 256
HEADS_PER_STEP = 1
LOOKAHEAD = 4
RC = 256
SB_LOOKAHEAD_QK = 4
SB_LOOKAHEAD_CUM = 2
TM_MIX = 256
TM_FFN = 512
TF_FFN = 512
CUM_CHUNK = 256


def _params(semantics, flags=None):
    return pltpu.CompilerParams(dimension_semantics=semantics, vmem_limit_bytes=VMEM_LIMIT,
                                flags=flags)


def _rms_norm(x, g):
    ms = jnp.mean(x * x, axis=-1, keepdims=True)
    return (x * lax.rsqrt(ms + RMS_EPS)) * g


def _log_sigmoid(x):
    return jnp.minimum(x, 0.0) - jnp.log1p(jnp.exp(-jnp.abs(x)))


def _sigmoid(x):
    return 1.0 / (1.0 + jnp.exp(-x))


def _split3(x):
    hi = x.astype(BF16)
    r1 = x - hi.astype(F32)
    mid = r1.astype(BF16)
    lo = (r1 - mid.astype(F32)).astype(BF16)
    return hi, mid, lo


def _qkv_kernel(x_ref, g_ref, w_ref, cs_ref, o_ref, u_ref):
    @pl.when(pl.program_id(1) == 0)
    def _():
        u_ref[...] = _rms_norm(x_ref[...], g_ref[...]).astype(BF16)

    acc = jnp.dot(u_ref[...], w_ref[...], preferred_element_type=F32)
    o_ref[...] = (acc * cs_ref[...]).astype(o_ref.dtype)


def _qkv_proj(x2, g, w, col_scale):
    t, d = x2.shape
    n = w.shape[1]
    return pl.pallas_call(
        _qkv_kernel,
        grid=(t // TM_PROJ, n // TN_PROJ),
        in_specs=[
            pl.BlockSpec((TM_PROJ, d), lambda i, j: (i, 0)),
            pl.BlockSpec((1, d), lambda i, j: (0, 0)),
            pl.BlockSpec((d, TN_PROJ), lambda i, j: (0, j)),
            pl.BlockSpec((1, TN_PROJ), lambda i, j: (0, j)),
        ],
        out_specs=pl.BlockSpec((TM_PROJ, TN_PROJ), lambda i, j: (i, j)),
        out_shape=jax.ShapeDtypeStruct((t, n), BF16),
        scratch_shapes=[pltpu.VMEM((TM_PROJ, d), BF16)],
        compiler_params=_params(("arbitrary", "arbitrary")),
        name="qkv_proj",
    )(x2, g, w, col_scale)


def _gates_kernel(x_ref, g_ref, w_ref, wf_ref, bf_ref, o_ref, cumc_ref, cumr_ref,
                  u_ref, carry_ref, *, tiles_per_seq):
    i = pl.program_id(0)

    @pl.when(pl.program_id(1) == 0)
    def _():
        u_ref[...] = _rms_norm(x_ref[...], g_ref[...]).astype(BF16)
        f = jnp.dot(u_ref[...], wf_ref[...], preferred_element_type=F32)
        log_f = _log_sigmoid(f + bf_ref[...])

        @pl.when(i % tiles_per_seq == 0)
        def _():
            carry_ref[...] = jnp.zeros_like(carry_ref)

        r = lax.broadcasted_iota(jnp.int32, (CUM_CHUNK, CUM_CHUNK), 0)
        c = lax.broadcasted_iota(jnp.int32, (CUM_CHUNK, CUM_CHUNK), 1)
        lower = (c <= r).astype(BF16)
        carry = carry_ref[...]
        for ch in range(TM_PROJ // CUM_CHUNK):
            blk = log_f[ch * CUM_CHUNK:(ch + 1) * CUM_CHUNK]
            hi, mid, lo = _split3(blk)
            cs = (jnp.dot(lower, hi, preferred_element_type=F32)
                  + jnp.dot(lower, mid, preferred_element_type=F32)
                  + jnp.dot(lower, lo, preferred_element_type=F32)) + carry
            cumc_ref[ch * CUM_CHUNK:(ch + 1) * CUM_CHUNK, :] = cs
            carry = cs[CUM_CHUNK - 1:CUM_CHUNK, :]
        carry_ref[...] = carry
        cumr_ref[0] = cumc_ref[...].T[:N_HEADS, :]

    acc = jnp.dot(u_ref[...], w_ref[...], preferred_element_type=F32)
    o_ref[...] = _sigmoid(acc).astype(o_ref.dtype)


def _gates_proj(x2, g, w, wf, bf, seq):
    t, d = x2.shape
    n = w.shape[1]
    tiles_per_seq = seq // TM_PROJ
    return pl.pallas_call(
        functools.partial(_gates_kernel, tiles_per_seq=tiles_per_seq),
        grid=(t // TM_PROJ, n // TN_PROJ),
        in_specs=[
            pl.BlockSpec((TM_PROJ, d), lambda i, j: (i, 0)),
            pl.BlockSpec((1, d), lambda i, j: (0, 0)),
            pl.BlockSpec((d, TN_PROJ), lambda i, j: (0, j)),
            pl.BlockSpec((d, LANES), lambda i, j: (0, 0)),
            pl.BlockSpec((1, LANES), lambda i, j: (0, 0)),
        ],
        out_specs=[
            pl.BlockSpec((TM_PROJ, TN_PROJ), lambda i, j: (i, j)),
            pl.BlockSpec((TM_PROJ, LANES), lambda i, j: (i, 0)),
            pl.BlockSpec((1, N_HEADS, TM_PROJ),
                         lambda i, j: (i // tiles_per_seq, 0, i % tiles_per_seq)),
        ],
        out_shape=[
            jax.ShapeDtypeStruct((t, n), BF16),
            jax.ShapeDtypeStruct((t, LANES), F32),
            jax.ShapeDtypeStruct((t // seq, N_HEADS, seq), F32),
        ],
        scratch_shapes=[pltpu.VMEM((TM_PROJ, d), BF16), pltpu.VMEM((1, LANES), F32)],
        compiler_params=_params(("arbitrary", "arbitrary")),
        name="gates_proj",
    )(x2, g, w, wf, bf)


def _qk(q, k):
    return lax.dot_general(q, k, (((1,), (1,)), ((), ())), preferred_element_type=F32)


def _neg_abs(x):
    bits = lax.bitcast_convert_type(x, jnp.uint32) | jnp.uint32(0x80000000)
    return lax.bitcast_convert_type(bits, F32)


def _set_rows_from(full, r0, new):
    return new if r0 == 0 else jnp.concatenate([full[:r0], new], axis=0)


def _set_cols_from(full, c0, new):
    return new if c0 == 0 else jnp.concatenate([full[:, :c0], new], axis=1)


def _head_cols(hh):
    return slice(hh * HEAD_DIM, (hh + 1) * HEAD_DIM)


def _sb_kernel(q_ref, k_ref, v_ref, o_ref, *, seq):
    r = lax.broadcasted_iota(jnp.int32, (TK, TK), 0)
    c = lax.broadcasted_iota(jnp.int32, (TK, TK), 1)
    later = (r > c).astype(BF16)
    later2 = jnp.concatenate([later, later], axis=0)

    def valid_mask(d, c):
        k0, r0 = d * TK, c * RC
        if k0 + TK <= r0:
            return None
        return (k0 + lax.broadcasted_iota(jnp.int32, (RC, TK), 1)
                < r0 + lax.broadcasted_iota(jnp.int32, (RC, TK), 0))

    def gates(hh, d, c):
        q = q_ref[0, c * RC:(c + 1) * RC, _head_cols(hh)]
        k = k_ref[0, d * TK:(d + 1) * TK, _head_cols(hh)]
        z = _qk(q, k)
        sp = jnp.log2(1.0 + jnp.exp2(_neg_abs(z)))
        log_beta = jnp.minimum(z, 0.0) - sp
        log_keep = log_beta - z
        valid = valid_mask(d, c)
        if valid is not None:
            log_keep = jnp.where(valid, log_keep, 0.0)
        hi = log_keep.astype(BF16)
        lo = (log_keep - hi.astype(F32)).astype(BF16)
        return log_beta, jnp.concatenate([hi, lo], axis=1), jnp.sum(log_keep, axis=1, keepdims=True)

    def within_block(log_beta, hilo, row_sum):
        return log_beta + jnp.dot(hilo, later2, preferred_element_type=F32), row_sum

    def accumulate(hh, d, c, log_w, row_sum, carry, acc):
        v = v_ref[0, d * TK:(d + 1) * TK, _head_cols(hh)]
        w = jnp.exp2(log_w + carry)
        valid = valid_mask(d, c)
        if valid is not None:
            w = jnp.where(valid, w, 0.0)
        return carry + row_sum, acc + jnp.dot(w.astype(BF16), v, preferred_element_type=F32)

    n_chunks = seq // RC
    state = {(hh, c): (jnp.zeros((RC, 1), F32), jnp.zeros((RC, HEAD_DIM), F32))
             for hh in range(HEADS_PER_STEP) for c in range(n_chunks)}
    items = [(hh, d, c) for d in reversed(range(seq // TK)) for c in range(d * TK // RC, n_chunks)
             for hh in range(HEADS_PER_STEP)]
    n = len(items)
    staged_a, staged_b = {}, {}
    for step in range(-SB_LOOKAHEAD_QK, n):
        i_a, i_b = step + SB_LOOKAHEAD_QK, step + SB_LOOKAHEAD_CUM
        if i_a < n:
            staged_a[i_a] = gates(*items[i_a])
        if 0 <= i_b < n:
            staged_b[i_b] = within_block(*staged_a.pop(i_b))
        if step >= 0:
            hh, d, c = items[step]
            state[hh, c] = accumulate(hh, d, c, *staged_b.pop(step), *state[hh, c])
    for hh in range(HEADS_PER_STEP):
        for c in range(n_chunks):
            o_ref[0, c * RC:(c + 1) * RC, _head_cols(hh)] = state[hh, c][1].astype(o_ref.dtype)


def _fox_kernel(q_ref, k_ref, v_ref, cq_ref, ck_ref, o_ref, *, seq):
    h0 = pl.program_id(1) * HEADS_PER_STEP
    lane = lax.broadcasted_iota(jnp.int32, (TK, LANES), 1)

    def scores(hh, d, c):
        k0, c0 = d * TK, c * QC
        q = q_ref[0, c0:c0 + QC, _head_cols(hh)]
        k = k_ref[0, k0:k0 + TK, _head_cols(hh)]
        ck = jnp.sum(jnp.where(lane == h0 + hh, ck_ref[0, k0:k0 + TK, :], 0.0),
                     axis=1, keepdims=True) * LOG2E
        y = _qk(k, q) - ck
        if k0 + TK > c0:
            valid = (k0 + lax.broadcasted_iota(jnp.int32, (TK, QC), 0)
                     <= c0 + lax.broadcasted_iota(jnp.int32, (TK, QC), 1))
            y = jnp.where(valid, y, NEG_BIG)
        return y

    def accumulate(hh, d, c, y, m, l, acc):
        k0, c0 = d * TK, c * QC
        v = v_ref[0, k0:k0 + TK, _head_cols(hh)]
        cq = cq_ref[0, pl.ds(h0 + hh, 1), c0:c0 + QC] * LOG2E
        m_new = jnp.maximum(m, cq + jnp.max(y, axis=0, keepdims=True))
        alpha = jnp.exp2(m - m_new)
        p = jnp.exp2(y - (m_new - cq))
        l = alpha * l + jnp.sum(p, axis=0, keepdims=True)
        pv = lax.dot_general(v, p.astype(BF16), (((0,), (0,)), ((), ())),
                             preferred_element_type=F32)
        return m_new, l, alpha * acc + pv

    n_chunks = seq // QC
    state = {(hh, c): (jnp.full((1, QC), NEG_BIG, F32), jnp.zeros((1, QC), F32),
                       jnp.zeros((HEAD_DIM, QC), F32))
             for hh in range(HEADS_PER_STEP) for c in range(n_chunks)}
    items = [(hh, d, c) for d in range(seq // TK) for c in range(d * TK // QC, n_chunks)
             for hh in range(HEADS_PER_STEP)]
    ys = [scores(*item) for item in items[:LOOKAHEAD]]
    for i, (hh, d, c) in enumerate(items):
        if i + LOOKAHEAD < len(items):
            ys.append(scores(*items[i + LOOKAHEAD]))
        state[hh, c] = accumulate(hh, d, c, ys.pop(0), *state[hh, c])
    for hh in range(HEADS_PER_STEP):
        for c in range(n_chunks):
            _, l, acc = state[hh, c]
            o_ref[0, c * QC:(c + 1) * QC, _head_cols(hh)] = (acc / l).T.astype(o_ref.dtype)


def _attention(qkv, cum_col, cum_row, batch, seq):
    groups = N_HEADS // HEADS_PER_STEP
    width = HEADS_PER_STEP * HEAD_DIM
    grid = (batch, groups)

    def head_spec(group0):
        return pl.BlockSpec((1, seq, width), lambda b, g: (b, 0, group0 + g))

    out_shape = jax.ShapeDtypeStruct((batch, seq, D_BRANCH), BF16)
    sem = ("arbitrary", "arbitrary")

    o_sb = pl.pallas_call(
        functools.partial(_sb_kernel, seq=seq),
        grid=grid,
        in_specs=[head_spec(0), head_spec(groups), head_spec(2 * groups)],
        out_specs=head_spec(0),
        out_shape=out_shape,
        compiler_params=_params(sem),
        name="stickbreak_attn",
    )(qkv, qkv, qkv)

    o_fx = pl.pallas_call(
        functools.partial(_fox_kernel, seq=seq),
        grid=grid,
        in_specs=[
            head_spec(3 * groups), head_spec(4 * groups), head_spec(5 * groups),
            pl.BlockSpec((1, N_HEADS, seq), lambda b, g: (b, 0, 0)),
            pl.BlockSpec((1, seq, LANES), lambda b, g: (b, 0, 0)),
        ],
        out_specs=head_spec(0),
        out_shape=out_shape,
        compiler_params=_params(sem),
        name="forgetting_attn",
    )(qkv, qkv, qkv, cum_row, cum_col)
    return o_sb, o_fx


def _mix_kernel(osb_ref, ofx_ref, gsb_ref, gfx_ref, x_ref, wsb_ref, wfx_ref, wout_ref,
                npost_ref, npre_ref, h_ref, u_ref):
    a = jnp.dot(osb_ref[...], wsb_ref[...], preferred_element_type=F32)
    b = jnp.dot(ofx_ref[...], wfx_ref[...], preferred_element_type=F32)
    merged = gsb_ref[...].astype(F32) * a + gfx_ref[...].astype(F32) * b
    mix = jnp.dot(merged.astype(BF16), wout_ref[...], preferred_element_type=F32)
    h = x_ref[...] + _rms_norm(mix, npost_ref[...])
    h_ref[...] = h
    u_ref[...] = _rms_norm(h, npre_ref[...]).astype(BF16)


def _const_spec(shape):
    return pl.BlockSpec(shape, lambda i: (0,) * len(shape), pipeline_mode=pl.Buffered(1))


def _mix(o_sb, o_fx, gates, x2, w_sb, w_fx, w_out, n_post, n_ffn_pre):
    t, d = x2.shape
    tm = TM_MIX
    return pl.pallas_call(
        _mix_kernel,
        grid=(t // tm,),
        in_specs=[
            pl.BlockSpec((tm, D_BRANCH), lambda i: (i, 0)),
            pl.BlockSpec((tm, D_BRANCH), lambda i: (i, 0)),
            pl.BlockSpec((tm, d), lambda i: (i, 0)),
            pl.BlockSpec((tm, d), lambda i: (i, 1)),
            pl.BlockSpec((tm, d), lambda i: (i, 0)),
            _const_spec((D_BRANCH, d)),
            _const_spec((D_BRANCH, d)),
            _const_spec((d, d)),
            _const_spec((1, d)),
            _const_spec((1, d)),
        ],
        out_specs=[
            pl.BlockSpec((tm, d), lambda i: (i, 0)),
            pl.BlockSpec((tm, d), lambda i: (i, 0)),
        ],
        out_shape=[
            jax.ShapeDtypeStruct((t, d), F32),
            jax.ShapeDtypeStruct((t, d), BF16),
        ],
        compiler_params=_params(("arbitrary",)),
        name="merge_out_proj",
    )(o_sb, o_fx, gates, gates, x2, w_sb, w_fx, w_out, n_post, n_ffn_pre)


def _ffn_kernel(u_ref, h_ref, wg_ref, wu_ref, wd_ref, npost_ref, o_ref, acc_ref):
    j = pl.program_id(1)

    @pl.when(j == 0)
    def _():
        acc_ref[...] = jnp.zeros_like(acc_ref)

    u = u_ref[...]
    g = jnp.dot(u, wg_ref[...], preferred_element_type=F32)
    up = jnp.dot(u, wu_ref[...], preferred_element_type=F32)
    hidden = (g * _sigmoid(g)) * up
    acc_ref[...] += jnp.dot(hidden.astype(BF16), wd_ref[...], preferred_element_type=F32)

    @pl.when(j == pl.num_programs(1) - 1)
    def _():
        o_ref[...] = h_ref[...] + _rms_norm(acc_ref[...], npost_ref[...])


def _ffn(u2, h1, w_gate, w_up, w_down, n_post):
    t, d = h1.shape
    f = w_gate.shape[1]
    tm, tf = TM_FFN, TF_FFN
    return pl.pallas_call(
        _ffn_kernel,
        grid=(t // tm, f // tf),
        in_specs=[
            pl.BlockSpec((tm, d), lambda i, j: (i, 0)),
            pl.BlockSpec((tm, d), lambda i, j: (i, 0)),
            pl.BlockSpec((d, tf), lambda i, j: (0, j)),
            pl.BlockSpec((d, tf), lambda i, j: (0, j)),
            pl.BlockSpec((tf, d), lambda i, j: (j, 0)),
            pl.BlockSpec((1, d), lambda i, j: (0, 0)),
        ],
        out_specs=pl.BlockSpec((tm, d), lambda i, j: (i, 0)),
        out_shape=jax.ShapeDtypeStruct((t, d), F32),
        scratch_shapes=[pltpu.VMEM((tm, d), F32)],
        compiler_params=_params(("arbitrary", "arbitrary")),
        name="swiglu_ffn",
    )(u2, h1, w_gate, w_up, w_down, n_post)


def kernel(x, norm_mix_pre, norm_mix_post, w_in, b_forget, w_branch_sb, w_branch_fox, w_out,
           norm_ffn_pre, norm_ffn_post, w_ffn_gate, w_ffn_up, w_ffn_down):
    batch, seq, d = x.shape
    depth = w_in.shape[0]
    n_qkv = 6 * D_BRANCH
    h = x.reshape(batch * seq, d)
    for l in range(depth):
        w_l = w_in[l]
        w_qkv = w_l[:, :n_qkv].astype(BF16)
        w_gates = w_l[:, n_qkv + N_HEADS:].astype(BF16)
        w_f = jnp.pad(w_l[:, n_qkv:n_qkv + N_HEADS], ((0, 0), (0, LANES - N_HEADS))).astype(BF16)
        b_f = jnp.pad(b_forget[l], (0, LANES - N_HEADS)).reshape(1, LANES)
        g_pre = norm_mix_pre[l].reshape(1, d)

        q_scale = jnp.full((D_BRANCH,), ATTN_SCALE * LOG2E, F32)
        ones = jnp.ones((D_BRANCH,), F32)
        col_scale = jnp.concatenate([q_scale, ones, ones, q_scale, ones, ones]).reshape(1, n_qkv)
        qkv = _qkv_proj(h, g_pre, w_qkv, col_scale)
        gates, cum_col, cum_row = _gates_proj(h, g_pre, w_gates, w_f, b_f, seq)
        o_sb, o_fx = _attention(qkv.reshape(batch, seq, n_qkv),
                                cum_col.reshape(batch, seq, LANES), cum_row, batch, seq)
        h1, u2 = _mix(o_sb.reshape(batch * seq, D_BRANCH), o_fx.reshape(batch * seq, D_BRANCH),
                      gates, h,
                      w_branch_sb[l].astype(BF16), w_branch_fox[l].astype(BF16),
                      w_out[l].astype(BF16),
                      norm_mix_post[l].reshape(1, d), norm_ffn_pre[l].reshape(1, d))
        h = _ffn(u2, h1, w_ffn_gate[l].astype(BF16), w_ffn_up[l].astype(BF16),
                 w_ffn_down[l].astype(BF16), norm_ffn_post[l].reshape(1, d))
    return h.reshape(batch, seq, d)
```

```python
import functools

import jax
import jax.numpy as jnp
from jax import lax
from jax.experimental import pallas as pl
from jax.experimental.pallas import tpu as pltpu

F32 = jnp.float32
BF16 = jnp.bfloat16

D_MODEL = 2048
HEAD_DIM = 128
N_HEADS = 8
D_BRANCH = N_HEADS * HEAD_DIM
D_FF = 5632
RMS_EPS = 1e-6
ATTN_SCALE = HEAD_DIM ** -0.5
NEG_BIG = -1e30
LOG2E = 1.4426950408889634

LANES = 128
VMEM_LIMIT = 56 * 1024 * 1024

TM_PROJ = 1024
TN_PROJ = 1024
CUM_CHUNK = 256
TK = 256
RC = 256
QC = 256
SB_LOOKAHEAD_QK = 4
SB_LOOKAHEAD_CUM = 2
FOX_LOOKAHEAD = 4
TM_MIX = 256
TM_FFN = 1024
TF_FFN = 512
DOWN_CHUNK = 512
NORM_ROWS = 256


def _params(semantics):
    return pltpu.CompilerParams(dimension_semantics=semantics, vmem_limit_bytes=VMEM_LIMIT)


def _rms_norm(x, g):
    ms = jnp.mean(x * x, axis=-1, keepdims=True)
    return (x * lax.rsqrt(ms + RMS_EPS)) * g


def _log_sigmoid(x):
    return jnp.minimum(x, 0.0) - jnp.log1p(jnp.exp(-jnp.abs(x)))


def _sigmoid(x):
    return 1.0 / (1.0 + jnp.exp(-x))


def _split3(x):
    hi = x.astype(BF16)
    r1 = x - hi.astype(F32)
    mid = r1.astype(BF16)
    lo = (r1 - mid.astype(F32)).astype(BF16)
    return hi, mid, lo


def _qkv_kernel(x_ref, g_ref, w_ref, cs_ref, o_ref, u_ref):
    @pl.when(pl.program_id(1) == 0)
    def _():
        u_ref[...] = _rms_norm(x_ref[...], g_ref[...]).astype(BF16)

    acc = jnp.dot(u_ref[...], w_ref[...], preferred_element_type=F32)
    o_ref[...] = (acc * cs_ref[...]).astype(o_ref.dtype)


def _qkv_proj(x2, g, w, col_scale):
    t, d = x2.shape
    n = col_scale.shape[1]
    return pl.pallas_call(
        _qkv_kernel,
        grid=(t // TM_PROJ, n // TN_PROJ),
        in_specs=[
            pl.BlockSpec((TM_PROJ, d), lambda i, j: (i, 0)),
            pl.BlockSpec((1, d), lambda i, j: (0, 0)),
            pl.BlockSpec((d, TN_PROJ), lambda i, j: (0, j)),
            pl.BlockSpec((1, TN_PROJ), lambda i, j: (0, j)),
        ],
        out_specs=[
            pl.BlockSpec((TM_PROJ, TN_PROJ), lambda i, j: (i, j)),
            pl.BlockSpec((TM_PROJ, d), lambda i, j: (i, 0)),
        ],
        out_shape=[
            jax.ShapeDtypeStruct((t, n), BF16),
            jax.ShapeDtypeStruct((t, d), BF16),
        ],
        compiler_params=_params(("arbitrary", "arbitrary")),
        name="qkv_proj",
    )(x2, g, w, col_scale)


def _gates_kernel(u_ref, w_ref, wf_ref, bf_ref, o_ref, cumc_ref, cumr_ref, carry_ref,
                  *, tiles_per_seq):
    i = pl.program_id(0)

    @pl.when(pl.program_id(1) == 0)
    def _():
        f = jnp.dot(u_ref[...], wf_ref[...], preferred_element_type=F32)
        log_f = _log_sigmoid(f + bf_ref[...])

        @pl.when(i % tiles_per_seq == 0)
        def _():
            carry_ref[...] = jnp.zeros_like(carry_ref)

        r = lax.broadcasted_iota(jnp.int32, (CUM_CHUNK, CUM_CHUNK), 0)
        c = lax.broadcasted_iota(jnp.int32, (CUM_CHUNK, CUM_CHUNK), 1)
        lower = (c <= r).astype(BF16)
        carry = carry_ref[...]
        for ch in range(TM_PROJ // CUM_CHUNK):
            blk = log_f[ch * CUM_CHUNK:(ch + 1) * CUM_CHUNK]
            hi, mid, lo = _split3(blk)
            cs = (jnp.dot(lower, hi, preferred_element_type=F32)
                  + jnp.dot(lower, mid, preferred_element_type=F32)
                  + jnp.dot(lower, lo, preferred_element_type=F32)) + carry
            cumc_ref[ch * CUM_CHUNK:(ch + 1) * CUM_CHUNK, :] = cs
            carry = cs[CUM_CHUNK - 1:CUM_CHUNK, :]
        carry_ref[...] = carry
        cumr_ref[0] = cumc_ref[...].T[:N_HEADS, :]

    acc = jnp.dot(u_ref[...], w_ref[...], preferred_element_type=F32)
    o_ref[...] = _sigmoid(acc).astype(o_ref.dtype)


def _gates_proj(u, w, w_all, f_block, bf, seq):
    t, d = u.shape
    n = w.shape[1]
    tiles_per_seq = seq // TM_PROJ
    return pl.pallas_call(
        functools.partial(_gates_kernel, tiles_per_seq=tiles_per_seq),
        grid=(t // TM_PROJ, n // TN_PROJ),
        in_specs=[
            pl.BlockSpec((TM_PROJ, d), lambda i, j: (i, 0)),
            pl.BlockSpec((d, TN_PROJ), lambda i, j: (0, j)),
            pl.BlockSpec((d, LANES), lambda i, j: (0, f_block)),
            pl.BlockSpec((1, LANES), lambda i, j: (0, 0)),
        ],
        out_specs=[
            pl.BlockSpec((TM_PROJ, TN_PROJ), lambda i, j: (i, j)),
            pl.BlockSpec((TM_PROJ, LANES), lambda i, j: (i, 0)),
            pl.BlockSpec((1, N_HEADS, TM_PROJ),
                         lambda i, j: (i // tiles_per_seq, 0, i % tiles_per_seq)),
        ],
        out_shape=[
            jax.ShapeDtypeStruct((t, n), BF16),
            jax.ShapeDtypeStruct((t, LANES), F32),
            jax.ShapeDtypeStruct((t // seq, N_HEADS, seq), F32),
        ],
        scratch_shapes=[pltpu.VMEM((1, LANES), F32)],
        compiler_params=_params(("arbitrary", "arbitrary")),
        name="gates_proj",
    )(u, w, w_all, bf)


def _qk(q, k):
    return lax.dot_general(q, k, (((1,), (1,)), ((), ())), preferred_element_type=F32)


def _sb_kernel(q_ref, k_ref, v_ref, o_ref, *, seq):
    r = lax.broadcasted_iota(jnp.int32, (TK, TK), 0)
    c = lax.broadcasted_iota(jnp.int32, (TK, TK), 1)
    later = (r > c).astype(BF16)
    later2 = jnp.concatenate([later, later], axis=0)

    def valid_mask(d, c):
        k0, r0 = d * TK, c * RC
        if k0 + TK <= r0:
            return None
        return (k0 + lax.broadcasted_iota(jnp.int32, (RC, TK), 1)
                < r0 + lax.broadcasted_iota(jnp.int32, (RC, TK), 0))

    def gates(d, c):
        q = q_ref[0, c * RC:(c + 1) * RC, :]
        k = k_ref[0, d * TK:(d + 1) * TK, :]
        z = _qk(q, k)
        sp = jnp.log2(1.0 + jnp.exp2(-jnp.abs(z)))
        log_beta = jnp.minimum(z, 0.0) - sp
        log_keep = log_beta - z
        valid = valid_mask(d, c)
        if valid is not None:
            log_keep = jnp.where(valid, log_keep, 0.0)
        hi = log_keep.astype(BF16)
        lo = (log_keep - hi.astype(F32)).astype(BF16)
        return log_beta, jnp.concatenate([hi, lo], axis=1), jnp.sum(log_keep, axis=1, keepdims=True)

    def within_block(log_beta, hilo, row_sum):
        return log_beta + jnp.dot(hilo, later2, preferred_element_type=F32), row_sum

    def accumulate(d, c, log_w, row_sum, carry, acc):
        v = v_ref[0, d * TK:(d + 1) * TK, :]
        w = jnp.exp2(log_w + carry)
        valid = valid_mask(d, c)
        if valid is not None:
            w = jnp.where(valid, w, 0.0)
        return carry + row_sum, acc + jnp.dot(w.astype(BF16), v, preferred_element_type=F32)

    n_chunks = seq // RC
    state = [(jnp.zeros((RC, 1), F32), jnp.zeros((RC, HEAD_DIM), F32)) for _ in range(n_chunks)]
    items = [(d, c) for d in reversed(range(seq // TK)) for c in range(d * TK // RC, n_chunks)]
    n = len(items)
    staged_a, staged_b = {}, {}
    for step in range(-SB_LOOKAHEAD_QK, n):
        i_a, i_b = step + SB_LOOKAHEAD_QK, step + SB_LOOKAHEAD_CUM
        if i_a < n:
            staged_a[i_a] = gates(*items[i_a])
        if 0 <= i_b < n:
            staged_b[i_b] = within_block(*staged_a.pop(i_b))
        if step >= 0:
            d, c = items[step]
            state[c] = accumulate(d, c, *staged_b.pop(step), *state[c])
    for c in range(n_chunks):
        o_ref[0, c * RC:(c + 1) * RC, :] = state[c][1].astype(o_ref.dtype)


def _fox_kernel(q_ref, k_ref, v_ref, cq_ref, ck_ref, o_ref, *, seq):
    h = pl.program_id(1)
    lane = lax.broadcasted_iota(jnp.int32, (TK, LANES), 1)

    def scores(d, c):
        k0, c0 = d * TK, c * QC
        q = q_ref[0, c0:c0 + QC, :]
        k = k_ref[0, k0:k0 + TK, :]
        ck = jnp.sum(jnp.where(lane == h, ck_ref[0, k0:k0 + TK, :], 0.0),
                     axis=1, keepdims=True) * LOG2E
        y = _qk(k, q) - ck
        if k0 + TK > c0:
            valid = (k0 + lax.broadcasted_iota(jnp.int32, (TK, QC), 0)
                     <= c0 + lax.broadcasted_iota(jnp.int32, (TK, QC), 1))
            y = jnp.where(valid, y, NEG_BIG)
        return y

    def accumulate(d, c, y, m, l, acc):
        k0, c0 = d * TK, c * QC
        v = v_ref[0, k0:k0 + TK, :]
        cq = cq_ref[0, pl.ds(h, 1), c0:c0 + QC] * LOG2E
        m_new = jnp.maximum(m, cq + jnp.max(y, axis=0, keepdims=True))
        alpha = jnp.exp2(m - m_new)
        p = jnp.exp2(y - (m_new - cq))
        l = alpha * l + jnp.sum(p, axis=0, keepdims=True)
        pv = lax.dot_general(v, p.astype(BF16), (((0,), (0,)), ((), ())),
                             preferred_element_type=F32)
        return m_new, l, alpha * acc + pv

    n_chunks = seq // QC
    state = [(jnp.full((1, QC), NEG_BIG, F32), jnp.zeros((1, QC), F32),
              jnp.zeros((HEAD_DIM, QC), F32)) for _ in range(n_chunks)]
    items = [(d, c) for d in range(seq // TK) for c in range(d * TK // QC, n_chunks)]
    ys = [scores(*item) for item in items[:FOX_LOOKAHEAD]]
    for i, (d, c) in enumerate(items):
        if i + FOX_LOOKAHEAD < len(items):
            ys.append(scores(*items[i + FOX_LOOKAHEAD]))
        state[c] = accumulate(d, c, ys.pop(0), *state[c])
    for c in range(n_chunks):
        _, l, acc = state[c]
        o_ref[0, c * QC:(c + 1) * QC, :] = (acc / l).T.astype(o_ref.dtype)


def _attention(qkv, cum_col, cum_row, batch, seq):
    grid = (batch, N_HEADS)

    def head_spec(head0):
        return pl.BlockSpec((1, seq, HEAD_DIM), lambda b, h: (b, 0, head0 + h))

    out_shape = jax.ShapeDtypeStruct((batch, seq, D_BRANCH), BF16)
    sem = ("arbitrary", "arbitrary")

    o_sb = pl.pallas_call(
        functools.partial(_sb_kernel, seq=seq),
        grid=grid,
        in_specs=[head_spec(0), head_spec(N_HEADS), head_spec(2 * N_HEADS)],
        out_specs=head_spec(0),
        out_shape=out_shape,
        compiler_params=_params(sem),
        name="stickbreak_attn",
    )(qkv, qkv, qkv)

    o_fx = pl.pallas_call(
        functools.partial(_fox_kernel, seq=seq),
        grid=grid,
        in_specs=[
            head_spec(3 * N_HEADS), head_spec(4 * N_HEADS), head_spec(5 * N_HEADS),
            pl.BlockSpec((1, N_HEADS, seq), lambda b, h: (b, 0, 0)),
            pl.BlockSpec((1, seq, LANES), lambda b, h: (b, 0, 0)),
        ],
        out_specs=head_spec(0),
        out_shape=out_shape,
        compiler_params=_params(sem),
        name="forgetting_attn",
    )(qkv, qkv, qkv, cum_row, cum_col)
    return o_sb, o_fx


def _mix_kernel(osb_ref, ofx_ref, gsb_ref, gfx_ref, x_ref, wsb_ref, wfx_ref, wout_ref,
                npost_ref, npre_ref, h_ref, u_ref):
    a = jnp.dot(osb_ref[...], wsb_ref[...], preferred_element_type=F32)
    b = jnp.dot(ofx_ref[...], wfx_ref[...], preferred_element_type=F32)
    merged = gsb_ref[...].astype(F32) * a + gfx_ref[...].astype(F32) * b
    mix = jnp.dot(merged.astype(BF16), wout_ref[...], preferred_element_type=F32)
    h = x_ref[...] + _rms_norm(mix, npost_ref[...])
    h_ref[...] = h
    u_ref[...] = _rms_norm(h, npre_ref[...]).astype(BF16)


def _const_spec(shape):
    return pl.BlockSpec(shape, lambda i: (0,) * len(shape), pipeline_mode=pl.Buffered(1))


def _mix(o_sb, o_fx, gates, x2, w_sb, w_fx, w_out, n_post, n_ffn_pre):
    t, d = x2.shape
    tm = TM_MIX
    return pl.pallas_call(
        _mix_kernel,
        grid=(t // tm,),
        in_specs=[
            pl.BlockSpec((tm, D_BRANCH), lambda i: (i, 0)),
            pl.BlockSpec((tm, D_BRANCH), lambda i: (i, 0)),
            pl.BlockSpec((tm, d), lambda i: (i, 0)),
            pl.BlockSpec((tm, d), lambda i: (i, 1)),
            pl.BlockSpec((tm, d), lambda i: (i, 0)),
            _const_spec((D_BRANCH, d)),
            _const_spec((D_BRANCH, d)),
            _const_spec((d, d)),
            _const_spec((1, d)),
            _const_spec((1, d)),
        ],
        out_specs=[
            pl.BlockSpec((tm, d), lambda i: (i, 0)),
            pl.BlockSpec((tm, d), lambda i: (i, 0)),
        ],
        out_shape=[
            jax.ShapeDtypeStruct((t, d), F32),
            jax.ShapeDtypeStruct((t, d), BF16),
        ],
        compiler_params=_params(("arbitrary",)),
        name="merge_out_proj",
    )(o_sb, o_fx, gates, gates, x2, w_sb, w_fx, w_out, n_post, n_ffn_pre)


def _ffn_kernel(u_ref, h_hbm_ref, wg_ref, wu_ref, wd_ref, npost_ref, o_ref, h_buf, h_sem):
    i, j = pl.program_id(0), pl.program_id(1)
    tm = o_ref.shape[0]
    h_copy = pltpu.make_async_copy(h_hbm_ref.at[pl.ds(i * tm, tm), :], h_buf, h_sem)

    @pl.when(j == 0)
    def _():
        h_copy.start()
        o_ref[...] = jnp.zeros_like(o_ref)

    u = u_ref[...]
    g = jnp.dot(u, wg_ref[...], preferred_element_type=F32)
    up = jnp.dot(u, wu_ref[...], preferred_element_type=F32)
    hidden = ((g * _sigmoid(g)) * up).astype(BF16)
    for c0 in range(0, o_ref.shape[1], DOWN_CHUNK):
        cols = slice(c0, c0 + DOWN_CHUNK)
        o_ref[:, cols] += jnp.dot(hidden, wd_ref[:, cols], preferred_element_type=F32)

    @pl.when(j == pl.num_programs(1) - 1)
    def _():
        h_copy.wait()
        for r0 in range(0, tm, NORM_ROWS):
            rows = slice(r0, r0 + NORM_ROWS)
            o_ref[rows, :] = h_buf[rows, :] + _rms_norm(o_ref[rows, :], npost_ref[...])


def _ffn(u2, h1, w_gate, w_up, w_down, n_post):
    t, d = h1.shape
    f = w_gate.shape[1]
    tm, tf = TM_FFN, TF_FFN
    return pl.pallas_call(
        _ffn_kernel,
        grid=(t // tm, f // tf),
        in_specs=[
            pl.BlockSpec((tm, d), lambda i, j: (i, 0)),
            pl.BlockSpec(memory_space=pl.ANY),
            pl.BlockSpec((d, tf), lambda i, j: (0, j)),
            pl.BlockSpec((d, tf), lambda i, j: (0, j)),
            pl.BlockSpec((tf, d), lambda i, j: (j, 0)),
            pl.BlockSpec((1, d), lambda i, j: (0, 0)),
        ],
        out_specs=pl.BlockSpec((tm, d), lambda i, j: (i, 0)),
        out_shape=jax.ShapeDtypeStruct((t, d), F32),
        scratch_shapes=[pltpu.VMEM((tm, d), F32), pltpu.SemaphoreType.DMA(())],
        compiler_params=_params(("arbitrary", "arbitrary")),
        name="swiglu_ffn",
    )(u2, h1, w_gate, w_up, w_down, n_post)


def kernel(x, norm_mix_pre, norm_mix_post, w_in, b_forget, w_branch_sb, w_branch_fox, w_out,
           norm_ffn_pre, norm_ffn_post, w_ffn_gate, w_ffn_up, w_ffn_down):
    batch, seq, d = x.shape
    depth = w_in.shape[0]
    n_qkv = 6 * D_BRANCH
    assert n_qkv % LANES == 0 and seq % TM_PROJ == 0
    h = x.reshape(batch * seq, d)
    for l in range(depth):
        w_all = w_in[l].astype(BF16)
        w_gates = w_all[:, n_qkv + N_HEADS:]
        b_f = jnp.pad(b_forget[l], (0, LANES - N_HEADS)).reshape(1, LANES)
        g_pre = norm_mix_pre[l].reshape(1, d)

        q_scale = jnp.full((D_BRANCH,), ATTN_SCALE * LOG2E, F32)
        ones = jnp.ones((D_BRANCH,), F32)
        col_scale = jnp.concatenate([q_scale, ones, ones, q_scale, ones, ones]).reshape(1, n_qkv)
        qkv, u1 = _qkv_proj(h, g_pre, w_all, col_scale)
        gates, cum_col, cum_row = _gates_proj(u1, w_gates, w_all, n_qkv // LANES, b_f, seq)
        o_sb, o_fx = _attention(qkv.reshape(batch, seq, n_qkv),
                                cum_col.reshape(batch, seq, LANES), cum_row, batch, seq)
        h1, u2 = _mix(o_sb.reshape(batch * seq, D_BRANCH), o_fx.reshape(batch * seq, D_BRANCH),
                      gates, h,
                      w_branch_sb[l].astype(BF16), w_branch_fox[l].astype(BF16),
                      w_out[l].astype(BF16),
                      norm_mix_post[l].reshape(1, d), norm_ffn_pre[l].reshape(1, d))
        h = _ffn(u2, h1, w_ffn_gate[l].astype(BF16), w_ffn_up[l].astype(BF16),
                 w_ffn_down[l].astype(BF16), norm_ffn_post[l].reshape(1, d))
    return h.reshape(batch, seq, d)
```

```python
import functools

import jax
import jax.numpy as jnp
from jax import lax
from jax.experimental import pallas as pl
from jax.experimental.pallas import tpu as pltpu

F32 = jnp.float32
BF16 = jnp.bfloat16

D_MODEL = 2048
HEAD_DIM = 128
N_HEADS = 8
D_BRANCH = N_HEADS * HEAD_DIM
D_FF = 5632
RMS_EPS = 1e-6
ATTN_SCALE = HEAD_DIM ** -0.5
NEG_BIG = -1e30
LOG2E = 1.4426950408889634

LANES = 128
BF16_SUBLANES = 16
VMEM_LIMIT = 56 * 1024 * 1024

TM_PROJ = 1024
TN_PROJ = 1024
CUM_CHUNK = 256
TK = 256
RC = 256
QC = 256
SB_LOOKAHEAD_QK = 4
SB_LOOKAHEAD_CUM = 2
FOX_LOOKAHEAD = 4
TM_MIX = 256
TM_FFN = 1024
TF_FFN = 512
DOWN_CHUNK = 512
NORM_ROWS = 256


def _params(semantics):
    return pltpu.CompilerParams(dimension_semantics=semantics, vmem_limit_bytes=VMEM_LIMIT)


def _rms_norm(x, g):
    ms = jnp.mean(x * x, axis=-1, keepdims=True)
    return (x * lax.rsqrt(ms + RMS_EPS)) * g


def _log_sigmoid(x):
    return jnp.minimum(x, 0.0) - jnp.log1p(jnp.exp(-jnp.abs(x)))


def _sigmoid(x):
    return 1.0 / (1.0 + jnp.exp(-x))


def _split3(x):
    hi = x.astype(BF16)
    r1 = x - hi.astype(F32)
    mid = r1.astype(BF16)
    lo = (r1 - mid.astype(F32)).astype(BF16)
    return hi, mid, lo


def _rider_plan(weights, steps_each):
    plan, s0 = [], 0
    for w, c in zip(weights, steps_each):
        assert w.shape[0] % c == 0 and (w.shape[0] // c) % BF16_SUBLANES == 0
        plan.append((w, s0, c))
        s0 += c
    return plan


def _rider_specs(plan, n_j):
    specs, shapes = [], []
    for w, s0, c in plan:
        def index(i, j, s0=s0, c=c):
            return (jnp.clip(i * n_j + j - s0, 0, c - 1), 0)
        specs.append(pl.BlockSpec((w.shape[0] // c, w.shape[1]), index))
        shapes.append(jax.ShapeDtypeStruct(w.shape, BF16))
    return specs, shapes


def _run_riders(step, windows, src_refs, dst_refs):
    for (s0, c), src, dst in zip(windows, src_refs, dst_refs):
        @pl.when((step >= s0) & (step < s0 + c))
        def _(src=src, dst=dst):
            dst[...] = src[...].astype(BF16)


def _qkv_kernel(x_ref, g_ref, w_ref, cs_ref, *refs, windows):
    k = len(windows)
    src_refs, (o_ref, u_ref), dst_refs = refs[:k], refs[k:k + 2], refs[k + 2:]
    _run_riders(pl.program_id(0) * pl.num_programs(1) + pl.program_id(1), windows,
                src_refs, dst_refs)

    @pl.when(pl.program_id(1) == 0)
    def _():
        u_ref[...] = _rms_norm(x_ref[...], g_ref[...]).astype(BF16)

    acc = jnp.dot(u_ref[...], w_ref[...], preferred_element_type=F32)
    o_ref[...] = (acc * cs_ref[...]).astype(o_ref.dtype)


def _qkv_proj(x2, g, w, col_scale, ride):
    t, d = x2.shape
    n = col_scale.shape[1]
    n_i, n_j = t // TM_PROJ, n // TN_PROJ
    plan = _rider_plan(ride, [n_i * n_j // len(ride)] * len(ride))
    ride_specs, ride_shapes = _rider_specs(plan, n_j)
    return pl.pallas_call(
        functools.partial(_qkv_kernel, windows=[(s0, c) for _, s0, c in plan]),
        grid=(n_i, n_j),
        in_specs=[
            pl.BlockSpec((TM_PROJ, d), lambda i, j: (i, 0)),
            pl.BlockSpec((1, d), lambda i, j: (0, 0)),
            pl.BlockSpec((d, TN_PROJ), lambda i, j: (0, j)),
            pl.BlockSpec((1, TN_PROJ), lambda i, j: (0, j)),
        ] + ride_specs,
        out_specs=[
            pl.BlockSpec((TM_PROJ, TN_PROJ), lambda i, j: (i, j)),
            pl.BlockSpec((TM_PROJ, d), lambda i, j: (i, 0)),
        ] + ride_specs,
        out_shape=[
            jax.ShapeDtypeStruct((t, n), BF16),
            jax.ShapeDtypeStruct((t, d), BF16),
        ] + ride_shapes,
        compiler_params=_params(("arbitrary", "arbitrary")),
        name="qkv_proj",
    )(x2, g, w, col_scale, *ride)


def _gates_kernel(u_ref, w_ref, wf_ref, bf_ref, *refs, tiles_per_seq, windows):
    k = len(windows)
    src_refs, (o_ref, cumc_ref, cumr_ref), dst_refs = refs[:k], refs[k:k + 3], refs[k + 3:-1]
    carry_ref = refs[-1]
    i = pl.program_id(0)
    _run_riders(i * pl.num_programs(1) + pl.program_id(1), windows, src_refs, dst_refs)

    @pl.when(pl.program_id(1) == 0)
    def _():
        f = jnp.dot(u_ref[...], wf_ref[...], preferred_element_type=F32)
        log_f = _log_sigmoid(f + bf_ref[...])

        @pl.when(i % tiles_per_seq == 0)
        def _():
            carry_ref[...] = jnp.zeros_like(carry_ref)

        r = lax.broadcasted_iota(jnp.int32, (CUM_CHUNK, CUM_CHUNK), 0)
        c = lax.broadcasted_iota(jnp.int32, (CUM_CHUNK, CUM_CHUNK), 1)
        lower = (c <= r).astype(BF16)
        carry = carry_ref[...]
        for ch in range(TM_PROJ // CUM_CHUNK):
            blk = log_f[ch * CUM_CHUNK:(ch + 1) * CUM_CHUNK]
            hi, mid, lo = _split3(blk)
            cs = (jnp.dot(lower, hi, preferred_element_type=F32)
                  + jnp.dot(lower, mid, preferred_element_type=F32)
                  + jnp.dot(lower, lo, preferred_element_type=F32)) + carry
            cumc_ref[ch * CUM_CHUNK:(ch + 1) * CUM_CHUNK, :] = cs
            carry = cs[CUM_CHUNK - 1:CUM_CHUNK, :]
        carry_ref[...] = carry
        cumr_ref[0] = cumc_ref[...].T[:N_HEADS, :]

    acc = jnp.dot(u_ref[...], w_ref[...], preferred_element_type=F32)
    o_ref[...] = _sigmoid(acc).astype(o_ref.dtype)


def _gates_proj(u, w, w_all, f_block, bf, seq, ride):
    t, d = u.shape
    n = w.shape[1]
    tiles_per_seq = seq // TM_PROJ
    n_i, n_j = t // TM_PROJ, n // TN_PROJ
    ride_rows = sum(r.shape[0] for r in ride)
    plan = _rider_plan(ride, [n_i * n_j * r.shape[0] // ride_rows for r in ride])
    ride_specs, ride_shapes = _rider_specs(plan, n_j)
    return pl.pallas_call(
        functools.partial(_gates_kernel, tiles_per_seq=tiles_per_seq,
                          windows=[(s0, c) for _, s0, c in plan]),
        grid=(n_i, n_j),
        in_specs=[
            pl.BlockSpec((TM_PROJ, d), lambda i, j: (i, 0)),
            pl.BlockSpec((d, TN_PROJ), lambda i, j: (0, j)),
            pl.BlockSpec((d, LANES), lambda i, j: (0, f_block)),
            pl.BlockSpec((1, LANES), lambda i, j: (0, 0)),
        ] + ride_specs,
        out_specs=[
            pl.BlockSpec((TM_PROJ, TN_PROJ), lambda i, j: (i, j)),
            pl.BlockSpec((TM_PROJ, LANES), lambda i, j: (i, 0)),
            pl.BlockSpec((1, N_HEADS, TM_PROJ),
                         lambda i, j: (i // tiles_per_seq, 0, i % tiles_per_seq)),
        ] + ride_specs,
        out_shape=[
            jax.ShapeDtypeStruct((t, n), BF16),
            jax.ShapeDtypeStruct((t, LANES), F32),
            jax.ShapeDtypeStruct((t // seq, N_HEADS, seq), F32),
        ] + ride_shapes,
        scratch_shapes=[pltpu.VMEM((1, LANES), F32)],
        compiler_params=_params(("arbitrary", "arbitrary")),
        name="gates_proj",
    )(u, w, w_all, bf, *ride)


def _qk(q, k):
    return lax.dot_general(q, k, (((1,), (1,)), ((), ())), preferred_element_type=F32)


def _sb_kernel(q_ref, k_ref, v_ref, o_ref, *, seq):
    r = lax.broadcasted_iota(jnp.int32, (TK, TK), 0)
    c = lax.broadcasted_iota(jnp.int32, (TK, TK), 1)
    later = (r > c).astype(BF16)
    later2 = jnp.concatenate([later, later], axis=0)

    def valid_mask(d, c):
        k0, r0 = d * TK, c * RC
        if k0 + TK <= r0:
            return None
        return (k0 + lax.broadcasted_iota(jnp.int32, (RC, TK), 1)
                < r0 + lax.broadcasted_iota(jnp.int32, (RC, TK), 0))

    def gates(d, c):
        q = q_ref[0, c * RC:(c + 1) * RC, :]
        k = k_ref[0, d * TK:(d + 1) * TK, :]
        z = _qk(q, k)
        sp = jnp.log2(1.0 + jnp.exp2(-jnp.abs(z)))
        log_beta = jnp.minimum(z, 0.0) - sp
        log_keep = log_beta - z
        valid = valid_mask(d, c)
        if valid is not None:
            log_keep = jnp.where(valid, log_keep, 0.0)
        hi = log_keep.astype(BF16)
        lo = (log_keep - hi.astype(F32)).astype(BF16)
        return log_beta, jnp.concatenate([hi, lo], axis=1), jnp.sum(log_keep, axis=1, keepdims=True)

    def within_block(log_beta, hilo, row_sum):
        return log_beta + jnp.dot(hilo, later2, preferred_element_type=F32), row_sum

    def accumulate(d, c, log_w, row_sum, carry, acc):
        v = v_ref[0, d * TK:(d + 1) * TK, :]
        w = jnp.exp2(log_w + carry)
        valid = valid_mask(d, c)
        if valid is not None:
            w = jnp.where(valid, w, 0.0)
        return carry + row_sum, acc + jnp.dot(w.astype(BF16), v, preferred_element_type=F32)

    n_chunks = seq // RC
    state = [(jnp.zeros((RC, 1), F32), jnp.zeros((RC, HEAD_DIM), F32)) for _ in range(n_chunks)]
    items = [(d, c) for d in reversed(range(seq // TK)) for c in range(d * TK // RC, n_chunks)]
    n = len(items)
    staged_a, staged_b = {}, {}
    for step in range(-SB_LOOKAHEAD_QK, n):
        i_a, i_b = step + SB_LOOKAHEAD_QK, step + SB_LOOKAHEAD_CUM
        if i_a < n:
            staged_a[i_a] = gates(*items[i_a])
        if 0 <= i_b < n:
            staged_b[i_b] = within_block(*staged_a.pop(i_b))
        if step >= 0:
            d, c = items[step]
            state[c] = accumulate(d, c, *staged_b.pop(step), *state[c])
    for c in range(n_chunks):
        o_ref[0, c * RC:(c + 1) * RC, :] = state[c][1].astype(o_ref.dtype)


def _fox_kernel(q_ref, k_ref, v_ref, cq_ref, ck_ref, o_ref, *, seq):
    h = pl.program_id(1)
    lane = lax.broadcasted_iota(jnp.int32, (TK, LANES), 1)
    ones_col = (lane == 0).astype(BF16)
    n_blocks = seq // TK
    ck_blocks = [jnp.sum(jnp.where(lane == h, ck_ref[0, d * TK:(d + 1) * TK, :], 0.0),
                         axis=1, keepdims=True) * LOG2E for d in range(n_blocks)]
    v_blocks = [jnp.concatenate([v_ref[0, d * TK:(d + 1) * TK, :], ones_col], axis=1)
                for d in range(n_blocks)]

    def scores(d, c):
        k0, c0 = d * TK, c * QC
        q = q_ref[0, c0:c0 + QC, :]
        k = k_ref[0, k0:k0 + TK, :]
        y = _qk(k, q) - ck_blocks[d]
        if k0 + TK > c0:
            valid = (k0 + lax.broadcasted_iota(jnp.int32, (TK, QC), 0)
                     <= c0 + lax.broadcasted_iota(jnp.int32, (TK, QC), 1))
            y = jnp.where(valid, y, NEG_BIG)
        return y

    def accumulate(d, c, y, m, l, acc):
        c0 = c * QC
        cq = cq_ref[0, pl.ds(h, 1), c0:c0 + QC] * LOG2E
        m_new = jnp.maximum(m, cq + jnp.max(y, axis=0, keepdims=True))
        alpha = jnp.exp2(m - m_new)
        p = jnp.exp2(y - (m_new - cq))
        pv = lax.dot_general(v_blocks[d], p.astype(BF16), (((0,), (0,)), ((), ())),
                             preferred_element_type=F32)
        l = alpha * l + pv[HEAD_DIM:HEAD_DIM + 1]
        return m_new, l, alpha * acc + pv[:HEAD_DIM]

    n_chunks = seq // QC
    state = [(jnp.full((1, QC), NEG_BIG, F32), jnp.zeros((1, QC), F32),
              jnp.zeros((HEAD_DIM, QC), F32)) for _ in range(n_chunks)]
    items = [(d, c) for d in range(seq // TK) for c in range(d * TK // QC, n_chunks)]
    ys = [scores(*item) for item in items[:FOX_LOOKAHEAD]]
    for i, (d, c) in enumerate(items):
        if i + FOX_LOOKAHEAD < len(items):
            ys.append(scores(*items[i + FOX_LOOKAHEAD]))
        state[c] = accumulate(d, c, ys.pop(0), *state[c])
    for c in range(n_chunks):
        _, l, acc = state[c]
        o_ref[0, c * QC:(c + 1) * QC, :] = (acc / l).T.astype(o_ref.dtype)


def _attention(qkv, cum_col, cum_row, batch, seq):
    grid = (batch, N_HEADS)

    def head_spec(head0):
        return pl.BlockSpec((1, seq, HEAD_DIM), lambda b, h: (b, 0, head0 + h))

    out_shape = jax.ShapeDtypeStruct((batch, seq, D_BRANCH), BF16)
    sem = ("arbitrary", "arbitrary")

    o_sb = pl.pallas_call(
        functools.partial(_sb_kernel, seq=seq),
        grid=grid,
        in_specs=[head_spec(0), head_spec(N_HEADS), head_spec(2 * N_HEADS)],
        out_specs=head_spec(0),
        out_shape=out_shape,
        compiler_params=_params(sem),
        name="stickbreak_attn",
    )(qkv, qkv, qkv)

    o_fx = pl.pallas_call(
        functools.partial(_fox_kernel, seq=seq),
        grid=grid,
        in_specs=[
            head_spec(3 * N_HEADS), head_spec(4 * N_HEADS), head_spec(5 * N_HEADS),
            pl.BlockSpec((1, N_HEADS, seq), lambda b, h: (b, 0, 0)),
            pl.BlockSpec((1, seq, LANES), lambda b, h: (b, 0, 0)),
        ],
        out_specs=head_spec(0),
        out_shape=out_shape,
        compiler_params=_params(sem),
        name="forgetting_attn",
    )(qkv, qkv, qkv, cum_row, cum_col)
    return o_sb, o_fx


def _mix_kernel(osb_ref, ofx_ref, gsb_ref, gfx_ref, x_ref, wsb_ref, wfx_ref, wout_ref,
                npost_ref, npre_ref, h_ref, u_ref):
    a = jnp.dot(osb_ref[...], wsb_ref[...], preferred_element_type=F32)
    b = jnp.dot(ofx_ref[...], wfx_ref[...], preferred_element_type=F32)
    merged = gsb_ref[...].astype(F32) * a + gfx_ref[...].astype(F32) * b
    mix = jnp.dot(merged.astype(BF16), wout_ref[...], preferred_element_type=F32)
    h = x_ref[...] + _rms_norm(mix, npost_ref[...])
    h_ref[...] = h
    u_ref[...] = _rms_norm(h, npre_ref[...]).astype(BF16)


def _const_spec(shape):
    return pl.BlockSpec(shape, lambda i: (0,) * len(shape), pipeline_mode=pl.Buffered(1))


def _mix(o_sb, o_fx, gates, x2, w_sb, w_fx, w_out, n_post, n_ffn_pre):
    t, d = x2.shape
    tm = TM_MIX
    return pl.pallas_call(
        _mix_kernel,
        grid=(t // tm,),
        in_specs=[
            pl.BlockSpec((tm, D_BRANCH), lambda i: (i, 0)),
            pl.BlockSpec((tm, D_BRANCH), lambda i: (i, 0)),
            pl.BlockSpec((tm, d), lambda i: (i, 0)),
            pl.BlockSpec((tm, d), lambda i: (i, 1)),
            pl.BlockSpec((tm, d), lambda i: (i, 0)),
            _const_spec((D_BRANCH, d)),
            _const_spec((D_BRANCH, d)),
            _const_spec((d, d)),
            _const_spec((1, d)),
            _const_spec((1, d)),
        ],
        out_specs=[
            pl.BlockSpec((tm, d), lambda i: (i, 0)),
            pl.BlockSpec((tm, d), lambda i: (i, 0)),
        ],
        out_shape=[
            jax.ShapeDtypeStruct((t, d), F32),
            jax.ShapeDtypeStruct((t, d), BF16),
        ],
        compiler_params=_params(("arbitrary",)),
        name="merge_out_proj",
    )(o_sb, o_fx, gates, gates, x2, w_sb, w_fx, w_out, n_post, n_ffn_pre)


def _ffn_kernel(u_ref, h_hbm_ref, wg_ref, wu_ref, wd_ref, npost_ref, o_ref, h_buf, h_sem):
    i, j = pl.program_id(0), pl.program_id(1)
    tm = o_ref.shape[0]
    h_copy = pltpu.make_async_copy(h_hbm_ref.at[pl.ds(i * tm, tm), :], h_buf, h_sem)

    @pl.when(j == 0)
    def _():
        h_copy.start()
        o_ref[...] = jnp.zeros_like(o_ref)

    u = u_ref[...]
    g = jnp.dot(u, wg_ref[...], preferred_element_type=F32)
    up = jnp.dot(u, wu_ref[...], preferred_element_type=F32)
    hidden = ((g * _sigmoid(g)) * up).astype(BF16)
    for c0 in range(0, o_ref.shape[1], DOWN_CHUNK):
        cols = slice(c0, c0 + DOWN_CHUNK)
        o_ref[:, cols] += jnp.dot(hidden, wd_ref[:, cols], preferred_element_type=F32)

    @pl.when(j == pl.num_programs(1) - 1)
    def _():
        h_copy.wait()
        for r0 in range(0, tm, NORM_ROWS):
            rows = slice(r0, r0 + NORM_ROWS)
            o_ref[rows, :] = h_buf[rows, :] + _rms_norm(o_ref[rows, :], npost_ref[...])


def _ffn(u2, h1, w_gate, w_up, w_down, n_post):
    t, d = h1.shape
    f = w_gate.shape[1]
    tm, tf = TM_FFN, TF_FFN
    return pl.pallas_call(
        _ffn_kernel,
        grid=(t // tm, f // tf),
        in_specs=[
            pl.BlockSpec((tm, d), lambda i, j: (i, 0)),
            pl.BlockSpec(memory_space=pl.ANY),
            pl.BlockSpec((d, tf), lambda i, j: (0, j)),
            pl.BlockSpec((d, tf), lambda i, j: (0, j)),
            pl.BlockSpec((tf, d), lambda i, j: (j, 0)),
            pl.BlockSpec((1, d), lambda i, j: (0, 0)),
        ],
        out_specs=pl.BlockSpec((tm, d), lambda i, j: (i, 0)),
        out_shape=jax.ShapeDtypeStruct((t, d), F32),
        scratch_shapes=[pltpu.VMEM((tm, d), F32), pltpu.SemaphoreType.DMA(())],
        compiler_params=_params(("arbitrary", "arbitrary")),
        name="swiglu_ffn",
    )(u2, h1, w_gate, w_up, w_down, n_post)


def kernel(x, norm_mix_pre, norm_mix_post, w_in, b_forget, w_branch_sb, w_branch_fox, w_out,
           norm_ffn_pre, norm_ffn_post, w_ffn_gate, w_ffn_up, w_ffn_down):
    batch, seq, d = x.shape
    depth = w_in.shape[0]
    n_qkv = 6 * D_BRANCH
    assert n_qkv % LANES == 0 and seq % TM_PROJ == 0
    h = x.reshape(batch * seq, d)
    for l in range(depth):
        w_all = w_in[l].astype(BF16)
        w_gates = w_all[:, n_qkv + N_HEADS:]
        b_f = jnp.pad(b_forget[l], (0, LANES - N_HEADS)).reshape(1, LANES)
        g_pre = norm_mix_pre[l].reshape(1, d)

        q_scale = jnp.full((D_BRANCH,), ATTN_SCALE * LOG2E, F32)
        ones = jnp.ones((D_BRANCH,), F32)
        col_scale = jnp.concatenate([q_scale, ones, ones, q_scale, ones, ones]).reshape(1, n_qkv)
        qkv, u1, wg_bf, wu_bf, wd_bf = _qkv_proj(
            h, g_pre, w_all, col_scale, [w_ffn_gate[l], w_ffn_up[l], w_ffn_down[l]])
        gates, cum_col, cum_row, wo_bf, wsb_bf, wfx_bf = _gates_proj(
            u1, w_gates, w_all, n_qkv // LANES, b_f, seq,
            [w_out[l], w_branch_sb[l], w_branch_fox[l]])
        o_sb, o_fx = _attention(qkv.reshape(batch, seq, n_qkv),
                                cum_col.reshape(batch, seq, LANES), cum_row, batch, seq)
        h1, u2 = _mix(o_sb.reshape(batch * seq, D_BRANCH), o_fx.reshape(batch * seq, D_BRANCH),
                      gates, h, wsb_bf, wfx_bf, wo_bf,
                      norm_mix_post[l].reshape(1, d), norm_ffn_pre[l].reshape(1, d))
        h = _ffn(u2, h1, wg_bf, wu_bf, wd_bf, norm_ffn_post[l].reshape(1, d))
    return h.reshape(batch, seq, d)
```

```python
import functools

import jax
import jax.numpy as jnp
from jax import lax
from jax.experimental import pallas as pl
from jax.experimental.pallas import tpu as pltpu

F32 = jnp.float32
BF16 = jnp.bfloat16

D_MODEL = 2048
HEAD_DIM = 128
N_HEADS = 8
D_BRANCH = N_HEADS * HEAD_DIM
D_FF = 5632
RMS_EPS = 1e-6
ATTN_SCALE = HEAD_DIM ** -0.5
NEG_BIG = -1e30
LOG2E = 1.4426950408889634

LANES = 128
BF16_SUBLANES = 16
VMEM_LIMIT = 56 * 1024 * 1024

TM_PROJ = 1024
TN_PROJ = 1024
CUM_CHUNK = 256
TK = 256
RC = 256
QC = 256
SB_LOOKAHEAD_QK = 3
SB_LOOKAHEAD_CUM = 2
FOX_LOOKAHEAD = 4
TM_MERGE = 1024
TN_MERGE = 1024
TM_OUT = 512
OUT_ROWS = 256
TM_FFN = 1024
TF_FFN = 512
DOWN_CHUNK = 512
NORM_ROWS = 256


def _params(semantics):
    return pltpu.CompilerParams(dimension_semantics=semantics, vmem_limit_bytes=VMEM_LIMIT)


def _rms_norm(x, g):
    ms = jnp.mean(x * x, axis=-1, keepdims=True)
    return (x * lax.rsqrt(ms + RMS_EPS)) * g


def _log_sigmoid(x):
    return jnp.minimum(x, 0.0) - jnp.log1p(jnp.exp(-jnp.abs(x)))


def _sigmoid(x):
    return 1.0 / (1.0 + jnp.exp(-x))


def _split3(x):
    hi = x.astype(BF16)
    r1 = x - hi.astype(F32)
    mid = r1.astype(BF16)
    lo = (r1 - mid.astype(F32)).astype(BF16)
    return hi, mid, lo


def _rider_plan(weights, steps_each):
    plan, s0 = [], 0
    for w, c in zip(weights, steps_each):
        assert w.shape[0] % c == 0 and (w.shape[0] // c) % BF16_SUBLANES == 0
        plan.append((w, s0, c))
        s0 += c
    return plan


def _rider_specs(plan, n_j):
    specs, shapes = [], []
    for w, s0, c in plan:
        def index(i, j, s0=s0, c=c):
            return (jnp.clip(i * n_j + j - s0, 0, c - 1), 0)
        specs.append(pl.BlockSpec((w.shape[0] // c, w.shape[1]), index))
        shapes.append(jax.ShapeDtypeStruct(w.shape, BF16))
    return specs, shapes


def _run_riders(step, windows, src_refs, dst_refs):
    for (s0, c), src, dst in zip(windows, src_refs, dst_refs):
        @pl.when((step >= s0) & (step < s0 + c))
        def _(src=src, dst=dst):
            dst[...] = src[...].astype(BF16)


def _qkv_kernel(x_ref, g_ref, w_ref, cs_ref, *refs, windows):
    k = len(windows)
    src_refs, (o_ref, u_ref), dst_refs = refs[:k], refs[k:k + 2], refs[k + 2:]
    _run_riders(pl.program_id(0) * pl.num_programs(1) + pl.program_id(1), windows,
                src_refs, dst_refs)

    @pl.when(pl.program_id(1) == 0)
    def _():
        u_ref[...] = _rms_norm(x_ref[...], g_ref[...]).astype(BF16)

    acc = jnp.dot(u_ref[...], w_ref[...], preferred_element_type=F32)
    o_ref[...] = (acc * cs_ref[...]).astype(o_ref.dtype)


def _qkv_proj(x2, g, w, col_scale, ride):
    t, d = x2.shape
    n = col_scale.shape[1]
    n_i, n_j = t // TM_PROJ, n // TN_PROJ
    plan = _rider_plan(ride, [n_i * n_j // len(ride)] * len(ride))
    ride_specs, ride_shapes = _rider_specs(plan, n_j)
    return pl.pallas_call(
        functools.partial(_qkv_kernel, windows=[(s0, c) for _, s0, c in plan]),
        grid=(n_i, n_j),
        in_specs=[
            pl.BlockSpec((TM_PROJ, d), lambda i, j: (i, 0)),
            pl.BlockSpec((1, d), lambda i, j: (0, 0)),
            pl.BlockSpec((d, TN_PROJ), lambda i, j: (0, j)),
            pl.BlockSpec((1, TN_PROJ), lambda i, j: (0, j)),
        ] + ride_specs,
        out_specs=[
            pl.BlockSpec((TM_PROJ, TN_PROJ), lambda i, j: (i, j)),
            pl.BlockSpec((TM_PROJ, d), lambda i, j: (i, 0)),
        ] + ride_specs,
        out_shape=[
            jax.ShapeDtypeStruct((t, n), BF16),
            jax.ShapeDtypeStruct((t, d), BF16),
        ] + ride_shapes,
        compiler_params=_params(("arbitrary", "arbitrary")),
        name="qkv_proj",
    )(x2, g, w, col_scale, *ride)


def _gates_kernel(u_ref, w_ref, wf_ref, bf_ref, *refs, tiles_per_seq, windows):
    k = len(windows)
    src_refs, (o_ref, cumc_ref, cumr_ref), dst_refs = refs[:k], refs[k:k + 3], refs[k + 3:-1]
    carry_ref = refs[-1]
    i = pl.program_id(0)
    _run_riders(i * pl.num_programs(1) + pl.program_id(1), windows, src_refs, dst_refs)

    @pl.when(pl.program_id(1) == 0)
    def _():
        f = jnp.dot(u_ref[...], wf_ref[...], preferred_element_type=F32)
        log_f = _log_sigmoid(f + bf_ref[...])

        @pl.when(i % tiles_per_seq == 0)
        def _():
            carry_ref[...] = jnp.zeros_like(carry_ref)

        r = lax.broadcasted_iota(jnp.int32, (CUM_CHUNK, CUM_CHUNK), 0)
        c = lax.broadcasted_iota(jnp.int32, (CUM_CHUNK, CUM_CHUNK), 1)
        lower = (c <= r).astype(BF16)
        carry = carry_ref[...]
        for ch in range(TM_PROJ // CUM_CHUNK):
            blk = log_f[ch * CUM_CHUNK:(ch + 1) * CUM_CHUNK]
            hi, mid, lo = _split3(blk)
            cs = (jnp.dot(lower, hi, preferred_element_type=F32)
                  + jnp.dot(lower, mid, preferred_element_type=F32)
                  + jnp.dot(lower, lo, preferred_element_type=F32)) + carry
            cumc_ref[ch * CUM_CHUNK:(ch + 1) * CUM_CHUNK, :] = cs
            carry = cs[CUM_CHUNK - 1:CUM_CHUNK, :]
        carry_ref[...] = carry
        cumr_ref[0] = cumc_ref[...].T[:N_HEADS, :]

    acc = jnp.dot(u_ref[...], w_ref[...], preferred_element_type=F32)
    o_ref[...] = _sigmoid(acc).astype(o_ref.dtype)


def _gates_proj(u, w, w_all, f_block, bf, seq, ride):
    t, d = u.shape
    n = w.shape[1]
    tiles_per_seq = seq // TM_PROJ
    n_i, n_j = t // TM_PROJ, n // TN_PROJ
    ride_rows = sum(r.shape[0] for r in ride)
    plan = _rider_plan(ride, [n_i * n_j * r.shape[0] // ride_rows for r in ride])
    ride_specs, ride_shapes = _rider_specs(plan, n_j)
    return pl.pallas_call(
        functools.partial(_gates_kernel, tiles_per_seq=tiles_per_seq,
                          windows=[(s0, c) for _, s0, c in plan]),
        grid=(n_i, n_j),
        in_specs=[
            pl.BlockSpec((TM_PROJ, d), lambda i, j: (i, 0)),
            pl.BlockSpec((d, TN_PROJ), lambda i, j: (0, j)),
            pl.BlockSpec((d, LANES), lambda i, j: (0, f_block)),
            pl.BlockSpec((1, LANES), lambda i, j: (0, 0)),
        ] + ride_specs,
        out_specs=[
            pl.BlockSpec((TM_PROJ, TN_PROJ), lambda i, j: (i, j)),
            pl.BlockSpec((TM_PROJ, LANES), lambda i, j: (i, 0)),
            pl.BlockSpec((1, N_HEADS, TM_PROJ),
                         lambda i, j: (i // tiles_per_seq, 0, i % tiles_per_seq)),
        ] + ride_specs,
        out_shape=[
            jax.ShapeDtypeStruct((t, n), BF16),
            jax.ShapeDtypeStruct((t, LANES), F32),
            jax.ShapeDtypeStruct((t // seq, N_HEADS, seq), F32),
        ] + ride_shapes,
        scratch_shapes=[pltpu.VMEM((1, LANES), F32)],
        compiler_params=_params(("arbitrary", "arbitrary")),
        name="gates_proj",
    )(u, w, w_all, bf, *ride)


def _qk(q, k):
    return lax.dot_general(q, k, (((1,), (1,)), ((), ())), preferred_element_type=F32)


def _sb_kernel(q_ref, k_ref, v_ref, o_ref, *, seq):
    r = lax.broadcasted_iota(jnp.int32, (TK, TK), 0)
    c = lax.broadcasted_iota(jnp.int32, (TK, TK), 1)
    later = (r > c).astype(BF16)

    def valid_mask(d, c):
        k0, r0 = d * TK, c * RC
        if k0 + TK <= r0:
            return None
        return (k0 + lax.broadcasted_iota(jnp.int32, (RC, TK), 1)
                < r0 + lax.broadcasted_iota(jnp.int32, (RC, TK), 0))

    def gates(d, c):
        q = q_ref[0, c * RC:(c + 1) * RC, :]
        k = k_ref[0, d * TK:(d + 1) * TK, :]
        z = _qk(q, k)
        sp = jnp.log2(1.0 + jnp.exp2(-jnp.abs(z)))
        log_beta = jnp.minimum(z, 0.0) - sp
        log_keep = log_beta - z
        valid = valid_mask(d, c)
        if valid is not None:
            log_keep = jnp.where(valid, log_keep, 0.0)
        return log_beta, log_keep.astype(BF16), jnp.sum(log_keep, axis=1, keepdims=True)

    def within_block(log_beta, keep_bf, row_sum):
        return log_beta + jnp.dot(keep_bf, later, preferred_element_type=F32), row_sum

    def accumulate(d, c, log_w, row_sum, carry, acc):
        v = v_ref[0, d * TK:(d + 1) * TK, :]
        w = jnp.exp2(log_w + carry)
        valid = valid_mask(d, c)
        if valid is not None:
            w = jnp.where(valid, w, 0.0)
        return carry + row_sum, acc + jnp.dot(w.astype(BF16), v, preferred_element_type=F32)

    n_chunks = seq // RC
    state = [(jnp.zeros((RC, 1), F32), jnp.zeros((RC, HEAD_DIM), F32)) for _ in range(n_chunks)]
    items = [(d, c) for d in reversed(range(seq // TK)) for c in range(d * TK // RC, n_chunks)]
    n = len(items)
    staged_a, staged_b = {}, {}
    for step in range(-SB_LOOKAHEAD_QK, n):
        i_a, i_b = step + SB_LOOKAHEAD_QK, step + SB_LOOKAHEAD_CUM
        if i_a < n:
            staged_a[i_a] = gates(*items[i_a])
        if 0 <= i_b < n:
            staged_b[i_b] = within_block(*staged_a.pop(i_b))
        if step >= 0:
            d, c = items[step]
            state[c] = accumulate(d, c, *staged_b.pop(step), *state[c])
    for c in range(n_chunks):
        o_ref[0, c * RC:(c + 1) * RC, :] = state[c][1].astype(o_ref.dtype)


def _fox_kernel(q_ref, k_ref, v_ref, cq_ref, ck_ref, o_ref, *, seq):
    h = pl.program_id(1)
    lane = lax.broadcasted_iota(jnp.int32, (TK, LANES), 1)
    ones_col = (lane == 0).astype(BF16)
    n_blocks = seq // TK
    ck_blocks = [jnp.sum(jnp.where(lane == h, ck_ref[0, d * TK:(d + 1) * TK, :], 0.0),
                         axis=1, keepdims=True) * LOG2E for d in range(n_blocks)]
    v_blocks = [jnp.concatenate([v_ref[0, d * TK:(d + 1) * TK, :], ones_col], axis=1)
                for d in range(n_blocks)]

    def scores(d, c):
        k0, c0 = d * TK, c * QC
        q = q_ref[0, c0:c0 + QC, :]
        k = k_ref[0, k0:k0 + TK, :]
        y = _qk(k, q) - ck_blocks[d]
        if k0 + TK > c0:
            valid = (k0 + lax.broadcasted_iota(jnp.int32, (TK, QC), 0)
                     <= c0 + lax.broadcasted_iota(jnp.int32, (TK, QC), 1))
            y = jnp.where(valid, y, NEG_BIG)
        return y

    def accumulate(d, c, y, m, l, acc):
        c0 = c * QC
        cq = cq_ref[0, pl.ds(h, 1), c0:c0 + QC] * LOG2E
        m_new = jnp.maximum(m, cq + jnp.max(y, axis=0, keepdims=True))
        alpha = jnp.exp2(m - m_new)
        p = jnp.exp2(y - (m_new - cq))
        pv = lax.dot_general(v_blocks[d], p.astype(BF16), (((0,), (0,)), ((), ())),
                             preferred_element_type=F32)
        l = alpha * l + pv[HEAD_DIM:HEAD_DIM + 1]
        return m_new, l, alpha * acc + pv[:HEAD_DIM]

    n_chunks = seq // QC
    state = [(jnp.full((1, QC), NEG_BIG, F32), jnp.zeros((1, QC), F32),
              jnp.zeros((HEAD_DIM, QC), F32)) for _ in range(n_chunks)]
    items = [(d, c) for d in range(seq // TK) for c in range(d * TK // QC, n_chunks)]
    ys = [scores(*item) for item in items[:FOX_LOOKAHEAD]]
    for i, (d, c) in enumerate(items):
        if i + FOX_LOOKAHEAD < len(items):
            ys.append(scores(*items[i + FOX_LOOKAHEAD]))
        state[c] = accumulate(d, c, ys.pop(0), *state[c])
    for c in range(n_chunks):
        _, l, acc = state[c]
        o_ref[0, c * QC:(c + 1) * QC, :] = (acc / l).T.astype(o_ref.dtype)


def _attention(qkv, cum_col, cum_row, batch, seq):
    grid = (batch, N_HEADS)

    def head_spec(head0):
        return pl.BlockSpec((1, seq, HEAD_DIM), lambda b, h: (b, 0, head0 + h))

    out_shape = jax.ShapeDtypeStruct((batch, seq, D_BRANCH), BF16)
    sem = ("arbitrary", "arbitrary")

    o_sb = pl.pallas_call(
        functools.partial(_sb_kernel, seq=seq),
        grid=grid,
        in_specs=[head_spec(0), head_spec(N_HEADS), head_spec(2 * N_HEADS)],
        out_specs=head_spec(0),
        out_shape=out_shape,
        compiler_params=_params(sem),
        name="stickbreak_attn",
    )(qkv, qkv, qkv)

    o_fx = pl.pallas_call(
        functools.partial(_fox_kernel, seq=seq),
        grid=grid,
        in_specs=[
            head_spec(3 * N_HEADS), head_spec(4 * N_HEADS), head_spec(5 * N_HEADS),
            pl.BlockSpec((1, N_HEADS, seq), lambda b, h: (b, 0, 0)),
            pl.BlockSpec((1, seq, LANES), lambda b, h: (b, 0, 0)),
        ],
        out_specs=head_spec(0),
        out_shape=out_shape,
        compiler_params=_params(sem),
        name="forgetting_attn",
    )(qkv, qkv, qkv, cum_row, cum_col)
    return o_sb, o_fx


def _merge_kernel(osb_ref, ofx_ref, gsb_ref, gfx_ref, wsb_ref, wfx_ref, o_ref):
    a = jnp.dot(osb_ref[...], wsb_ref[...], preferred_element_type=F32)
    b = jnp.dot(ofx_ref[...], wfx_ref[...], preferred_element_type=F32)
    o_ref[...] = (gsb_ref[...].astype(F32) * a + gfx_ref[...].astype(F32) * b).astype(o_ref.dtype)


def _merge(o_sb, o_fx, gates, w_sb, w_fx):
    t, k = o_sb.shape
    d = w_sb.shape[1]
    tm, tn = TM_MERGE, TN_MERGE
    n_j = d // tn
    return pl.pallas_call(
        _merge_kernel,
        grid=(t // tm, n_j),
        in_specs=[
            pl.BlockSpec((tm, k), lambda i, j: (i, 0)),
            pl.BlockSpec((tm, k), lambda i, j: (i, 0)),
            pl.BlockSpec((tm, tn), lambda i, j: (i, j)),
            pl.BlockSpec((tm, tn), lambda i, j: (i, n_j + j)),
            pl.BlockSpec((k, tn), lambda i, j: (0, j)),
            pl.BlockSpec((k, tn), lambda i, j: (0, j)),
        ],
        out_specs=pl.BlockSpec((tm, tn), lambda i, j: (i, j)),
        out_shape=jax.ShapeDtypeStruct((t, d), BF16),
        compiler_params=_params(("arbitrary", "arbitrary")),
        name="gated_merge",
    )(o_sb, o_fx, gates, gates, w_sb, w_fx)


def _out_proj_kernel(m_ref, x_ref, wout_ref, npost_ref, npre_ref, h_ref, u_ref):
    tm = m_ref.shape[0]
    chunks = [slice(r0, r0 + OUT_ROWS) for r0 in range(0, tm, OUT_ROWS)]
    mixes = [jnp.dot(m_ref[rows, :], wout_ref[...], preferred_element_type=F32)
             for rows in chunks]
    for rows, mix in zip(chunks, mixes):
        h = x_ref[rows, :] + _rms_norm(mix, npost_ref[...])
        h_ref[rows, :] = h
        u_ref[rows, :] = _rms_norm(h, npre_ref[...]).astype(BF16)


def _const_spec(shape):
    return pl.BlockSpec(shape, lambda i: (0,) * len(shape), pipeline_mode=pl.Buffered(1))


def _out_proj(merged, x2, w_out, n_post, n_ffn_pre):
    t, d = x2.shape
    tm = TM_OUT
    return pl.pallas_call(
        _out_proj_kernel,
        grid=(t // tm,),
        in_specs=[
            pl.BlockSpec((tm, d), lambda i: (i, 0)),
            pl.BlockSpec((tm, d), lambda i: (i, 0)),
            _const_spec((d, d)),
            _const_spec((1, d)),
            _const_spec((1, d)),
        ],
        out_specs=[
            pl.BlockSpec((tm, d), lambda i: (i, 0)),
            pl.BlockSpec((tm, d), lambda i: (i, 0)),
        ],
        out_shape=[
            jax.ShapeDtypeStruct((t, d), F32),
            jax.ShapeDtypeStruct((t, d), BF16),
        ],
        compiler_params=_params(("arbitrary",)),
        name="out_proj",
    )(merged, x2, w_out, n_post, n_ffn_pre)


def _ffn_kernel(u_ref, h_hbm_ref, wg_ref, wu_ref, wd_ref, npost_ref, o_ref, h_buf, h_sem):
    i, j = pl.program_id(0), pl.program_id(1)
    tm = o_ref.shape[0]
    h_copy = pltpu.make_async_copy(h_hbm_ref.at[pl.ds(i * tm, tm), :], h_buf, h_sem)

    @pl.when(j == 0)
    def _():
        h_copy.start()
        o_ref[...] = jnp.zeros_like(o_ref)

    u = u_ref[...]
    g = jnp.dot(u, wg_ref[...], preferred_element_type=F32)
    up = jnp.dot(u, wu_ref[...], preferred_element_type=F32)
    hidden = ((g * _sigmoid(g)) * up).astype(BF16)
    for c0 in range(0, o_ref.shape[1], DOWN_CHUNK):
        cols = slice(c0, c0 + DOWN_CHUNK)
        o_ref[:, cols] += jnp.dot(hidden, wd_ref[:, cols], preferred_element_type=F32)

    @pl.when(j == pl.num_programs(1) - 1)
    def _():
        h_copy.wait()
        for r0 in range(0, tm, NORM_ROWS):
            rows = slice(r0, r0 + NORM_ROWS)
            o_ref[rows, :] = h_buf[rows, :] + _rms_norm(o_ref[rows, :], npost_ref[...])


def _ffn(u2, h1, w_gate, w_up, w_down, n_post):
    t, d = h1.shape
    f = w_gate.shape[1]
    tm, tf = TM_FFN, TF_FFN
    return pl.pallas_call(
        _ffn_kernel,
        grid=(t // tm, f // tf),
        in_specs=[
            pl.BlockSpec((tm, d), lambda i, j: (i, 0)),
            pl.BlockSpec(memory_space=pl.ANY),
            pl.BlockSpec((d, tf), lambda i, j: (0, j)),
            pl.BlockSpec((d, tf), lambda i, j: (0, j)),
            pl.BlockSpec((tf, d), lambda i, j: (j, 0)),
            pl.BlockSpec((1, d), lambda i, j: (0, 0)),
        ],
        out_specs=pl.BlockSpec((tm, d), lambda i, j: (i, 0)),
        out_shape=jax.ShapeDtypeStruct((t, d), F32),
        scratch_shapes=[pltpu.VMEM((tm, d), F32), pltpu.SemaphoreType.DMA(())],
        compiler_params=_params(("arbitrary", "arbitrary")),
        name="swiglu_ffn",
    )(u2, h1, w_gate, w_up, w_down, n_post)


def kernel(x, norm_mix_pre, norm_mix_post, w_in, b_forget, w_branch_sb, w_branch_fox, w_out,
           norm_ffn_pre, norm_ffn_post, w_ffn_gate, w_ffn_up, w_ffn_down):
    batch, seq, d = x.shape
    depth = w_in.shape[0]
    n_qkv = 6 * D_BRANCH
    assert n_qkv % LANES == 0 and seq % TM_PROJ == 0
    h = x.reshape(batch * seq, d)
    for l in range(depth):
        w_all = w_in[l].astype(BF16)
        w_gates = w_all[:, n_qkv + N_HEADS:]
        b_f = jnp.pad(b_forget[l], (0, LANES - N_HEADS)).reshape(1, LANES)
        g_pre = norm_mix_pre[l].reshape(1, d)

        q_scale = jnp.full((D_BRANCH,), ATTN_SCALE * LOG2E, F32)
        ones = jnp.ones((D_BRANCH,), F32)
        col_scale = jnp.concatenate([q_scale, ones, ones, q_scale, ones, ones]).reshape(1, n_qkv)
        qkv, u1, wg_bf, wu_bf, wd_bf = _qkv_proj(
            h, g_pre, w_all, col_scale, [w_ffn_gate[l], w_ffn_up[l], w_ffn_down[l]])
        gates, cum_col, cum_row, wo_bf, wsb_bf, wfx_bf = _gates_proj(
            u1, w_gates, w_all, n_qkv // LANES, b_f, seq,
            [w_out[l], w_branch_sb[l], w_branch_fox[l]])
        o_sb, o_fx = _attention(qkv.reshape(batch, seq, n_qkv),
                                cum_col.reshape(batch, seq, LANES), cum_row, batch, seq)
        merged = _merge(o_sb.reshape(batch * seq, D_BRANCH), o_fx.reshape(batch * seq, D_BRANCH),
                        gates, wsb_bf, wfx_bf)
        h1, u2 = _out_proj(merged, h, wo_bf,
                           norm_mix_post[l].reshape(1, d), norm_ffn_pre[l].reshape(1, d))
        h = _ffn(u2, h1, wg_bf, wu_bf, wd_bf, norm_ffn_post[l].reshape(1, d))
    return h.reshape(batch, seq, d)
```

```python
import functools

import jax
import jax.numpy as jnp
from jax import lax
from jax.experimental import pallas as pl
from jax.experimental.pallas import tpu as pltpu

F32 = jnp.float32
BF16 = jnp.bfloat16

D_MODEL = 2048
HEAD_DIM = 128
N_HEADS = 8
D_BRANCH = N_HEADS * HEAD_DIM
D_FF = 5632
RMS_EPS = 1e-6
ATTN_SCALE = HEAD_DIM ** -0.5
NEG_BIG = -1e30
LOG2E = 1.4426950408889634
MAX_NEG_LOGIT = 126.0

LANES = 128
BF16_SUBLANES = 16
VMEM_LIMIT = 56 * 1024 * 1024

TM_PROJ = 1024
TN_PROJ = 1024
CUM_CHUNK = 256
TK = 256
RC = 256
QC = 256
SB_LOOKAHEAD_QK = 3
SB_LOOKAHEAD_CUM = 2
FOX_LOOKAHEAD = 3
TM_MERGE = 1024
TN_MERGE = 1024
TM_OUT = 512
OUT_ROWS = 256
TM_FFN = 1024
TF_FFN = 512
DOWN_CHUNK = 512
NORM_ROWS = 256


def _params(semantics):
    return pltpu.CompilerParams(dimension_semantics=semantics, vmem_limit_bytes=VMEM_LIMIT)


def _rms_norm(x, g):
    ms = jnp.mean(x * x, axis=-1, keepdims=True)
    return (x * lax.rsqrt(ms + RMS_EPS)) * g


def _log_sigmoid(x):
    return jnp.minimum(x, 0.0) - jnp.log1p(jnp.exp(-jnp.abs(x)))


def _sigmoid(x):
    return 1.0 / (1.0 + jnp.exp(-x))


def _split3(x):
    hi = x.astype(BF16)
    r1 = x - hi.astype(F32)
    mid = r1.astype(BF16)
    lo = (r1 - mid.astype(F32)).astype(BF16)
    return hi, mid, lo


def _rider_plan(weights, steps_each):
    plan, s0 = [], 0
    for w, c in zip(weights, steps_each):
        assert w.shape[0] % c == 0 and (w.shape[0] // c) % BF16_SUBLANES == 0
        plan.append((w, s0, c))
        s0 += c
    return plan


def _rider_specs(plan, n_j):
    specs, shapes = [], []
    for w, s0, c in plan:
        def index(i, j, s0=s0, c=c):
            return (jnp.clip(i * n_j + j - s0, 0, c - 1), 0)
        specs.append(pl.BlockSpec((w.shape[0] // c, w.shape[1]), index))
        shapes.append(jax.ShapeDtypeStruct(w.shape, BF16))
    return specs, shapes


def _run_riders(step, windows, src_refs, dst_refs):
    for (s0, c), src, dst in zip(windows, src_refs, dst_refs):
        @pl.when((step >= s0) & (step < s0 + c))
        def _(src=src, dst=dst):
            dst[...] = src[...].astype(BF16)


def _qkv_kernel(x_ref, g_ref, w_ref, cs_ref, *refs, windows):
    k = len(windows)
    src_refs, (o_ref, u_ref), dst_refs = refs[:k], refs[k:k + 2], refs[k + 2:]
    _run_riders(pl.program_id(0) * pl.num_programs(1) + pl.program_id(1), windows,
                src_refs, dst_refs)

    @pl.when(pl.program_id(1) == 0)
    def _():
        u_ref[...] = _rms_norm(x_ref[...], g_ref[...]).astype(BF16)

    acc = jnp.dot(u_ref[...], w_ref[...], preferred_element_type=F32)
    o_ref[...] = (acc * cs_ref[...]).astype(o_ref.dtype)


def _qkv_proj(x2, g, w, col_scale, ride):
    t, d = x2.shape
    n = col_scale.shape[1]
    n_i, n_j = t // TM_PROJ, n // TN_PROJ
    plan = _rider_plan(ride, [n_i * n_j // len(ride)] * len(ride))
    ride_specs, ride_shapes = _rider_specs(plan, n_j)
    return pl.pallas_call(
        functools.partial(_qkv_kernel, windows=[(s0, c) for _, s0, c in plan]),
        grid=(n_i, n_j),
        in_specs=[
            pl.BlockSpec((TM_PROJ, d), lambda i, j: (i, 0)),
            pl.BlockSpec((1, d), lambda i, j: (0, 0)),
            pl.BlockSpec((d, TN_PROJ), lambda i, j: (0, j)),
            pl.BlockSpec((1, TN_PROJ), lambda i, j: (0, j)),
        ] + ride_specs,
        out_specs=[
            pl.BlockSpec((TM_PROJ, TN_PROJ), lambda i, j: (i, j)),
            pl.BlockSpec((TM_PROJ, d), lambda i, j: (i, 0)),
        ] + ride_specs,
        out_shape=[
            jax.ShapeDtypeStruct((t, n), BF16),
            jax.ShapeDtypeStruct((t, d), BF16),
        ] + ride_shapes,
        compiler_params=_params(("arbitrary", "arbitrary")),
        name="qkv_proj",
    )(x2, g, w, col_scale, *ride)


def _gates_kernel(u_ref, w_ref, wf_ref, bf_ref, *refs, tiles_per_seq, windows):
    k = len(windows)
    src_refs, (o_ref, cumc_ref, cumr_ref), dst_refs = refs[:k], refs[k:k + 3], refs[k + 3:-1]
    carry_ref = refs[-1]
    i = pl.program_id(0)
    _run_riders(i * pl.num_programs(1) + pl.program_id(1), windows, src_refs, dst_refs)

    @pl.when(pl.program_id(1) == 0)
    def _():
        f = jnp.dot(u_ref[...], wf_ref[...], preferred_element_type=F32)
        log_f = _log_sigmoid(f + bf_ref[...])

        @pl.when(i % tiles_per_seq == 0)
        def _():
            carry_ref[...] = jnp.zeros_like(carry_ref)

        r = lax.broadcasted_iota(jnp.int32, (CUM_CHUNK, CUM_CHUNK), 0)
        c = lax.broadcasted_iota(jnp.int32, (CUM_CHUNK, CUM_CHUNK), 1)
        lower = (c <= r).astype(BF16)
        carry = carry_ref[...]
        for ch in range(TM_PROJ // CUM_CHUNK):
            blk = log_f[ch * CUM_CHUNK:(ch + 1) * CUM_CHUNK]
            hi, mid, lo = _split3(blk)
            cs = (jnp.dot(lower, hi, preferred_element_type=F32)
                  + jnp.dot(lower, mid, preferred_element_type=F32)
                  + jnp.dot(lower, lo, preferred_element_type=F32)) + carry
            cumc_ref[ch * CUM_CHUNK:(ch + 1) * CUM_CHUNK, :] = cs
            carry = cs[CUM_CHUNK - 1:CUM_CHUNK, :]
        carry_ref[...] = carry
        cumr_ref[0] = cumc_ref[...].T[:N_HEADS, :]

    acc = jnp.dot(u_ref[...], w_ref[...], preferred_element_type=F32)
    o_ref[...] = _sigmoid(acc).astype(o_ref.dtype)


def _gates_proj(u, w, w_all, f_block, bf, seq, ride):
    t, d = u.shape
    n = w.shape[1]
    tiles_per_seq = seq // TM_PROJ
    n_i, n_j = t // TM_PROJ, n // TN_PROJ
    ride_rows = sum(r.shape[0] for r in ride)
    plan = _rider_plan(ride, [n_i * n_j * r.shape[0] // ride_rows for r in ride])
    ride_specs, ride_shapes = _rider_specs(plan, n_j)
    return pl.pallas_call(
        functools.partial(_gates_kernel, tiles_per_seq=tiles_per_seq,
                          windows=[(s0, c) for _, s0, c in plan]),
        grid=(n_i, n_j),
        in_specs=[
            pl.BlockSpec((TM_PROJ, d), lambda i, j: (i, 0)),
            pl.BlockSpec((d, TN_PROJ), lambda i, j: (0, j)),
            pl.BlockSpec((d, LANES), lambda i, j: (0, f_block)),
            pl.BlockSpec((1, LANES), lambda i, j: (0, 0)),
        ] + ride_specs,
        out_specs=[
            pl.BlockSpec((TM_PROJ, TN_PROJ), lambda i, j: (i, j)),
            pl.BlockSpec((TM_PROJ, LANES), lambda i, j: (i, 0)),
            pl.BlockSpec((1, N_HEADS, TM_PROJ),
                         lambda i, j: (i // tiles_per_seq, 0, i % tiles_per_seq)),
        ] + ride_specs,
        out_shape=[
            jax.ShapeDtypeStruct((t, n), BF16),
            jax.ShapeDtypeStruct((t, LANES), F32),
            jax.ShapeDtypeStruct((t // seq, N_HEADS, seq), F32),
        ] + ride_shapes,
        scratch_shapes=[pltpu.VMEM((1, LANES), F32)],
        compiler_params=_params(("arbitrary", "arbitrary")),
        name="gates_proj",
    )(u, w, w_all, bf, *ride)


def _qk(q, k):
    return lax.dot_general(q, k, (((1,), (1,)), ((), ())), preferred_element_type=F32)


def _sb_kernel(q_ref, k_ref, v_ref, o_ref, *, seq):
    r = lax.broadcasted_iota(jnp.int32, (TK, TK), 0)
    c = lax.broadcasted_iota(jnp.int32, (TK, TK), 1)
    later = (r > c).astype(BF16)

    def valid_mask(d, c):
        k0, r0 = d * TK, c * RC
        if k0 + TK <= r0:
            return None
        return (k0 + lax.broadcasted_iota(jnp.int32, (RC, TK), 1)
                < r0 + lax.broadcasted_iota(jnp.int32, (RC, TK), 0))

    def gates(d, c):
        q = q_ref[0, c * RC:(c + 1) * RC, :]
        k = k_ref[0, d * TK:(d + 1) * TK, :]
        nz = jnp.minimum(_qk(q, k), MAX_NEG_LOGIT)
        neg_log_beta = jnp.log2(1.0 + jnp.exp2(nz))
        log_keep = nz - neg_log_beta
        valid = valid_mask(d, c)
        if valid is not None:
            log_keep = jnp.where(valid, log_keep, 0.0)
        return neg_log_beta, log_keep.astype(BF16), jnp.sum(log_keep, axis=1, keepdims=True)

    def within_block(neg_log_beta, keep_bf, row_sum):
        return jnp.dot(keep_bf, later, preferred_element_type=F32) - neg_log_beta, row_sum

    def accumulate(d, c, log_w, row_sum, carry, acc):
        v = v_ref[0, d * TK:(d + 1) * TK, :]
        w = jnp.exp2(log_w + carry)
        valid = valid_mask(d, c)
        if valid is not None:
            w = jnp.where(valid, w, 0.0)
        return carry + row_sum, acc + jnp.dot(w.astype(BF16), v, preferred_element_type=F32)

    n_chunks = seq // RC
    state = [(jnp.zeros((RC, 1), F32), jnp.zeros((RC, HEAD_DIM), F32)) for _ in range(n_chunks)]
    items = [(d, c) for d in reversed(range(seq // TK)) for c in range(d * TK // RC, n_chunks)]
    n = len(items)
    staged_a, staged_b = {}, {}
    for step in range(-SB_LOOKAHEAD_QK, n):
        i_a, i_b = step + SB_LOOKAHEAD_QK, step + SB_LOOKAHEAD_CUM
        if i_a < n:
            staged_a[i_a] = gates(*items[i_a])
        if 0 <= i_b < n:
            staged_b[i_b] = within_block(*staged_a.pop(i_b))
        if step >= 0:
            d, c = items[step]
            state[c] = accumulate(d, c, *staged_b.pop(step), *state[c])
    for c in range(n_chunks):
        o_ref[0, c * RC:(c + 1) * RC, :] = state[c][1].astype(o_ref.dtype)


def _fox_kernel(q_ref, k_ref, v_ref, cq_ref, ck_ref, o_ref, *, seq):
    h = pl.program_id(1)
    lane = lax.broadcasted_iota(jnp.int32, (TK, LANES), 1)
    ones_col = (lane == 0).astype(BF16)
    n_blocks = seq // TK
    ck_blocks = [jnp.sum(jnp.where(lane == h, ck_ref[0, d * TK:(d + 1) * TK, :], 0.0),
                         axis=1, keepdims=True) * LOG2E for d in range(n_blocks)]
    v_blocks = [jnp.concatenate([v_ref[0, d * TK:(d + 1) * TK, :], ones_col], axis=1)
                for d in range(n_blocks)]

    def scores(d, c):
        k0, c0 = d * TK, c * QC
        q = q_ref[0, c0:c0 + QC, :]
        k = k_ref[0, k0:k0 + TK, :]
        y = _qk(k, q) - ck_blocks[d]
        if k0 + TK > c0:
            valid = (k0 + lax.broadcasted_iota(jnp.int32, (TK, QC), 0)
                     <= c0 + lax.broadcasted_iota(jnp.int32, (TK, QC), 1))
            y = jnp.where(valid, y, NEG_BIG)
        return y

    def accumulate(d, c, y, m, l, acc):
        c0 = c * QC
        cq = cq_ref[0, pl.ds(h, 1), c0:c0 + QC] * LOG2E
        m_new = jnp.maximum(m, cq + jnp.max(y, axis=0, keepdims=True))
        alpha = jnp.exp2(m - m_new)
        p = jnp.exp2(y - (m_new - cq))
        pv = lax.dot_general(v_blocks[d], p.astype(BF16), (((0,), (0,)), ((), ())),
                             preferred_element_type=F32)
        l = alpha * l + pv[HEAD_DIM:HEAD_DIM + 1]
        return m_new, l, alpha * acc + pv[:HEAD_DIM]

    n_chunks = seq // QC
    state = [(jnp.full((1, QC), NEG_BIG, F32), jnp.zeros((1, QC), F32),
              jnp.zeros((HEAD_DIM, QC), F32)) for _ in range(n_chunks)]
    items = [(d, c) for d in range(seq // TK) for c in range(d * TK // QC, n_chunks)]
    ys = [scores(*item) for item in items[:FOX_LOOKAHEAD]]
    for i, (d, c) in enumerate(items):
        if i + FOX_LOOKAHEAD < len(items):
            ys.append(scores(*items[i + FOX_LOOKAHEAD]))
        state[c] = accumulate(d, c, ys.pop(0), *state[c])
    for c in range(n_chunks):
        _, l, acc = state[c]
        o_ref[0, c * QC:(c + 1) * QC, :] = (acc / l).T.astype(o_ref.dtype)


def _attention(qkv, cum_col, cum_row, batch, seq):
    grid = (batch, N_HEADS)

    def head_spec(head0):
        return pl.BlockSpec((1, seq, HEAD_DIM), lambda b, h: (b, 0, head0 + h))

    out_shape = jax.ShapeDtypeStruct((batch, seq, D_BRANCH), BF16)
    sem = ("arbitrary", "arbitrary")

    o_sb = pl.pallas_call(
        functools.partial(_sb_kernel, seq=seq),
        grid=grid,
        in_specs=[head_spec(0), head_spec(N_HEADS), head_spec(2 * N_HEADS)],
        out_specs=head_spec(0),
        out_shape=out_shape,
        compiler_params=_params(sem),
        name="stickbreak_attn",
    )(qkv, qkv, qkv)

    o_fx = pl.pallas_call(
        functools.partial(_fox_kernel, seq=seq),
        grid=grid,
        in_specs=[
            head_spec(3 * N_HEADS), head_spec(4 * N_HEADS), head_spec(5 * N_HEADS),
            pl.BlockSpec((1, N_HEADS, seq), lambda b, h: (b, 0, 0)),
            pl.BlockSpec((1, seq, LANES), lambda b, h: (b, 0, 0)),
        ],
        out_specs=head_spec(0),
        out_shape=out_shape,
        compiler_params=_params(sem),
        name="forgetting_attn",
    )(qkv, qkv, qkv, cum_row, cum_col)
    return o_sb, o_fx


def _merge_kernel(osb_ref, ofx_ref, gsb_ref, gfx_ref, wsb_ref, wfx_ref, o_ref):
    a = jnp.dot(osb_ref[...], wsb_ref[...], preferred_element_type=F32)
    b = jnp.dot(ofx_ref[...], wfx_ref[...], preferred_element_type=F32)
    o_ref[...] = (gsb_ref[...].astype(F32) * a + gfx_ref[...].astype(F32) * b).astype(o_ref.dtype)


def _merge(o_sb, o_fx, gates, w_sb, w_fx):
    t, k = o_sb.shape
    d = w_sb.shape[1]
    tm, tn = TM_MERGE, TN_MERGE
    n_j = d // tn
    return pl.pallas_call(
        _merge_kernel,
        grid=(t // tm, n_j),
        in_specs=[
            pl.BlockSpec((tm, k), lambda i, j: (i, 0)),
            pl.BlockSpec((tm, k), lambda i, j: (i, 0)),
            pl.BlockSpec((tm, tn), lambda i, j: (i, j)),
            pl.BlockSpec((tm, tn), lambda i, j: (i, n_j + j)),
            pl.BlockSpec((k, tn), lambda i, j: (0, j)),
            pl.BlockSpec((k, tn), lambda i, j: (0, j)),
        ],
        out_specs=pl.BlockSpec((tm, tn), lambda i, j: (i, j)),
        out_shape=jax.ShapeDtypeStruct((t, d), BF16),
        compiler_params=_params(("arbitrary", "arbitrary")),
        name="gated_merge",
    )(o_sb, o_fx, gates, gates, w_sb, w_fx)


def _out_proj_kernel(m_ref, x_ref, wout_ref, npost_ref, npre_ref, h_ref, u_ref):
    tm = m_ref.shape[0]
    chunks = [slice(r0, r0 + OUT_ROWS) for r0 in range(0, tm, OUT_ROWS)]
    mixes = [jnp.dot(m_ref[rows, :], wout_ref[...], preferred_element_type=F32)
             for rows in chunks]
    for rows, mix in zip(chunks, mixes):
        h = x_ref[rows, :] + _rms_norm(mix, npost_ref[...])
        h_ref[rows, :] = h
        u_ref[rows, :] = _rms_norm(h, npre_ref[...]).astype(BF16)


def _const_spec(shape):
    return pl.BlockSpec(shape, lambda i: (0,) * len(shape), pipeline_mode=pl.Buffered(1))


def _out_proj(merged, x2, w_out, n_post, n_ffn_pre):
    t, d = x2.shape
    tm = TM_OUT
    return pl.pallas_call(
        _out_proj_kernel,
        grid=(t // tm,),
        in_specs=[
            pl.BlockSpec((tm, d), lambda i: (i, 0)),
            pl.BlockSpec((tm, d), lambda i: (i, 0)),
            _const_spec((d, d)),
            _const_spec((1, d)),
            _const_spec((1, d)),
        ],
        out_specs=[
            pl.BlockSpec((tm, d), lambda i: (i, 0)),
            pl.BlockSpec((tm, d), lambda i: (i, 0)),
        ],
        out_shape=[
            jax.ShapeDtypeStruct((t, d), F32),
            jax.ShapeDtypeStruct((t, d), BF16),
        ],
        compiler_params=_params(("arbitrary",)),
        name="out_proj",
    )(merged, x2, w_out, n_post, n_ffn_pre)


def _ffn_kernel(u_ref, h_hbm_ref, wg_ref, wu_ref, wd_ref, npost_ref, o_ref, h_buf, h_sem):
    i, j = pl.program_id(0), pl.program_id(1)
    tm = o_ref.shape[0]
    h_copy = pltpu.make_async_copy(h_hbm_ref.at[pl.ds(i * tm, tm), :], h_buf, h_sem)

    @pl.when(j == 0)
    def _():
        h_copy.start()
        o_ref[...] = jnp.zeros_like(o_ref)

    u = u_ref[...]
    g = jnp.dot(u, wg_ref[...], preferred_element_type=F32)
    up = jnp.dot(u, wu_ref[...], preferred_element_type=F32)
    hidden = ((g * _sigmoid(g)) * up).astype(BF16)
    for c0 in range(0, o_ref.shape[1], DOWN_CHUNK):
        cols = slice(c0, c0 + DOWN_CHUNK)
        o_ref[:, cols] += jnp.dot(hidden, wd_ref[:, cols], preferred_element_type=F32)

    @pl.when(j == pl.num_programs(1) - 1)
    def _():
        h_copy.wait()
        for r0 in range(0, tm, NORM_ROWS):
            rows = slice(r0, r0 + NORM_ROWS)
            o_ref[rows, :] = h_buf[rows, :] + _rms_norm(o_ref[rows, :], npost_ref[...])


def _ffn(u2, h1, w_gate, w_up, w_down, n_post):
    t, d = h1.shape
    f = w_gate.shape[1]
    tm, tf = TM_FFN, TF_FFN
    return pl.pallas_call(
        _ffn_kernel,
        grid=(t // tm, f // tf),
        in_specs=[
            pl.BlockSpec((tm, d), lambda i, j: (i, 0)),
            pl.BlockSpec(memory_space=pl.ANY),
            pl.BlockSpec((d, tf), lambda i, j: (0, j)),
            pl.BlockSpec((d, tf), lambda i, j: (0, j)),
            pl.BlockSpec((tf, d), lambda i, j: (j, 0)),
            pl.BlockSpec((1, d), lambda i, j: (0, 0)),
        ],
        out_specs=pl.BlockSpec((tm, d), lambda i, j: (i, 0)),
        out_shape=jax.ShapeDtypeStruct((t, d), F32),
        scratch_shapes=[pltpu.VMEM((tm, d), F32), pltpu.SemaphoreType.DMA(())],
        compiler_params=_params(("arbitrary", "arbitrary")),
        name="swiglu_ffn",
    )(u2, h1, w_gate, w_up, w_down, n_post)


def kernel(x, norm_mix_pre, norm_mix_post, w_in, b_forget, w_branch_sb, w_branch_fox, w_out,
           norm_ffn_pre, norm_ffn_post, w_ffn_gate, w_ffn_up, w_ffn_down):
    batch, seq, d = x.shape
    depth = w_in.shape[0]
    n_qkv = 6 * D_BRANCH
    assert n_qkv % LANES == 0 and seq % TM_PROJ == 0
    h = x.reshape(batch * seq, d)
    for l in range(depth):
        w_all = w_in[l].astype(BF16)
        w_gates = w_all[:, n_qkv + N_HEADS:]
        b_f = jnp.pad(b_forget[l], (0, LANES - N_HEADS)).reshape(1, LANES)
        g_pre = norm_mix_pre[l].reshape(1, d)

        q_scale = jnp.full((D_BRANCH,), ATTN_SCALE * LOG2E, F32)
        ones = jnp.ones((D_BRANCH,), F32)
        col_scale = jnp.concatenate([-q_scale, ones, ones, q_scale, ones, ones]).reshape(1, n_qkv)
        qkv, u1, wg_bf, wu_bf, wd_bf = _qkv_proj(
            h, g_pre, w_all, col_scale, [w_ffn_gate[l], w_ffn_up[l], w_ffn_down[l]])
        gates, cum_col, cum_row, wo_bf, wsb_bf, wfx_bf = _gates_proj(
            u1, w_gates, w_all, n_qkv // LANES, b_f, seq,
            [w_out[l], w_branch_sb[l], w_branch_fox[l]])
        o_sb, o_fx = _attention(qkv.reshape(batch, seq, n_qkv),
                                cum_col.reshape(batch, seq, LANES), cum_row, batch, seq)
        merged = _merge(o_sb.reshape(batch * seq, D_BRANCH), o_fx.reshape(batch * seq, D_BRANCH),
                        gates, wsb_bf, wfx_bf)
        h1, u2 = _out_proj(merged, h, wo_bf,
                           norm_mix_post[l].reshape(1, d), norm_ffn_pre[l].reshape(1, d))
        h = _ffn(u2, h1, wg_bf, wu_bf, wd_bf, norm_ffn_post[l].reshape(1, d))
    return h.reshape(batch, seq, d)
```

```python
import functools

import jax
import jax.numpy as jnp
from jax import lax
from jax.experimental import pallas as pl
from jax.experimental.pallas import tpu as pltpu

F32 = jnp.float32
BF16 = jnp.bfloat16

D_MODEL = 2048
HEAD_DIM = 128
N_HEADS = 8
D_BRANCH = N_HEADS * HEAD_DIM
D_FF = 5632
RMS_EPS = 1e-6
ATTN_SCALE = HEAD_DIM ** -0.5
NEG_BIG = -1e30
LOG2E = 1.4426950408889634
MAX_NEG_LOGIT = 126.0

LANES = 128
BF16_SUBLANES = 16
VMEM_LIMIT = 56 * 1024 * 1024

TM_PROJ = 1024
TN_PROJ = 1024
CUM_CHUNK = 256
TK = 256
RC = 256
QC = 256
SB_LOOKAHEAD_QK = 3
SB_LOOKAHEAD_CUM = 2
FOX_LOOKAHEAD = 3
TM_MERGE = 1024
TN_MERGE = 1024
TM_OUT = 512
OUT_ROWS = 256
TM_FFN = 1024
TF_FFN = 512
DOWN_CHUNK = 512
NORM_ROWS = 256


def _params(semantics):
    return pltpu.CompilerParams(dimension_semantics=semantics, vmem_limit_bytes=VMEM_LIMIT)


def _rms_norm(x, g):
    ms = jnp.mean(x * x, axis=-1, keepdims=True)
    return (x * lax.rsqrt(ms + RMS_EPS)) * g


def _log_sigmoid(x):
    return jnp.minimum(x, 0.0) - jnp.log1p(jnp.exp(-jnp.abs(x)))


def _sigmoid(x):
    return 1.0 / (1.0 + jnp.exp(-x))


def _split3(x):
    hi = x.astype(BF16)
    r1 = x - hi.astype(F32)
    mid = r1.astype(BF16)
    lo = (r1 - mid.astype(F32)).astype(BF16)
    return hi, mid, lo


def _rider_plan(weights, steps_each):
    plan, s0 = [], 0
    for w, c in zip(weights, steps_each):
        assert w.shape[0] % c == 0 and (w.shape[0] // c) % BF16_SUBLANES == 0
        plan.append((w, s0, c))
        s0 += c
    return plan


def _rider_specs(plan, n_j):
    specs, shapes = [], []
    for w, s0, c in plan:
        def index(i, j, s0=s0, c=c):
            return (jnp.clip(i * n_j + j - s0, 0, c - 1), 0)
        specs.append(pl.BlockSpec((w.shape[0] // c, w.shape[1]), index))
        shapes.append(jax.ShapeDtypeStruct(w.shape, BF16))
    return specs, shapes


def _run_riders(step, windows, src_refs, dst_refs):
    for (s0, c), src, dst in zip(windows, src_refs, dst_refs):
        @pl.when((step >= s0) & (step < s0 + c))
        def _(src=src, dst=dst):
            dst[...] = src[...].astype(BF16)


def _qkv_kernel(x_ref, g_ref, w_ref, cs_ref, *refs, windows):
    k = len(windows)
    src_refs, (o_ref, u_ref), dst_refs = refs[:k], refs[k:k + 2], refs[k + 2:]
    _run_riders(pl.program_id(0) * pl.num_programs(1) + pl.program_id(1), windows,
                src_refs, dst_refs)

    @pl.when(pl.program_id(1) == 0)
    def _():
        u_ref[...] = _rms_norm(x_ref[...], g_ref[...]).astype(BF16)

    acc = jnp.dot(u_ref[...], w_ref[...], preferred_element_type=F32)
    o_ref[...] = (acc * cs_ref[...]).astype(o_ref.dtype)


def _qkv_proj(x2, g, w, col_scale, ride):
    t, d = x2.shape
    n = col_scale.shape[1]
    n_i, n_j = t // TM_PROJ, n // TN_PROJ
    plan = _rider_plan(ride, [n_i * n_j // len(ride)] * len(ride))
    ride_specs, ride_shapes = _rider_specs(plan, n_j)
    return pl.pallas_call(
        functools.partial(_qkv_kernel, windows=[(s0, c) for _, s0, c in plan]),
        grid=(n_i, n_j),
        in_specs=[
            pl.BlockSpec((TM_PROJ, d), lambda i, j: (i, 0)),
            pl.BlockSpec((1, d), lambda i, j: (0, 0)),
            pl.BlockSpec((d, TN_PROJ), lambda i, j: (0, j)),
            pl.BlockSpec((1, TN_PROJ), lambda i, j: (0, j)),
        ] + ride_specs,
        out_specs=[
            pl.BlockSpec((TM_PROJ, TN_PROJ), lambda i, j: (i, j)),
            pl.BlockSpec((TM_PROJ, d), lambda i, j: (i, 0)),
        ] + ride_specs,
        out_shape=[
            jax.ShapeDtypeStruct((t, n), BF16),
            jax.ShapeDtypeStruct((t, d), BF16),
        ] + ride_shapes,
        compiler_params=_params(("arbitrary", "arbitrary")),
        name="qkv_proj",
    )(x2, g, w, col_scale, *ride)


def _gates_kernel(u_ref, w_ref, wf_ref, bf_ref, *refs, tiles_per_seq, windows):
    k = len(windows)
    src_refs, (o_ref, cumc_ref, cumr_ref), dst_refs = refs[:k], refs[k:k + 3], refs[k + 3:-1]
    carry_ref = refs[-1]
    i = pl.program_id(0)
    _run_riders(i * pl.num_programs(1) + pl.program_id(1), windows, src_refs, dst_refs)

    @pl.when(pl.program_id(1) == 0)
    def _():
        f = jnp.dot(u_ref[...], wf_ref[...], preferred_element_type=F32)
        log_f = _log_sigmoid(f + bf_ref[...])

        @pl.when(i % tiles_per_seq == 0)
        def _():
            carry_ref[...] = jnp.zeros_like(carry_ref)

        r = lax.broadcasted_iota(jnp.int32, (CUM_CHUNK, CUM_CHUNK), 0)
        c = lax.broadcasted_iota(jnp.int32, (CUM_CHUNK, CUM_CHUNK), 1)
        lower = (c <= r).astype(BF16)
        carry = carry_ref[...]
        for ch in range(TM_PROJ // CUM_CHUNK):
            blk = log_f[ch * CUM_CHUNK:(ch + 1) * CUM_CHUNK]
            hi, mid, lo = _split3(blk)
            cs = (jnp.dot(lower, hi, preferred_element_type=F32)
                  + jnp.dot(lower, mid, preferred_element_type=F32)
                  + jnp.dot(lower, lo, preferred_element_type=F32)) + carry
            cumc_ref[ch * CUM_CHUNK:(ch + 1) * CUM_CHUNK, :] = cs
            carry = cs[CUM_CHUNK - 1:CUM_CHUNK, :]
        carry_ref[...] = carry
        cumr_ref[0] = cumc_ref[...].T[:N_HEADS, :]

    acc = jnp.dot(u_ref[...], w_ref[...], preferred_element_type=F32)
    o_ref[...] = _sigmoid(acc).astype(o_ref.dtype)


def _gates_proj(u, w, w_all, f_block, bf, seq, ride):
    t, d = u.shape
    n = w.shape[1]
    tiles_per_seq = seq // TM_PROJ
    n_i, n_j = t // TM_PROJ, n // TN_PROJ
    ride_rows = sum(r.shape[0] for r in ride)
    plan = _rider_plan(ride, [n_i * n_j * r.shape[0] // ride_rows for r in ride])
    ride_specs, ride_shapes = _rider_specs(plan, n_j)
    return pl.pallas_call(
        functools.partial(_gates_kernel, tiles_per_seq=tiles_per_seq,
                          windows=[(s0, c) for _, s0, c in plan]),
        grid=(n_i, n_j),
        in_specs=[
            pl.BlockSpec((TM_PROJ, d), lambda i, j: (i, 0)),
            pl.BlockSpec((d, TN_PROJ), lambda i, j: (0, j)),
            pl.BlockSpec((d, LANES), lambda i, j: (0, f_block)),
            pl.BlockSpec((1, LANES), lambda i, j: (0, 0)),
        ] + ride_specs,
        out_specs=[
            pl.BlockSpec((TM_PROJ, TN_PROJ), lambda i, j: (i, j)),
            pl.BlockSpec((TM_PROJ, LANES), lambda i, j: (i, 0)),
            pl.BlockSpec((1, N_HEADS, TM_PROJ),
                         lambda i, j: (i // tiles_per_seq, 0, i % tiles_per_seq)),
        ] + ride_specs,
        out_shape=[
            jax.ShapeDtypeStruct((t, n), BF16),
            jax.ShapeDtypeStruct((t, LANES), F32),
            jax.ShapeDtypeStruct((t // seq, N_HEADS, seq), F32),
        ] + ride_shapes,
        scratch_shapes=[pltpu.VMEM((1, LANES), F32)],
        compiler_params=_params(("arbitrary", "arbitrary")),
        name="gates_proj",
    )(u, w, w_all, bf, *ride)


def _qk(q, k):
    return lax.dot_general(q, k, (((1,), (1,)), ((), ())), preferred_element_type=F32)


def _sb_kernel(q_ref, k_ref, v_ref, o_ref, *, seq):
    r = lax.broadcasted_iota(jnp.int32, (TK, TK), 0)
    c = lax.broadcasted_iota(jnp.int32, (TK, TK), 1)
    later = (r > c).astype(BF16)

    def valid_mask(d, c):
        k0, r0 = d * TK, c * RC
        if k0 + TK <= r0:
            return None
        return (k0 + lax.broadcasted_iota(jnp.int32, (RC, TK), 1)
                < r0 + lax.broadcasted_iota(jnp.int32, (RC, TK), 0))

    def gates(d, c):
        q = q_ref[0, c * RC:(c + 1) * RC, :]
        k = k_ref[0, d * TK:(d + 1) * TK, :]
        nz = jnp.minimum(_qk(q, k), MAX_NEG_LOGIT)
        neg_log_beta = jnp.log2(1.0 + jnp.exp2(nz))
        log_keep = nz - neg_log_beta
        valid = valid_mask(d, c)
        if valid is not None:
            log_keep = jnp.where(valid, log_keep, 0.0)
        return neg_log_beta, log_keep.astype(BF16), jnp.sum(log_keep, axis=1, keepdims=True)

    def within_block(neg_log_beta, keep_bf, row_sum):
        return jnp.dot(keep_bf, later, preferred_element_type=F32) - neg_log_beta, row_sum

    def accumulate(d, c, log_w, row_sum, carry, acc):
        v = v_ref[0, d * TK:(d + 1) * TK, :]
        w = jnp.exp2(log_w + carry)
        valid = valid_mask(d, c)
        if valid is not None:
            w = jnp.where(valid, w, 0.0)
        return carry + row_sum, acc + jnp.dot(w.astype(BF16), v, preferred_element_type=F32)

    n_chunks = seq // RC
    state = [(jnp.zeros((RC, 1), F32), jnp.zeros((RC, HEAD_DIM), F32)) for _ in range(n_chunks)]
    items = [(d, c) for d in reversed(range(seq // TK)) for c in range(d * TK // RC, n_chunks)]
    n = len(items)
    staged_a, staged_b = {}, {}
    for step in range(-SB_LOOKAHEAD_QK, n):
        i_a, i_b = step + SB_LOOKAHEAD_QK, step + SB_LOOKAHEAD_CUM
        if i_a < n:
            staged_a[i_a] = gates(*items[i_a])
        if 0 <= i_b < n:
            staged_b[i_b] = within_block(*staged_a.pop(i_b))
        if step >= 0:
            d, c = items[step]
            state[c] = accumulate(d, c, *staged_b.pop(step), *state[c])
    for c in range(n_chunks):
        o_ref[0, c * RC:(c + 1) * RC, :] = state[c][1].astype(o_ref.dtype)


def _fox_kernel(q_ref, k_ref, v_ref, cq_ref, ck_ref, o_ref, *, seq):
    h = pl.program_id(1)
    lane = lax.broadcasted_iota(jnp.int32, (TK, LANES), 1)
    ones_col = (lane == 0).astype(BF16)
    n_blocks = seq // TK
    ck_blocks = [jnp.sum(jnp.where(lane == h, ck_ref[0, d * TK:(d + 1) * TK, :], 0.0),
                         axis=1, keepdims=True) * LOG2E for d in range(n_blocks)]
    v_blocks = [jnp.concatenate([v_ref[0, d * TK:(d + 1) * TK, :], ones_col], axis=1)
                for d in range(n_blocks)]

    def scores(d, c):
        k0, c0 = d * TK, c * QC
        q = q_ref[0, c0:c0 + QC, :]
        k = k_ref[0, k0:k0 + TK, :]
        y = _qk(k, q) - ck_blocks[d]
        if k0 + TK > c0:
            valid = (k0 + lax.broadcasted_iota(jnp.int32, (TK, QC), 0)
                     <= c0 + lax.broadcasted_iota(jnp.int32, (TK, QC), 1))
            y = jnp.where(valid, y, NEG_BIG)
        return y

    def accumulate(d, c, y, m, l, acc):
        c0 = c * QC
        cq = cq_ref[0, pl.ds(h, 1), c0:c0 + QC] * LOG2E
        m_new = jnp.maximum(m, cq + jnp.max(y, axis=0, keepdims=True))
        alpha = jnp.exp2(m - m_new)
        p = jnp.exp2(y - (m_new - cq))
        pv = lax.dot_general(v_blocks[d], p.astype(BF16), (((0,), (0,)), ((), ())),
                             preferred_element_type=F32)
        l = alpha * l + pv[HEAD_DIM:HEAD_DIM + 1]
        return m_new, l, alpha * acc + pv[:HEAD_DIM]

    n_chunks = seq // QC
    state = [(jnp.full((1, QC), NEG_BIG, F32), jnp.zeros((1, QC), F32),
              jnp.zeros((HEAD_DIM, QC), F32)) for _ in range(n_chunks)]
    items = [(d, c) for d in range(seq // TK) for c in range(d * TK // QC, n_chunks)]
    ys = [scores(*item) for item in items[:FOX_LOOKAHEAD]]
    for i, (d, c) in enumerate(items):
        if i + FOX_LOOKAHEAD < len(items):
            ys.append(scores(*items[i + FOX_LOOKAHEAD]))
        state[c] = accumulate(d, c, ys.pop(0), *state[c])
    for c in range(n_chunks):
        _, l, acc = state[c]
        o_ref[0, c * QC:(c + 1) * QC, :] = (acc / l).T.astype(o_ref.dtype)


def _attention(qkv, cum_col, cum_row, batch, seq):
    grid = (batch, N_HEADS)

    def head_spec(head0):
        return pl.BlockSpec((1, seq, HEAD_DIM), lambda b, h: (b, 0, head0 + h))

    out_shape = jax.ShapeDtypeStruct((batch, seq, D_BRANCH), BF16)
    sem = ("arbitrary", "arbitrary")

    o_sb = pl.pallas_call(
        functools.partial(_sb_kernel, seq=seq),
        grid=grid,
        in_specs=[head_spec(0), head_spec(N_HEADS), head_spec(2 * N_HEADS)],
        out_specs=head_spec(0),
        out_shape=out_shape,
        compiler_params=_params(sem),
        name="stickbreak_attn",
    )(qkv, qkv, qkv)

    o_fx = pl.pallas_call(
        functools.partial(_fox_kernel, seq=seq),
        grid=grid,
        in_specs=[
            head_spec(3 * N_HEADS), head_spec(4 * N_HEADS), head_spec(5 * N_HEADS),
            pl.BlockSpec((1, N_HEADS, seq), lambda b, h: (b, 0, 0)),
            pl.BlockSpec((1, seq, LANES), lambda b, h: (b, 0, 0)),
        ],
        out_specs=head_spec(0),
        out_shape=out_shape,
        compiler_params=_params(sem),
        name="forgetting_attn",
    )(qkv, qkv, qkv, cum_row, cum_col)
    return o_sb, o_fx


def _merge_kernel(osb_ref, ofx_ref, gsb_ref, gfx_ref, wsb_ref, wfx_ref, o_ref):
    a = jnp.dot(osb_ref[...], wsb_ref[...], preferred_element_type=F32)
    b = jnp.dot(ofx_ref[...], wfx_ref[...], preferred_element_type=F32)
    o_ref[...] = (gsb_ref[...].astype(F32) * a + gfx_ref[...].astype(F32) * b).astype(o_ref.dtype)


def _merge(o_sb, o_fx, gates, w_sb, w_fx):
    t, k = o_sb.shape
    d = w_sb.shape[1]
    tm, tn = TM_MERGE, TN_MERGE
    n_j = d // tn
    return pl.pallas_call(
        _merge_kernel,
        grid=(t // tm, n_j),
        in_specs=[
            pl.BlockSpec((tm, k), lambda i, j: (i, 0)),
            pl.BlockSpec((tm, k), lambda i, j: (i, 0)),
            pl.BlockSpec((tm, tn), lambda i, j: (i, j)),
            pl.BlockSpec((tm, tn), lambda i, j: (i, n_j + j)),
            pl.BlockSpec((k, tn), lambda i, j: (0, j)),
            pl.BlockSpec((k, tn), lambda i, j: (0, j)),
        ],
        out_specs=pl.BlockSpec((tm, tn), lambda i, j: (i, j)),
        out_shape=jax.ShapeDtypeStruct((t, d), BF16),
        compiler_params=_params(("arbitrary", "arbitrary")),
        name="gated_merge",
    )(o_sb, o_fx, gates, gates, w_sb, w_fx)


def _out_proj_kernel(m_ref, x_ref, wout_ref, npost_ref, npre_ref, h_ref, u_ref):
    tm = m_ref.shape[0]
    chunks = [slice(r0, r0 + OUT_ROWS) for r0 in range(0, tm, OUT_ROWS)]
    mixes = [jnp.dot(m_ref[rows, :], wout_ref[...], preferred_element_type=F32)
             for rows in chunks]
    for rows, mix in zip(chunks, mixes):
        h = x_ref[rows, :] + _rms_norm(mix, npost_ref[...])
        h_ref[rows, :] = h
        u_ref[rows, :] = _rms_norm(h, npre_ref[...]).astype(BF16)


def _const_spec(shape):
    return pl.BlockSpec(shape, lambda i: (0,) * len(shape), pipeline_mode=pl.Buffered(1))


def _out_proj(merged, x2, w_out, n_post, n_ffn_pre):
    t, d = x2.shape
    tm = TM_OUT
    return pl.pallas_call(
        _out_proj_kernel,
        grid=(t // tm,),
        in_specs=[
            pl.BlockSpec((tm, d), lambda i: (i, 0)),
            pl.BlockSpec((tm, d), lambda i: (i, 0)),
            _const_spec((d, d)),
            _const_spec((1, d)),
            _const_spec((1, d)),
        ],
        out_specs=[
            pl.BlockSpec((tm, d), lambda i: (i, 0)),
            pl.BlockSpec((tm, d), lambda i: (i, 0)),
        ],
        out_shape=[
            jax.ShapeDtypeStruct((t, d), F32),
            jax.ShapeDtypeStruct((t, d), BF16),
        ],
        compiler_params=_params(("arbitrary",)),
        name="out_proj",
    )(merged, x2, w_out, n_post, n_ffn_pre)


def _ffn_kernel(u_ref, h_hbm_ref, wg_hbm_ref, wu_hbm_ref, wd_hbm_ref, npost_ref, o_ref,
                h_buf, wg_buf, wu_buf, wd_buf, h_sem, w_sem, *, n_ff):
    i = pl.program_id(0)
    tm = o_ref.shape[0]
    tf = wg_buf.shape[2]

    def slab_copies(j, slot):
        cols = pl.ds(j * tf, tf)
        return (pltpu.make_async_copy(wg_hbm_ref.at[:, cols], wg_buf.at[slot], w_sem.at[0, slot]),
                pltpu.make_async_copy(wu_hbm_ref.at[:, cols], wu_buf.at[slot], w_sem.at[1, slot]),
                pltpu.make_async_copy(wd_hbm_ref.at[cols, :], wd_buf.at[slot], w_sem.at[2, slot]))

    h_copy = pltpu.make_async_copy(h_hbm_ref.at[pl.ds(i * tm, tm), :], h_buf, h_sem)
    h_copy.start()

    @pl.when(i == 0)
    def _():
        for copy in slab_copies(0, 0):
            copy.start()

    o_ref[...] = jnp.zeros_like(o_ref)

    def ff_step(j, carry):
        slot = (i * n_ff + j) % 2
        for copy in slab_copies(j, slot):
            copy.wait()

        @pl.when(j + 1 < n_ff)
        def _():
            for copy in slab_copies(j + 1, 1 - slot):
                copy.start()

        @pl.when((j + 1 == n_ff) & (i + 1 < pl.num_programs(0)))
        def _():
            for copy in slab_copies(0, 1 - slot):
                copy.start()

        u = u_ref[...]
        g = jnp.dot(u, wg_buf[slot], preferred_element_type=F32)
        up = jnp.dot(u, wu_buf[slot], preferred_element_type=F32)
        hidden = ((g * _sigmoid(g)) * up).astype(BF16)
        for c0 in range(0, o_ref.shape[1], DOWN_CHUNK):
            cols = slice(c0, c0 + DOWN_CHUNK)
            o_ref[:, cols] += jnp.dot(hidden, wd_buf[slot, :, cols], preferred_element_type=F32)
        return carry

    lax.fori_loop(0, n_ff, ff_step, 0)

    h_copy.wait()
    for r0 in range(0, tm, NORM_ROWS):
        rows = slice(r0, r0 + NORM_ROWS)
        o_ref[rows, :] = h_buf[rows, :] + _rms_norm(o_ref[rows, :], npost_ref[...])


def _ffn(u2, h1, w_gate, w_up, w_down, n_post):
    t, d = h1.shape
    f = w_gate.shape[1]
    tm, tf = TM_FFN, TF_FFN
    return pl.pallas_call(
        functools.partial(_ffn_kernel, n_ff=f // tf),
        grid=(t // tm,),
        in_specs=[
            pl.BlockSpec((tm, d), lambda i: (i, 0)),
            pl.BlockSpec(memory_space=pl.ANY),
            pl.BlockSpec(memory_space=pl.ANY),
            pl.BlockSpec(memory_space=pl.ANY),
            pl.BlockSpec(memory_space=pl.ANY),
            pl.BlockSpec((1, d), lambda i: (0, 0)),
        ],
        out_specs=pl.BlockSpec((tm, d), lambda i: (i, 0)),
        out_shape=jax.ShapeDtypeStruct((t, d), F32),
        scratch_shapes=[
            pltpu.VMEM((tm, d), F32),
            pltpu.VMEM((2, d, tf), BF16),
            pltpu.VMEM((2, d, tf), BF16),
            pltpu.VMEM((2, tf, d), BF16),
            pltpu.SemaphoreType.DMA(()),
            pltpu.SemaphoreType.DMA((3, 2)),
        ],
        compiler_params=_params(("arbitrary",)),
        name="swiglu_ffn",
    )(u2, h1, w_gate, w_up, w_down, n_post)


def kernel(x, norm_mix_pre, norm_mix_post, w_in, b_forget, w_branch_sb, w_branch_fox, w_out,
           norm_ffn_pre, norm_ffn_post, w_ffn_gate, w_ffn_up, w_ffn_down):
    batch, seq, d = x.shape
    depth = w_in.shape[0]
    n_qkv = 6 * D_BRANCH
    assert n_qkv % LANES == 0 and seq % TM_PROJ == 0
    h = x.reshape(batch * seq, d)
    for l in range(depth):
        w_all = w_in[l].astype(BF16)
        w_gates = w_all[:, n_qkv + N_HEADS:]
        b_f = jnp.pad(b_forget[l], (0, LANES - N_HEADS)).reshape(1, LANES)
        g_pre = norm_mix_pre[l].reshape(1, d)

        q_scale = jnp.full((D_BRANCH,), ATTN_SCALE * LOG2E, F32)
        ones = jnp.ones((D_BRANCH,), F32)
        col_scale = jnp.concatenate([-q_scale, ones, ones, q_scale, ones, ones]).reshape(1, n_qkv)
        qkv, u1, wg_bf, wu_bf, wd_bf = _qkv_proj(
            h, g_pre, w_all, col_scale, [w_ffn_gate[l], w_ffn_up[l], w_ffn_down[l]])
        gates, cum_col, cum_row, wo_bf, wsb_bf, wfx_bf = _gates_proj(
            u1, w_gates, w_all, n_qkv // LANES, b_f, seq,
            [w_out[l], w_branch_sb[l], w_branch_fox[l]])
        o_sb, o_fx = _attention(qkv.reshape(batch, seq, n_qkv),
                                cum_col.reshape(batch, seq, LANES), cum_row, batch, seq)
        merged = _merge(o_sb.reshape(batch * seq, D_BRANCH), o_fx.reshape(batch * seq, D_BRANCH),
                        gates, wsb_bf, wfx_bf)
        h1, u2 = _out_proj(merged, h, wo_bf,
                           norm_mix_post[l].reshape(1, d), norm_ffn_pre[l].reshape(1, d))
        h = _ffn(u2, h1, wg_bf, wu_bf, wd_bf, norm_ffn_post[l].reshape(1, d))
    return h.reshape(batch, seq, d)
```

```python
import functools

import jax
import jax.numpy as jnp
from jax import lax
from jax.experimental import pallas as pl
from jax.experimental.pallas import tpu as pltpu

F32 = jnp.float32
BF16 = jnp.bfloat16

D_MODEL = 2048
HEAD_DIM = 128
N_HEADS = 8
D_BRANCH = N_HEADS * HEAD_DIM
D_FF = 5632
RMS_EPS = 1e-6
ATTN_SCALE = HEAD_DIM ** -0.5
NEG_BIG = -1e30
LOG2E = 1.4426950408889634
MAX_NEG_LOGIT = 126.0

LANES = 128
BF16_SUBLANES = 16
VMEM_LIMIT = 56 * 1024 * 1024

TM_PROJ = 1024
TN_PROJ = 1024
CUM_CHUNK = 256
TK = 256
RC = 256
QC = 256
SB_LOOKAHEAD_QK = 3
SB_LOOKAHEAD_CUM = 2
FOX_LOOKAHEAD = 3
TM_MERGE = 1024
TN_MERGE = 1024
TM_OUT = 512
OUT_ROWS = 256
TM_FFN = 1024
TF_FFN = 512
DOWN_CHUNK = 512
NORM_ROWS = 256


def _params(semantics):
    return pltpu.CompilerParams(dimension_semantics=semantics, vmem_limit_bytes=VMEM_LIMIT)


def _rms_norm(x, g):
    ms = jnp.mean(x * x, axis=-1, keepdims=True)
    return (x * lax.rsqrt(ms + RMS_EPS)) * g


def _log_sigmoid(x):
    return jnp.minimum(x, 0.0) - jnp.log1p(jnp.exp(-jnp.abs(x)))


def _sigmoid(x):
    return 1.0 / (1.0 + jnp.exp(-x))


def _dot_nt(a, b):
    return lax.dot_general(a, b, (((1,), (1,)), ((), ())), preferred_element_type=F32)


def _split3(x):
    hi = x.astype(BF16)
    r1 = x - hi.astype(F32)
    mid = r1.astype(BF16)
    lo = (r1 - mid.astype(F32)).astype(BF16)
    return hi, mid, lo


def _rider_plan(weights, steps_each):
    plan, s0 = [], 0
    for w, c in zip(weights, steps_each):
        assert w.shape[0] % c == 0 and (w.shape[0] // c) % BF16_SUBLANES == 0
        plan.append((w, s0, c))
        s0 += c
    return plan


def _rider_specs(plan, n_j):
    specs, shapes = [], []
    for w, s0, c in plan:
        def index(i, j, s0=s0, c=c):
            return (jnp.clip(i * n_j + j - s0, 0, c - 1), 0)
        specs.append(pl.BlockSpec((w.shape[0] // c, w.shape[1]), index))
        shapes.append(jax.ShapeDtypeStruct(w.shape, BF16))
    return specs, shapes


def _run_riders(step, windows, src_refs, dst_refs):
    for (s0, c), src, dst in zip(windows, src_refs, dst_refs):
        @pl.when((step >= s0) & (step < s0 + c))
        def _(src=src, dst=dst):
            dst[...] = src[...].astype(BF16)


def _qkv_kernel(x_ref, g_ref, w_ref, cs_ref, *refs, windows):
    k = len(windows)
    src_refs, (o_ref, u_ref), dst_refs = refs[:k], refs[k:k + 2], refs[k + 2:]
    _run_riders(pl.program_id(0) * pl.num_programs(1) + pl.program_id(1), windows,
                src_refs, dst_refs)

    @pl.when(pl.program_id(1) == 0)
    def _():
        u_ref[...] = _rms_norm(x_ref[...], g_ref[...]).astype(BF16)

    acc = _dot_nt(u_ref[...], w_ref[...])
    o_ref[...] = (acc * cs_ref[...]).astype(o_ref.dtype)


def _qkv_proj(x2, g, w_t, col_scale, ride):
    t, d = x2.shape
    n = col_scale.shape[1]
    n_i, n_j = t // TM_PROJ, n // TN_PROJ
    plan = _rider_plan(ride, [n_i * n_j // len(ride)] * len(ride))
    ride_specs, ride_shapes = _rider_specs(plan, n_j)
    return pl.pallas_call(
        functools.partial(_qkv_kernel, windows=[(s0, c) for _, s0, c in plan]),
        grid=(n_i, n_j),
        in_specs=[
            pl.BlockSpec((TM_PROJ, d), lambda i, j: (i, 0)),
            pl.BlockSpec((1, d), lambda i, j: (0, 0)),
            pl.BlockSpec((TN_PROJ, d), lambda i, j: (j, 0)),
            pl.BlockSpec((1, TN_PROJ), lambda i, j: (0, j)),
        ] + ride_specs,
        out_specs=[
            pl.BlockSpec((TM_PROJ, TN_PROJ), lambda i, j: (i, j)),
            pl.BlockSpec((TM_PROJ, d), lambda i, j: (i, 0)),
        ] + ride_specs,
        out_shape=[
            jax.ShapeDtypeStruct((t, n), BF16),
            jax.ShapeDtypeStruct((t, d), BF16),
        ] + ride_shapes,
        compiler_params=_params(("arbitrary", "arbitrary")),
        name="qkv_proj",
    )(x2, g, w_t, col_scale, *ride)


def _gates_kernel(u_ref, wt_hbm_ref, wf_ref, bf_ref, *refs, tiles_per_seq, windows, row0):
    k = len(windows)
    src_refs, (o_ref, cumc_ref, cumr_ref), dst_refs = refs[:k], refs[k:k + 3], refs[k + 3:-3]
    carry_ref, w_buf, w_sem = refs[-3:]
    i, j = pl.program_id(0), pl.program_id(1)
    n_j = pl.num_programs(1)
    step = i * n_j + j
    _run_riders(step, windows, src_refs, dst_refs)

    def slab_copy(jj, slot):
        rows = pl.ds(row0 + jj * TN_PROJ, TN_PROJ)
        return pltpu.make_async_copy(wt_hbm_ref.at[rows, :], w_buf.at[slot], w_sem.at[slot])

    slot = step % 2

    @pl.when(step == 0)
    def _():
        slab_copy(0, 0).start()

    @pl.when(step + 1 < pl.num_programs(0) * n_j)
    def _():
        slab_copy((j + 1) % n_j, 1 - slot).start()

    @pl.when(pl.program_id(1) == 0)
    def _():
        f = _dot_nt(u_ref[...], wf_ref[...].astype(BF16))
        log_f = _log_sigmoid(f + bf_ref[...])

        @pl.when(i % tiles_per_seq == 0)
        def _():
            carry_ref[...] = jnp.zeros_like(carry_ref)

        r = lax.broadcasted_iota(jnp.int32, (CUM_CHUNK, CUM_CHUNK), 0)
        c = lax.broadcasted_iota(jnp.int32, (CUM_CHUNK, CUM_CHUNK), 1)
        lower = (c <= r).astype(BF16)
        carry = carry_ref[...]
        for ch in range(TM_PROJ // CUM_CHUNK):
            blk = log_f[ch * CUM_CHUNK:(ch + 1) * CUM_CHUNK]
            hi, mid, lo = _split3(blk)
            cs = (jnp.dot(lower, hi, preferred_element_type=F32)
                  + jnp.dot(lower, mid, preferred_element_type=F32)
                  + jnp.dot(lower, lo, preferred_element_type=F32)) + carry
            cumc_ref[ch * CUM_CHUNK:(ch + 1) * CUM_CHUNK, :] = cs
            carry = cs[CUM_CHUNK - 1:CUM_CHUNK, :]
        carry_ref[...] = carry
        cumr_ref[0] = cumc_ref[...].T[:N_HEADS, :]

    slab_copy(j, slot).wait()
    acc = _dot_nt(u_ref[...], w_buf[slot].astype(BF16))
    o_ref[...] = _sigmoid(acc).astype(o_ref.dtype)


def _gates_proj(u, w_t, f_row0, n, bf, seq, ride):
    t, d = u.shape
    assert f_row0 % LANES == 0
    tiles_per_seq = seq // TM_PROJ
    n_i, n_j = t // TM_PROJ, n // TN_PROJ
    ride_rows = sum(r.shape[0] for r in ride)
    plan = _rider_plan(ride, [n_i * n_j * r.shape[0] // ride_rows for r in ride])
    ride_specs, ride_shapes = _rider_specs(plan, n_j)
    return pl.pallas_call(
        functools.partial(_gates_kernel, tiles_per_seq=tiles_per_seq,
                          windows=[(s0, c) for _, s0, c in plan], row0=f_row0 + N_HEADS),
        grid=(n_i, n_j),
        in_specs=[
            pl.BlockSpec((TM_PROJ, d), lambda i, j: (i, 0)),
            pl.BlockSpec(memory_space=pl.ANY),
            pl.BlockSpec((LANES, d), lambda i, j: (f_row0 // LANES, 0)),
            pl.BlockSpec((1, LANES), lambda i, j: (0, 0)),
        ] + ride_specs,
        out_specs=[
            pl.BlockSpec((TM_PROJ, TN_PROJ), lambda i, j: (i, j)),
            pl.BlockSpec((TM_PROJ, LANES), lambda i, j: (i, 0)),
            pl.BlockSpec((1, N_HEADS, TM_PROJ),
                         lambda i, j: (i // tiles_per_seq, 0, i % tiles_per_seq)),
        ] + ride_specs,
        out_shape=[
            jax.ShapeDtypeStruct((t, n), BF16),
            jax.ShapeDtypeStruct((t, LANES), F32),
            jax.ShapeDtypeStruct((t // seq, N_HEADS, seq), F32),
        ] + ride_shapes,
        scratch_shapes=[pltpu.VMEM((1, LANES), F32), pltpu.VMEM((2, TN_PROJ, d), F32),
                        pltpu.SemaphoreType.DMA((2,))],
        compiler_params=_params(("arbitrary", "arbitrary")),
        name="gates_proj",
    )(u, w_t, w_t, bf, *ride)


def _qk(q, k):
    return lax.dot_general(q, k, (((1,), (1,)), ((), ())), preferred_element_type=F32)


def _sb_kernel(q_ref, k_ref, v_ref, o_ref, *, seq):
    r = lax.broadcasted_iota(jnp.int32, (TK, TK), 0)
    c = lax.broadcasted_iota(jnp.int32, (TK, TK), 1)
    later = (r > c).astype(BF16)

    def valid_mask(d, c):
        k0, r0 = d * TK, c * RC
        if k0 + TK <= r0:
            return None
        return (k0 + lax.broadcasted_iota(jnp.int32, (RC, TK), 1)
                < r0 + lax.broadcasted_iota(jnp.int32, (RC, TK), 0))

    def gates(d, c):
        q = q_ref[0, c * RC:(c + 1) * RC, :]
        k = k_ref[0, d * TK:(d + 1) * TK, :]
        nz = jnp.minimum(_qk(q, k), MAX_NEG_LOGIT)
        neg_log_beta = jnp.log2(1.0 + jnp.exp2(nz))
        log_keep = nz - neg_log_beta
        valid = valid_mask(d, c)
        if valid is not None:
            log_keep = jnp.where(valid, log_keep, 0.0)
        return neg_log_beta, log_keep.astype(BF16), jnp.sum(log_keep, axis=1, keepdims=True)

    def within_block(neg_log_beta, keep_bf, row_sum):
        return jnp.dot(keep_bf, later, preferred_element_type=F32) - neg_log_beta, row_sum

    def accumulate(d, c, log_w, row_sum, carry, acc):
        v = v_ref[0, d * TK:(d + 1) * TK, :]
        w = jnp.exp2(log_w + carry)
        valid = valid_mask(d, c)
        if valid is not None:
            w = jnp.where(valid, w, 0.0)
        return carry + row_sum, acc + jnp.dot(w.astype(BF16), v, preferred_element_type=F32)

    n_chunks = seq // RC
    state = [(jnp.zeros((RC, 1), F32), jnp.zeros((RC, HEAD_DIM), F32)) for _ in range(n_chunks)]
    items = [(d, c) for d in reversed(range(seq // TK)) for c in range(d * TK // RC, n_chunks)]
    n = len(items)
    staged_a, staged_b = {}, {}
    for step in range(-SB_LOOKAHEAD_QK, n):
        i_a, i_b = step + SB_LOOKAHEAD_QK, step + SB_LOOKAHEAD_CUM
        if i_a < n:
            staged_a[i_a] = gates(*items[i_a])
        if 0 <= i_b < n:
            staged_b[i_b] = within_block(*staged_a.pop(i_b))
        if step >= 0:
            d, c = items[step]
            state[c] = accumulate(d, c, *staged_b.pop(step), *state[c])
    for c in range(n_chunks):
        o_ref[0, c * RC:(c + 1) * RC, :] = state[c][1].astype(o_ref.dtype)


def _fox_kernel(q_ref, k_ref, v_ref, cq_ref, ck_ref, o_ref, *, seq):
    h = pl.program_id(1)
    lane = lax.broadcasted_iota(jnp.int32, (TK, LANES), 1)
    ones_col = (lane == 0).astype(BF16)
    n_blocks = seq // TK
    ck_blocks = [jnp.sum(jnp.where(lane == h, ck_ref[0, d * TK:(d + 1) * TK, :], 0.0),
                         axis=1, keepdims=True) * LOG2E for d in range(n_blocks)]
    v_blocks = [jnp.concatenate([v_ref[0, d * TK:(d + 1) * TK, :], ones_col], axis=1)
                for d in range(n_blocks)]

    def scores(d, c):
        k0, c0 = d * TK, c * QC
        q = q_ref[0, c0:c0 + QC, :]
        k = k_ref[0, k0:k0 + TK, :]
        y = _qk(k, q) - ck_blocks[d]
        if k0 + TK > c0:
            valid = (k0 + lax.broadcasted_iota(jnp.int32, (TK, QC), 0)
                     <= c0 + lax.broadcasted_iota(jnp.int32, (TK, QC), 1))
            y = jnp.where(valid, y, NEG_BIG)
        return y

    def accumulate(d, c, y, m, l, acc):
        c0 = c * QC
        cq = cq_ref[0, pl.ds(h, 1), c0:c0 + QC] * LOG2E
        m_new = jnp.maximum(m, cq + jnp.max(y, axis=0, keepdims=True))
        alpha = jnp.exp2(m - m_new)
        p = jnp.exp2(y - (m_new - cq))
        pv = lax.dot_general(v_blocks[d], p.astype(BF16), (((0,), (0,)), ((), ())),
                             preferred_element_type=F32)
        l = alpha * l + pv[HEAD_DIM:HEAD_DIM + 1]
        return m_new, l, alpha * acc + pv[:HEAD_DIM]

    n_chunks = seq // QC
    state = [(jnp.full((1, QC), NEG_BIG, F32), jnp.zeros((1, QC), F32),
              jnp.zeros((HEAD_DIM, QC), F32)) for _ in range(n_chunks)]
    items = [(d, c) for d in range(seq // TK) for c in range(d * TK // QC, n_chunks)]
    ys = [scores(*item) for item in items[:FOX_LOOKAHEAD]]
    for i, (d, c) in enumerate(items):
        if i + FOX_LOOKAHEAD < len(items):
            ys.append(scores(*items[i + FOX_LOOKAHEAD]))
        state[c] = accumulate(d, c, ys.pop(0), *state[c])
    for c in range(n_chunks):
        _, l, acc = state[c]
        o_ref[0, c * QC:(c + 1) * QC, :] = (acc / l).T.astype(o_ref.dtype)


def _attention(qkv, cum_col, cum_row, batch, seq):
    grid = (batch, N_HEADS)

    def head_spec(head0):
        return pl.BlockSpec((1, seq, HEAD_DIM), lambda b, h: (b, 0, head0 + h))

    out_shape = jax.ShapeDtypeStruct((batch, seq, D_BRANCH), BF16)
    sem = ("arbitrary", "arbitrary")

    o_sb = pl.pallas_call(
        functools.partial(_sb_kernel, seq=seq),
        grid=grid,
        in_specs=[head_spec(0), head_spec(N_HEADS), head_spec(2 * N_HEADS)],
        out_specs=head_spec(0),
        out_shape=out_shape,
        compiler_params=_params(sem),
        name="stickbreak_attn",
    )(qkv, qkv, qkv)

    o_fx = pl.pallas_call(
        functools.partial(_fox_kernel, seq=seq),
        grid=grid,
        in_specs=[
            head_spec(3 * N_HEADS), head_spec(4 * N_HEADS), head_spec(5 * N_HEADS),
            pl.BlockSpec((1, N_HEADS, seq), lambda b, h: (b, 0, 0)),
            pl.BlockSpec((1, seq, LANES), lambda b, h: (b, 0, 0)),
        ],
        out_specs=head_spec(0),
        out_shape=out_shape,
        compiler_params=_params(sem),
        name="forgetting_attn",
    )(qkv, qkv, qkv, cum_row, cum_col)
    return o_sb, o_fx


def _merge_kernel(osb_ref, ofx_ref, gsb_ref, gfx_ref, wsb_ref, wfx_ref, o_ref):
    a = jnp.dot(osb_ref[...], wsb_ref[...], preferred_element_type=F32)
    b = jnp.dot(ofx_ref[...], wfx_ref[...], preferred_element_type=F32)
    o_ref[...] = (gsb_ref[...].astype(F32) * a + gfx_ref[...].astype(F32) * b).astype(o_ref.dtype)


def _merge(o_sb, o_fx, gates, w_sb, w_fx):
    t, k = o_sb.shape
    d = w_sb.shape[1]
    tm, tn = TM_MERGE, TN_MERGE
    n_j = d // tn
    return pl.pallas_call(
        _merge_kernel,
        grid=(t // tm, n_j),
        in_specs=[
            pl.BlockSpec((tm, k), lambda i, j: (i, 0)),
            pl.BlockSpec((tm, k), lambda i, j: (i, 0)),
            pl.BlockSpec((tm, tn), lambda i, j: (i, j)),
            pl.BlockSpec((tm, tn), lambda i, j: (i, n_j + j)),
            pl.BlockSpec((k, tn), lambda i, j: (0, j)),
            pl.BlockSpec((k, tn), lambda i, j: (0, j)),
        ],
        out_specs=pl.BlockSpec((tm, tn), lambda i, j: (i, j)),
        out_shape=jax.ShapeDtypeStruct((t, d), BF16),
        compiler_params=_params(("arbitrary", "arbitrary")),
        name="gated_merge",
    )(o_sb, o_fx, gates, gates, w_sb, w_fx)


def _out_proj_kernel(m_ref, x_ref, wout_ref, npost_ref, npre_ref, h_ref, u_ref):
    tm = m_ref.shape[0]
    chunks = [slice(r0, r0 + OUT_ROWS) for r0 in range(0, tm, OUT_ROWS)]
    mixes = [jnp.dot(m_ref[rows, :], wout_ref[...], preferred_element_type=F32)
             for rows in chunks]
    for rows, mix in zip(chunks, mixes):
        h = x_ref[rows, :] + _rms_norm(mix, npost_ref[...])
        h_ref[rows, :] = h
        u_ref[rows, :] = _rms_norm(h, npre_ref[...]).astype(BF16)


def _const_spec(shape):
    return pl.BlockSpec(shape, lambda i: (0,) * len(shape), pipeline_mode=pl.Buffered(1))


def _out_proj(merged, x2, w_out, n_post, n_ffn_pre):
    t, d = x2.shape
    tm = TM_OUT
    return pl.pallas_call(
        _out_proj_kernel,
        grid=(t // tm,),
        in_specs=[
            pl.BlockSpec((tm, d), lambda i: (i, 0)),
            pl.BlockSpec((tm, d), lambda i: (i, 0)),
            _const_spec((d, d)),
            _const_spec((1, d)),
            _const_spec((1, d)),
        ],
        out_specs=[
            pl.BlockSpec((tm, d), lambda i: (i, 0)),
            pl.BlockSpec((tm, d), lambda i: (i, 0)),
        ],
        out_shape=[
            jax.ShapeDtypeStruct((t, d), F32),
            jax.ShapeDtypeStruct((t, d), BF16),
        ],
        compiler_params=_params(("arbitrary",)),
        name="out_proj",
    )(merged, x2, w_out, n_post, n_ffn_pre)


def _ffn_kernel(u_ref, h_hbm_ref, wg_ref, wu_ref, wd_ref, npost_ref, o_ref, h_buf, h_sem):
    i, j = pl.program_id(0), pl.program_id(1)
    tm = o_ref.shape[0]
    h_copy = pltpu.make_async_copy(h_hbm_ref.at[pl.ds(i * tm, tm), :], h_buf, h_sem)

    @pl.when(j == 0)
    def _():
        h_copy.start()
        o_ref[...] = jnp.zeros_like(o_ref)

    u = u_ref[...]
    g = jnp.dot(u, wg_ref[...], preferred_element_type=F32)
    up = jnp.dot(u, wu_ref[...], preferred_element_type=F32)
    hidden = ((g * _sigmoid(g)) * up).astype(BF16)
    for c0 in range(0, o_ref.shape[1], DOWN_CHUNK):
        cols = slice(c0, c0 + DOWN_CHUNK)
        o_ref[:, cols] += jnp.dot(hidden, wd_ref[:, cols], preferred_element_type=F32)

    @pl.when(j == pl.num_programs(1) - 1)
    def _():
        h_copy.wait()
        for r0 in range(0, tm, NORM_ROWS):
            rows = slice(r0, r0 + NORM_ROWS)
            o_ref[rows, :] = h_buf[rows, :] + _rms_norm(o_ref[rows, :], npost_ref[...])


def _ffn(u2, h1, w_gate, w_up, w_down, n_post):
    t, d = h1.shape
    f = w_gate.shape[1]
    tm, tf = TM_FFN, TF_FFN
    return pl.pallas_call(
        _ffn_kernel,
        grid=(t // tm, f // tf),
        in_specs=[
            pl.BlockSpec((tm, d), lambda i, j: (i, 0)),
            pl.BlockSpec(memory_space=pl.ANY),
            pl.BlockSpec((d, tf), lambda i, j: (0, j)),
            pl.BlockSpec((d, tf), lambda i, j: (0, j)),
            pl.BlockSpec((tf, d), lambda i, j: (j, 0)),
            pl.BlockSpec((1, d), lambda i, j: (0, 0)),
        ],
        out_specs=pl.BlockSpec((tm, d), lambda i, j: (i, 0)),
        out_shape=jax.ShapeDtypeStruct((t, d), F32),
        scratch_shapes=[pltpu.VMEM((tm, d), F32), pltpu.SemaphoreType.DMA(())],
        compiler_params=_params(("arbitrary", "arbitrary")),
        name="swiglu_ffn",
    )(u2, h1, w_gate, w_up, w_down, n_post)


def kernel(x, norm_mix_pre, norm_mix_post, w_in, b_forget, w_branch_sb, w_branch_fox, w_out,
           norm_ffn_pre, norm_ffn_post, w_ffn_gate, w_ffn_up, w_ffn_down):
    batch, seq, d = x.shape
    depth = w_in.shape[0]
    n_qkv = 6 * D_BRANCH
    assert n_qkv % LANES == 0 and seq % TM_PROJ == 0
    h = x.reshape(batch * seq, d)
    for l in range(depth):
        w_t = jnp.swapaxes(w_in[l], 0, 1)
        wq_t = w_t[:n_qkv].astype(BF16)
        b_f = jnp.pad(b_forget[l], (0, LANES - N_HEADS)).reshape(1, LANES)
        g_pre = norm_mix_pre[l].reshape(1, d)

        q_scale = jnp.full((D_BRANCH,), ATTN_SCALE * LOG2E, F32)
        ones = jnp.ones((D_BRANCH,), F32)
        col_scale = jnp.concatenate([-q_scale, ones, ones, q_scale, ones, ones]).reshape(1, n_qkv)
        qkv, u1, wg_bf, wu_bf, wd_bf = _qkv_proj(
            h, g_pre, wq_t, col_scale, [w_ffn_gate[l], w_ffn_up[l], w_ffn_down[l]])
        gates, cum_col, cum_row, wo_bf, wsb_bf, wfx_bf = _gates_proj(
            u1, w_t, n_qkv, 2 * d, b_f, seq, [w_out[l], w_branch_sb[l], w_branch_fox[l]])
        o_sb, o_fx = _attention(qkv.reshape(batch, seq, n_qkv),
                                cum_col.reshape(batch, seq, LANES), cum_row, batch, seq)
        merged = _merge(o_sb.reshape(batch * seq, D_BRANCH), o_fx.reshape(batch * seq, D_BRANCH),
                        gates, wsb_bf, wfx_bf)
        h1, u2 = _out_proj(merged, h, wo_bf,
                           norm_mix_post[l].reshape(1, d), norm_ffn_pre[l].reshape(1, d))
        h = _ffn(u2, h1, wg_bf, wu_bf, wd_bf, norm_ffn_post[l].reshape(1, d))
    return h.reshape(batch, seq, d)
```

```python
import functools

import jax
import jax.numpy as jnp
from jax import lax
from jax.experimental import pallas as pl
from jax.experimental.pallas import tpu as pltpu

F32 = jnp.float32
BF16 = jnp.bfloat16

D_MODEL = 2048
HEAD_DIM = 128
N_HEADS = 8
D_BRANCH = N_HEADS * HEAD_DIM
D_FF = 5632
RMS_EPS = 1e-6
ATTN_SCALE = HEAD_DIM ** -0.5
NEG_BIG = -1e30
LOG2E = 1.4426950408889634
MAX_NEG_LOGIT = 126.0

LANES = 128
BF16_SUBLANES = 16
VMEM_LIMIT = 56 * 1024 * 1024

TM_PROJ = 1024
TN_PROJ = 1024
CUM_CHUNK = 256
TK = 256
RC = 256
QC = 256
SB_LOOKAHEAD_QK = 3
SB_LOOKAHEAD_CUM = 2
FOX_LOOKAHEAD = 3
TM_MERGE = 1024
TN_MERGE = 1024
TM_OUT = 512
OUT_ROWS = 256
TM_FFN = 1024
TF_FFN = 512
DOWN_CHUNK = 512
NORM_ROWS = 256


def _params(semantics):
    return pltpu.CompilerParams(dimension_semantics=semantics, vmem_limit_bytes=VMEM_LIMIT)


def _rms_norm(x, g):
    ms = jnp.mean(x * x, axis=-1, keepdims=True)
    return (x * lax.rsqrt(ms + RMS_EPS)) * g


def _log_sigmoid(x):
    return jnp.minimum(x, 0.0) - jnp.log1p(jnp.exp(-jnp.abs(x)))


def _sigmoid(x):
    return 1.0 / (1.0 + jnp.exp(-x))


def _dot_nt(a, b):
    return lax.dot_general(a, b, (((1,), (1,)), ((), ())), preferred_element_type=F32)


def _split3(x):
    hi = x.astype(BF16)
    r1 = x - hi.astype(F32)
    mid = r1.astype(BF16)
    lo = (r1 - mid.astype(F32)).astype(BF16)
    return hi, mid, lo


def _rider_plan(weights, steps_each):
    plan, s0 = [], 0
    for w, c in zip(weights, steps_each):
        assert w.shape[0] % c == 0 and (w.shape[0] // c) % BF16_SUBLANES == 0
        plan.append((w, s0, c))
        s0 += c
    return plan


def _rider_specs(plan, n_j):
    specs, shapes = [], []
    for w, s0, c in plan:
        def index(i, j, s0=s0, c=c):
            return (jnp.clip(i * n_j + j - s0, 0, c - 1), 0)
        specs.append(pl.BlockSpec((w.shape[0] // c, w.shape[1]), index))
        shapes.append(jax.ShapeDtypeStruct(w.shape, BF16))
    return specs, shapes


def _run_riders(step, windows, src_refs, dst_refs):
    for (s0, c), src, dst in zip(windows, src_refs, dst_refs):
        @pl.when((step >= s0) & (step < s0 + c))
        def _(src=src, dst=dst):
            dst[...] = src[...].astype(BF16)


def _qkv_kernel(x_ref, g_ref, w_ref, cs_ref, *refs, windows):
    k = len(windows)
    src_refs, (o_ref, u_ref), dst_refs = refs[:k], refs[k:k + 2], refs[k + 2:]
    _run_riders(pl.program_id(0) * pl.num_programs(1) + pl.program_id(1), windows,
                src_refs, dst_refs)

    @pl.when(pl.program_id(1) == 0)
    def _():
        u_ref[...] = _rms_norm(x_ref[...], g_ref[...]).astype(BF16)

    acc = jnp.dot(u_ref[...], w_ref[...], preferred_element_type=F32)
    o_ref[...] = (acc * cs_ref[...]).astype(o_ref.dtype)


def _qkv_proj(x2, g, w, col_scale, ride):
    t, d = x2.shape
    n = col_scale.shape[1]
    n_i, n_j = t // TM_PROJ, n // TN_PROJ
    plan = _rider_plan(ride, [n_i * n_j // len(ride)] * len(ride))
    ride_specs, ride_shapes = _rider_specs(plan, n_j)
    return pl.pallas_call(
        functools.partial(_qkv_kernel, windows=[(s0, c) for _, s0, c in plan]),
        grid=(n_i, n_j),
        in_specs=[
            pl.BlockSpec((TM_PROJ, d), lambda i, j: (i, 0)),
            pl.BlockSpec((1, d), lambda i, j: (0, 0)),
            pl.BlockSpec((d, TN_PROJ), lambda i, j: (0, j)),
            pl.BlockSpec((1, TN_PROJ), lambda i, j: (0, j)),
        ] + ride_specs,
        out_specs=[
            pl.BlockSpec((TM_PROJ, TN_PROJ), lambda i, j: (i, j)),
            pl.BlockSpec((TM_PROJ, d), lambda i, j: (i, 0)),
        ] + ride_specs,
        out_shape=[
            jax.ShapeDtypeStruct((t, n), BF16),
            jax.ShapeDtypeStruct((t, d), BF16),
        ] + ride_shapes,
        compiler_params=_params(("arbitrary", "arbitrary")),
        name="qkv_proj",
    )(x2, g, w, col_scale, *ride)


def _gates_kernel(u_ref, wt_hbm_ref, wf_ref, bf_ref, *refs, tiles_per_seq, windows, row0):
    k = len(windows)
    src_refs, (o_ref, cumc_ref, cumr_ref), dst_refs = refs[:k], refs[k:k + 3], refs[k + 3:-3]
    carry_ref, w_buf, w_sem = refs[-3:]
    i, j = pl.program_id(0), pl.program_id(1)
    n_j = pl.num_programs(1)
    step = i * n_j + j
    _run_riders(step, windows, src_refs, dst_refs)

    def slab_copy(jj, slot):
        rows = pl.ds(row0 + jj * TN_PROJ, TN_PROJ)
        return pltpu.make_async_copy(wt_hbm_ref.at[rows, :], w_buf.at[slot], w_sem.at[slot])

    slot = step % 2

    @pl.when(step == 0)
    def _():
        slab_copy(0, 0).start()

    @pl.when(step + 1 < pl.num_programs(0) * n_j)
    def _():
        slab_copy((j + 1) % n_j, 1 - slot).start()

    @pl.when(pl.program_id(1) == 0)
    def _():
        f = _dot_nt(u_ref[...], wf_ref[...].astype(BF16))
        log_f = _log_sigmoid(f + bf_ref[...])

        @pl.when(i % tiles_per_seq == 0)
        def _():
            carry_ref[...] = jnp.zeros_like(carry_ref)

        r = lax.broadcasted_iota(jnp.int32, (CUM_CHUNK, CUM_CHUNK), 0)
        c = lax.broadcasted_iota(jnp.int32, (CUM_CHUNK, CUM_CHUNK), 1)
        lower = (c <= r).astype(BF16)
        carry = carry_ref[...]
        for ch in range(TM_PROJ // CUM_CHUNK):
            blk = log_f[ch * CUM_CHUNK:(ch + 1) * CUM_CHUNK]
            hi, mid, lo = _split3(blk)
            cs = (jnp.dot(lower, hi, preferred_element_type=F32)
                  + jnp.dot(lower, mid, preferred_element_type=F32)
                  + jnp.dot(lower, lo, preferred_element_type=F32)) + carry
            cumc_ref[ch * CUM_CHUNK:(ch + 1) * CUM_CHUNK, :] = cs
            carry = cs[CUM_CHUNK - 1:CUM_CHUNK, :]
        carry_ref[...] = carry
        cumr_ref[0] = cumc_ref[...].T[:N_HEADS, :]

    slab_copy(j, slot).wait()
    acc = _dot_nt(u_ref[...], w_buf[slot].astype(BF16))
    o_ref[...] = _sigmoid(acc).astype(o_ref.dtype)


def _gates_proj(u, w_t, f_row0, n, bf, seq, ride):
    t, d = u.shape
    assert f_row0 % LANES == 0
    tiles_per_seq = seq // TM_PROJ
    n_i, n_j = t // TM_PROJ, n // TN_PROJ
    ride_rows = sum(r.shape[0] for r in ride)
    plan = _rider_plan(ride, [n_i * n_j * r.shape[0] // ride_rows for r in ride])
    ride_specs, ride_shapes = _rider_specs(plan, n_j)
    return pl.pallas_call(
        functools.partial(_gates_kernel, tiles_per_seq=tiles_per_seq,
                          windows=[(s0, c) for _, s0, c in plan], row0=f_row0 + N_HEADS),
        grid=(n_i, n_j),
        in_specs=[
            pl.BlockSpec((TM_PROJ, d), lambda i, j: (i, 0)),
            pl.BlockSpec(memory_space=pl.ANY),
            pl.BlockSpec((LANES, d), lambda i, j: (f_row0 // LANES, 0)),
            pl.BlockSpec((1, LANES), lambda i, j: (0, 0)),
        ] + ride_specs,
        out_specs=[
            pl.BlockSpec((TM_PROJ, TN_PROJ), lambda i, j: (i, j)),
            pl.BlockSpec((TM_PROJ, LANES), lambda i, j: (i, 0)),
            pl.BlockSpec((1, N_HEADS, TM_PROJ),
                         lambda i, j: (i // tiles_per_seq, 0, i % tiles_per_seq)),
        ] + ride_specs,
        out_shape=[
            jax.ShapeDtypeStruct((t, n), BF16),
            jax.ShapeDtypeStruct((t, LANES), F32),
            jax.ShapeDtypeStruct((t // seq, N_HEADS, seq), F32),
        ] + ride_shapes,
        scratch_shapes=[pltpu.VMEM((1, LANES), F32), pltpu.VMEM((2, TN_PROJ, d), F32),
                        pltpu.SemaphoreType.DMA((2,))],
        compiler_params=_params(("arbitrary", "arbitrary")),
        name="gates_proj",
    )(u, w_t, w_t, bf, *ride)


def _qk(q, k):
    return lax.dot_general(q, k, (((1,), (1,)), ((), ())), preferred_element_type=F32)


def _sb_kernel(q_ref, k_ref, v_ref, o_ref, *, seq):
    r = lax.broadcasted_iota(jnp.int32, (TK, TK), 0)
    c = lax.broadcasted_iota(jnp.int32, (TK, TK), 1)
    later = (r > c).astype(BF16)

    def valid_mask(d, c):
        k0, r0 = d * TK, c * RC
        if k0 + TK <= r0:
            return None
        return (k0 + lax.broadcasted_iota(jnp.int32, (RC, TK), 1)
                < r0 + lax.broadcasted_iota(jnp.int32, (RC, TK), 0))

    def gates(d, c):
        q = q_ref[0, c * RC:(c + 1) * RC, :]
        k = k_ref[0, d * TK:(d + 1) * TK, :]
        nz = jnp.minimum(_qk(q, k), MAX_NEG_LOGIT)
        neg_log_beta = jnp.log2(1.0 + jnp.exp2(nz))
        log_keep = nz - neg_log_beta
        valid = valid_mask(d, c)
        if valid is not None:
            log_keep = jnp.where(valid, log_keep, 0.0)
        return neg_log_beta, log_keep.astype(BF16), jnp.sum(log_keep, axis=1, keepdims=True)

    def within_block(neg_log_beta, keep_bf, row_sum):
        return jnp.dot(keep_bf, later, preferred_element_type=F32) - neg_log_beta, row_sum

    def accumulate(d, c, log_w, row_sum, carry, acc):
        v = v_ref[0, d * TK:(d + 1) * TK, :]
        w = jnp.exp2(log_w + carry)
        valid = valid_mask(d, c)
        if valid is not None:
            w = jnp.where(valid, w, 0.0)
        return carry + row_sum, acc + jnp.dot(w.astype(BF16), v, preferred_element_type=F32)

    n_chunks = seq // RC
    state = [(jnp.zeros((RC, 1), F32), jnp.zeros((RC, HEAD_DIM), F32)) for _ in range(n_chunks)]
    items = [(d, c) for d in reversed(range(seq // TK)) for c in range(d * TK // RC, n_chunks)]
    n = len(items)
    staged_a, staged_b = {}, {}
    for step in range(-SB_LOOKAHEAD_QK, n):
        i_a, i_b = step + SB_LOOKAHEAD_QK, step + SB_LOOKAHEAD_CUM
        if i_a < n:
            staged_a[i_a] = gates(*items[i_a])
        if 0 <= i_b < n:
            staged_b[i_b] = within_block(*staged_a.pop(i_b))
        if step >= 0:
            d, c = items[step]
            state[c] = accumulate(d, c, *staged_b.pop(step), *state[c])
    for c in range(n_chunks):
        o_ref[0, c * RC:(c + 1) * RC, :] = state[c][1].astype(o_ref.dtype)


def _fox_kernel(q_ref, k_ref, v_ref, cq_ref, ck_ref, o_ref, *, seq):
    h = pl.program_id(1)
    lane = lax.broadcasted_iota(jnp.int32, (TK, LANES), 1)
    ones_col = (lane == 0).astype(BF16)
    n_blocks = seq // TK
    ck_blocks = [jnp.sum(jnp.where(lane == h, ck_ref[0, d * TK:(d + 1) * TK, :], 0.0),
                         axis=1, keepdims=True) * LOG2E for d in range(n_blocks)]
    v_blocks = [jnp.concatenate([v_ref[0, d * TK:(d + 1) * TK, :], ones_col], axis=1)
                for d in range(n_blocks)]

    def scores(d, c):
        k0, c0 = d * TK, c * QC
        q = q_ref[0, c0:c0 + QC, :]
        k = k_ref[0, k0:k0 + TK, :]
        y = _qk(k, q) - ck_blocks[d]
        if k0 + TK > c0:
            valid = (k0 + lax.broadcasted_iota(jnp.int32, (TK, QC), 0)
                     <= c0 + lax.broadcasted_iota(jnp.int32, (TK, QC), 1))
            y = jnp.where(valid, y, NEG_BIG)
        return y

    def accumulate(d, c, y, m, l, acc):
        c0 = c * QC
        cq = cq_ref[0, pl.ds(h, 1), c0:c0 + QC] * LOG2E
        m_new = jnp.maximum(m, cq + jnp.max(y, axis=0, keepdims=True))
        alpha = jnp.exp2(m - m_new)
        p = jnp.exp2(y - (m_new - cq))
        pv = lax.dot_general(v_blocks[d], p.astype(BF16), (((0,), (0,)), ((), ())),
                             preferred_element_type=F32)
        l = alpha * l + pv[HEAD_DIM:HEAD_DIM + 1]
        return m_new, l, alpha * acc + pv[:HEAD_DIM]

    n_chunks = seq // QC
    state = [(jnp.full((1, QC), NEG_BIG, F32), jnp.zeros((1, QC), F32),
              jnp.zeros((HEAD_DIM, QC), F32)) for _ in range(n_chunks)]
    items = [(d, c) for d in range(seq // TK) for c in range(d * TK // QC, n_chunks)]
    ys = [scores(*item) for item in items[:FOX_LOOKAHEAD]]
    for i, (d, c) in enumerate(items):
        if i + FOX_LOOKAHEAD < len(items):
            ys.append(scores(*items[i + FOX_LOOKAHEAD]))
        state[c] = accumulate(d, c, ys.pop(0), *state[c])
    for c in range(n_chunks):
        _, l, acc = state[c]
        o_ref[0, c * QC:(c + 1) * QC, :] = (acc / l).T.astype(o_ref.dtype)


def _attention(qkv, cum_col, cum_row, batch, seq):
    grid = (batch, N_HEADS)

    def head_spec(head0):
        return pl.BlockSpec((1, seq, HEAD_DIM), lambda b, h: (b, 0, head0 + h))

    out_shape = jax.ShapeDtypeStruct((batch, seq, D_BRANCH), BF16)
    sem = ("arbitrary", "arbitrary")

    o_sb = pl.pallas_call(
        functools.partial(_sb_kernel, seq=seq),
        grid=grid,
        in_specs=[head_spec(0), head_spec(N_HEADS), head_spec(2 * N_HEADS)],
        out_specs=head_spec(0),
        out_shape=out_shape,
        compiler_params=_params(sem),
        name="stickbreak_attn",
    )(qkv, qkv, qkv)

    o_fx = pl.pallas_call(
        functools.partial(_fox_kernel, seq=seq),
        grid=grid,
        in_specs=[
            head_spec(3 * N_HEADS), head_spec(4 * N_HEADS), head_spec(5 * N_HEADS),
            pl.BlockSpec((1, N_HEADS, seq), lambda b, h: (b, 0, 0)),
            pl.BlockSpec((1, seq, LANES), lambda b, h: (b, 0, 0)),
        ],
        out_specs=head_spec(0),
        out_shape=out_shape,
        compiler_params=_params(sem),
        name="forgetting_attn",
    )(qkv, qkv, qkv, cum_row, cum_col)
    return o_sb, o_fx


def _merge_kernel(osb_ref, ofx_ref, gsb_ref, gfx_ref, wsb_ref, wfx_ref, o_ref):
    a = jnp.dot(osb_ref[...], wsb_ref[...], preferred_element_type=F32)
    b = jnp.dot(ofx_ref[...], wfx_ref[...], preferred_element_type=F32)
    o_ref[...] = (gsb_ref[...].astype(F32) * a + gfx_ref[...].astype(F32) * b).astype(o_ref.dtype)


def _merge(o_sb, o_fx, gates, w_sb, w_fx):
    t, k = o_sb.shape
    d = w_sb.shape[1]
    tm, tn = TM_MERGE, TN_MERGE
    n_j = d // tn
    return pl.pallas_call(
        _merge_kernel,
        grid=(t // tm, n_j),
        in_specs=[
            pl.BlockSpec((tm, k), lambda i, j: (i, 0)),
            pl.BlockSpec((tm, k), lambda i, j: (i, 0)),
            pl.BlockSpec((tm, tn), lambda i, j: (i, j)),
            pl.BlockSpec((tm, tn), lambda i, j: (i, n_j + j)),
            pl.BlockSpec((k, tn), lambda i, j: (0, j)),
            pl.BlockSpec((k, tn), lambda i, j: (0, j)),
        ],
        out_specs=pl.BlockSpec((tm, tn), lambda i, j: (i, j)),
        out_shape=jax.ShapeDtypeStruct((t, d), BF16),
        compiler_params=_params(("arbitrary", "arbitrary")),
        name="gated_merge",
    )(o_sb, o_fx, gates, gates, w_sb, w_fx)


def _out_proj_kernel(m_ref, x_ref, wout_ref, npost_ref, npre_ref, h_ref, u_ref):
    tm = m_ref.shape[0]
    chunks = [slice(r0, r0 + OUT_ROWS) for r0 in range(0, tm, OUT_ROWS)]
    mixes = [jnp.dot(m_ref[rows, :], wout_ref[...], preferred_element_type=F32)
             for rows in chunks]
    for rows, mix in zip(chunks, mixes):
        h = x_ref[rows, :] + _rms_norm(mix, npost_ref[...])
        h_ref[rows, :] = h
        u_ref[rows, :] = _rms_norm(h, npre_ref[...]).astype(BF16)


def _const_spec(shape):
    return pl.BlockSpec(shape, lambda i: (0,) * len(shape), pipeline_mode=pl.Buffered(1))


def _out_proj(merged, x2, w_out, n_post, n_ffn_pre):
    t, d = x2.shape
    tm = TM_OUT
    return pl.pallas_call(
        _out_proj_kernel,
        grid=(t // tm,),
        in_specs=[
            pl.BlockSpec((tm, d), lambda i: (i, 0)),
            pl.BlockSpec((tm, d), lambda i: (i, 0)),
            _const_spec((d, d)),
            _const_spec((1, d)),
            _const_spec((1, d)),
        ],
        out_specs=[
            pl.BlockSpec((tm, d), lambda i: (i, 0)),
            pl.BlockSpec((tm, d), lambda i: (i, 0)),
        ],
        out_shape=[
            jax.ShapeDtypeStruct((t, d), F32),
            jax.ShapeDtypeStruct((t, d), BF16),
        ],
        compiler_params=_params(("arbitrary",)),
        name="out_proj",
    )(merged, x2, w_out, n_post, n_ffn_pre)


def _ffn_kernel(u_ref, h_hbm_ref, wg_ref, wu_ref, wd_ref, npost_ref, o_ref, h_buf, h_sem):
    i, j = pl.program_id(0), pl.program_id(1)
    tm = o_ref.shape[0]
    h_copy = pltpu.make_async_copy(h_hbm_ref.at[pl.ds(i * tm, tm), :], h_buf, h_sem)

    @pl.when(j == 0)
    def _():
        h_copy.start()
        o_ref[...] = jnp.zeros_like(o_ref)

    u = u_ref[...]
    g = jnp.dot(u, wg_ref[...], preferred_element_type=F32)
    up = jnp.dot(u, wu_ref[...], preferred_element_type=F32)
    hidden = ((g * _sigmoid(g)) * up).astype(BF16)
    for c0 in range(0, o_ref.shape[1], DOWN_CHUNK):
        cols = slice(c0, c0 + DOWN_CHUNK)
        o_ref[:, cols] += jnp.dot(hidden, wd_ref[:, cols], preferred_element_type=F32)

    @pl.when(j == pl.num_programs(1) - 1)
    def _():
        h_copy.wait()
        for r0 in range(0, tm, NORM_ROWS):
            rows = slice(r0, r0 + NORM_ROWS)
            o_ref[rows, :] = h_buf[rows, :] + _rms_norm(o_ref[rows, :], npost_ref[...])


def _ffn(u2, h1, w_gate, w_up, w_down, n_post):
    t, d = h1.shape
    f = w_gate.shape[1]
    tm, tf = TM_FFN, TF_FFN
    return pl.pallas_call(
        _ffn_kernel,
        grid=(t // tm, f // tf),
        in_specs=[
            pl.BlockSpec((tm, d), lambda i, j: (i, 0)),
            pl.BlockSpec(memory_space=pl.ANY),
            pl.BlockSpec((d, tf), lambda i, j: (0, j)),
            pl.BlockSpec((d, tf), lambda i, j: (0, j)),
            pl.BlockSpec((tf, d), lambda i, j: (j, 0)),
            pl.BlockSpec((1, d), lambda i, j: (0, 0)),
        ],
        out_specs=pl.BlockSpec((tm, d), lambda i, j: (i, 0)),
        out_shape=jax.ShapeDtypeStruct((t, d), F32),
        scratch_shapes=[pltpu.VMEM((tm, d), F32), pltpu.SemaphoreType.DMA(())],
        compiler_params=_params(("arbitrary", "arbitrary")),
        name="swiglu_ffn",
    )(u2, h1, w_gate, w_up, w_down, n_post)


def kernel(x, norm_mix_pre, norm_mix_post, w_in, b_forget, w_branch_sb, w_branch_fox, w_out,
           norm_ffn_pre, norm_ffn_post, w_ffn_gate, w_ffn_up, w_ffn_down):
    batch, seq, d = x.shape
    depth = w_in.shape[0]
    n_qkv = 6 * D_BRANCH
    assert n_qkv % LANES == 0 and seq % TM_PROJ == 0
    h = x.reshape(batch * seq, d)
    for l in range(depth):
        w_q = w_in[l][:, :n_qkv].astype(BF16)
        w_t = jnp.swapaxes(w_in[l], 0, 1)
        b_f = jnp.pad(b_forget[l], (0, LANES - N_HEADS)).reshape(1, LANES)
        g_pre = norm_mix_pre[l].reshape(1, d)

        q_scale = jnp.full((D_BRANCH,), ATTN_SCALE * LOG2E, F32)
        ones = jnp.ones((D_BRANCH,), F32)
        col_scale = jnp.concatenate([-q_scale, ones, ones, q_scale, ones, ones]).reshape(1, n_qkv)
        qkv, u1, wg_bf, wu_bf, wd_bf = _qkv_proj(
            h, g_pre, w_q, col_scale, [w_ffn_gate[l], w_ffn_up[l], w_ffn_down[l]])
        gates, cum_col, cum_row, wo_bf, wsb_bf, wfx_bf = _gates_proj(
            u1, w_t, n_qkv, 2 * d, b_f, seq, [w_out[l], w_branch_sb[l], w_branch_fox[l]])
        o_sb, o_fx = _attention(qkv.reshape(batch, seq, n_qkv),
                                cum_col.reshape(batch, seq, LANES), cum_row, batch, seq)
        merged = _merge(o_sb.reshape(batch * seq, D_BRANCH), o_fx.reshape(batch * seq, D_BRANCH),
                        gates, wsb_bf, wfx_bf)
        h1, u2 = _out_proj(merged, h, wo_bf,
                           norm_mix_post[l].reshape(1, d), norm_ffn_pre[l].reshape(1, d))
        h = _ffn(u2, h1, wg_bf, wu_bf, wd_bf, norm_ffn_post[l].reshape(1, d))
    return h.reshape(batch, seq, d)
```

```python
import functools

import jax
import jax.numpy as jnp
from jax import lax
from jax.experimental import pallas as pl
from jax.experimental.pallas import tpu as pltpu

F32 = jnp.float32
BF16 = jnp.bfloat16

D_MODEL = 2048
HEAD_DIM = 128
N_HEADS = 8
D_BRANCH = N_HEADS * HEAD_DIM
D_FF = 5632
RMS_EPS = 1e-6
ATTN_SCALE = HEAD_DIM ** -0.5
NEG_BIG = -1e30
LOG2E = 1.4426950408889634
MAX_NEG_LOGIT = 126.0

LANES = 128
BF16_SUBLANES = 16
VMEM_LIMIT = 56 * 1024 * 1024

TM_PROJ = 1024
TN_PROJ = 1024
CUM_CHUNK = 256
TK = 256
RC = 256
QC = 256
SB_LOOKAHEAD_QK = 3
SB_LOOKAHEAD_CUM = 2
FOX_LOOKAHEAD = 3
TM_MERGE = 1024
TN_MERGE = 1024
TM_OUT = 512
OUT_ROWS = 256
TM_FFN = 1024
TF_FFN = 512
DOWN_CHUNK = 512
NORM_ROWS = 256


def _params(semantics):
    return pltpu.CompilerParams(dimension_semantics=semantics, vmem_limit_bytes=VMEM_LIMIT)


def _rms_norm(x, g):
    ms = jnp.mean(x * x, axis=-1, keepdims=True)
    return (x * lax.rsqrt(ms + RMS_EPS)) * g


def _log_sigmoid(x):
    return jnp.minimum(x, 0.0) - jnp.log1p(jnp.exp(-jnp.abs(x)))


def _sigmoid(x):
    return 1.0 / (1.0 + jnp.exp(-x))


def _dot_nt(a, b):
    return lax.dot_general(a, b, (((1,), (1,)), ((), ())), preferred_element_type=F32)


def _split3(x):
    hi = x.astype(BF16)
    r1 = x - hi.astype(F32)
    mid = r1.astype(BF16)
    lo = (r1 - mid.astype(F32)).astype(BF16)
    return hi, mid, lo


def _rider_plan(weights, steps_each):
    plan, s0 = [], 0
    for w, c in zip(weights, steps_each):
        assert w.shape[0] % c == 0 and (w.shape[0] // c) % BF16_SUBLANES == 0
        plan.append((w, s0, c))
        s0 += c
    return plan


def _rider_specs(plan, n_j):
    specs, shapes = [], []
    for w, s0, c in plan:
        def index(i, j, s0=s0, c=c):
            return (jnp.clip(i * n_j + j - s0, 0, c - 1), 0)
        specs.append(pl.BlockSpec((w.shape[0] // c, w.shape[1]), index))
        shapes.append(jax.ShapeDtypeStruct(w.shape, BF16))
    return specs, shapes


def _run_riders(step, windows, src_refs, dst_refs):
    for (s0, c), src, dst in zip(windows, src_refs, dst_refs):
        @pl.when((step >= s0) & (step < s0 + c))
        def _(src=src, dst=dst):
            dst[...] = src[...].astype(BF16)


def _qkv_kernel(x_ref, g_ref, w_ref, cs_ref, *refs, windows):
    k = len(windows)
    src_refs, (o_ref, u_ref), dst_refs = refs[:k], refs[k:k + 2], refs[k + 2:]
    _run_riders(pl.program_id(0) * pl.num_programs(1) + pl.program_id(1), windows,
                src_refs, dst_refs)

    @pl.when(pl.program_id(1) == 0)
    def _():
        u_ref[...] = _rms_norm(x_ref[...], g_ref[...]).astype(BF16)

    acc = jnp.dot(u_ref[...], w_ref[...], preferred_element_type=F32)
    o_ref[...] = (acc * cs_ref[...]).astype(o_ref.dtype)


def _qkv_proj(x2, g, w, col_scale, ride):
    t, d = x2.shape
    n = col_scale.shape[1]
    n_i, n_j = t // TM_PROJ, n // TN_PROJ
    plan = _rider_plan(ride, [n_i * n_j // len(ride)] * len(ride))
    ride_specs, ride_shapes = _rider_specs(plan, n_j)
    return pl.pallas_call(
        functools.partial(_qkv_kernel, windows=[(s0, c) for _, s0, c in plan]),
        grid=(n_i, n_j),
        in_specs=[
            pl.BlockSpec((TM_PROJ, d), lambda i, j: (i, 0)),
            pl.BlockSpec((1, d), lambda i, j: (0, 0)),
            pl.BlockSpec((d, TN_PROJ), lambda i, j: (0, j)),
            pl.BlockSpec((1, TN_PROJ), lambda i, j: (0, j)),
        ] + ride_specs,
        out_specs=[
            pl.BlockSpec((TM_PROJ, TN_PROJ), lambda i, j: (i, j)),
            pl.BlockSpec((TM_PROJ, d), lambda i, j: (i, 0)),
        ] + ride_specs,
        out_shape=[
            jax.ShapeDtypeStruct((t, n), BF16),
            jax.ShapeDtypeStruct((t, d), BF16),
        ] + ride_shapes,
        compiler_params=_params(("arbitrary", "arbitrary")),
        name="qkv_proj",
    )(x2, g, w, col_scale, *ride)


def _gates_kernel(u_ref, wt_hbm_ref, wf_ref, bf_ref, *refs, tiles_per_seq, windows, row0):
    k = len(windows)
    src_refs, (o_ref, cumc_ref, cumr_ref), dst_refs = refs[:k], refs[k:k + 3], refs[k + 3:-3]
    carry_ref, w_buf, w_sem = refs[-3:]
    i, j = pl.program_id(0), pl.program_id(1)
    n_j = pl.num_programs(1)
    step = i * n_j + j
    _run_riders(step, windows, src_refs, dst_refs)

    def slab_copy(jj, slot):
        rows = pl.ds(row0 + jj * TN_PROJ, TN_PROJ)
        return pltpu.make_async_copy(wt_hbm_ref.at[rows, :], w_buf.at[slot], w_sem.at[slot])

    slot = step % 2

    @pl.when(step == 0)
    def _():
        slab_copy(0, 0).start()

    @pl.when(step + 1 < pl.num_programs(0) * n_j)
    def _():
        slab_copy((j + 1) % n_j, 1 - slot).start()

    @pl.when(pl.program_id(1) == 0)
    def _():
        f = _dot_nt(u_ref[...], wf_ref[...].astype(BF16))
        log_f = _log_sigmoid(f + bf_ref[...])

        @pl.when(i % tiles_per_seq == 0)
        def _():
            carry_ref[...] = jnp.zeros_like(carry_ref)

        r = lax.broadcasted_iota(jnp.int32, (CUM_CHUNK, CUM_CHUNK), 0)
        c = lax.broadcasted_iota(jnp.int32, (CUM_CHUNK, CUM_CHUNK), 1)
        lower = (c <= r).astype(BF16)
        carry = carry_ref[...]
        for ch in range(TM_PROJ // CUM_CHUNK):
            blk = log_f[ch * CUM_CHUNK:(ch + 1) * CUM_CHUNK]
            hi, mid, lo = _split3(blk)
            cs = (jnp.dot(lower, hi, preferred_element_type=F32)
                  + jnp.dot(lower, mid, preferred_element_type=F32)
                  + jnp.dot(lower, lo, preferred_element_type=F32)) + carry
            cumc_ref[ch * CUM_CHUNK:(ch + 1) * CUM_CHUNK, :] = cs
            carry = cs[CUM_CHUNK - 1:CUM_CHUNK, :]
        carry_ref[...] = carry
        cumr_ref[0] = cumc_ref[...].T[:N_HEADS, :]

    slab_copy(j, slot).wait()
    acc = _dot_nt(u_ref[...], w_buf[slot].astype(BF16))
    o_ref[...] = _sigmoid(acc).astype(o_ref.dtype)


def _gates_proj(u, w_t, f_row0, n, bf, seq, ride):
    t, d = u.shape
    assert f_row0 % LANES == 0
    tiles_per_seq = seq // TM_PROJ
    n_i, n_j = t // TM_PROJ, n // TN_PROJ
    ride_rows = sum(r.shape[0] for r in ride)
    plan = _rider_plan(ride, [n_i * n_j * r.shape[0] // ride_rows for r in ride])
    ride_specs, ride_shapes = _rider_specs(plan, n_j)
    return pl.pallas_call(
        functools.partial(_gates_kernel, tiles_per_seq=tiles_per_seq,
                          windows=[(s0, c) for _, s0, c in plan], row0=f_row0 + N_HEADS),
        grid=(n_i, n_j),
        in_specs=[
            pl.BlockSpec((TM_PROJ, d), lambda i, j: (i, 0)),
            pl.BlockSpec(memory_space=pl.ANY),
            pl.BlockSpec((LANES, d), lambda i, j: (f_row0 // LANES, 0)),
            pl.BlockSpec((1, LANES), lambda i, j: (0, 0)),
        ] + ride_specs,
        out_specs=[
            pl.BlockSpec((TM_PROJ, TN_PROJ), lambda i, j: (i, j)),
            pl.BlockSpec((TM_PROJ, LANES), lambda i, j: (i, 0)),
            pl.BlockSpec((1, N_HEADS, TM_PROJ),
                         lambda i, j: (i // tiles_per_seq, 0, i % tiles_per_seq)),
        ] + ride_specs,
        out_shape=[
            jax.ShapeDtypeStruct((t, n), BF16),
            jax.ShapeDtypeStruct((t, LANES), F32),
            jax.ShapeDtypeStruct((t // seq, N_HEADS, seq), F32),
        ] + ride_shapes,
        scratch_shapes=[pltpu.VMEM((1, LANES), F32), pltpu.VMEM((2, TN_PROJ, d), F32),
                        pltpu.SemaphoreType.DMA((2,))],
        compiler_params=_params(("arbitrary", "arbitrary")),
        name="gates_proj",
    )(u, w_t, w_t, bf, *ride)


def _qk(q, k):
    return lax.dot_general(q, k, (((1,), (1,)), ((), ())), preferred_element_type=F32)


def _sb_kernel(q_ref, k_ref, v_ref, o_ref, *, seq):
    r = lax.broadcasted_iota(jnp.int32, (TK, TK), 0)
    c = lax.broadcasted_iota(jnp.int32, (TK, TK), 1)
    later = (r > c).astype(BF16)

    def valid_mask(d, c):
        k0, r0 = d * TK, c * RC
        if k0 + TK <= r0:
            return None
        return (k0 + lax.broadcasted_iota(jnp.int32, (RC, TK), 1)
                < r0 + lax.broadcasted_iota(jnp.int32, (RC, TK), 0))

    def gates(d, c):
        q = q_ref[0, c * RC:(c + 1) * RC, :]
        k = k_ref[0, d * TK:(d + 1) * TK, :]
        nz = jnp.minimum(_qk(q, k), MAX_NEG_LOGIT)
        neg_log_beta = jnp.log2(1.0 + jnp.exp2(nz))
        log_keep = nz - neg_log_beta
        valid = valid_mask(d, c)
        if valid is not None:
            log_keep = jnp.where(valid, log_keep, 0.0)
        return neg_log_beta, log_keep.astype(BF16), jnp.sum(log_keep, axis=1, keepdims=True)

    def within_block(neg_log_beta, keep_bf, row_sum):
        return jnp.dot(keep_bf, later, preferred_element_type=F32) - neg_log_beta, row_sum

    def accumulate(d, c, log_w, row_sum, carry, acc):
        v = v_ref[0, d * TK:(d + 1) * TK, :]
        w = jnp.exp2(log_w + carry)
        valid = valid_mask(d, c)
        if valid is not None:
            w = jnp.where(valid, w, 0.0)
        return carry + row_sum, acc + jnp.dot(w.astype(BF16), v, preferred_element_type=F32)

    n_chunks = seq // RC
    state = [(jnp.zeros((RC, 1), F32), jnp.zeros((RC, HEAD_DIM), F32)) for _ in range(n_chunks)]
    items = [(d, c) for d in reversed(range(seq // TK)) for c in range(d * TK // RC, n_chunks)]
    n = len(items)
    staged_a, staged_b = {}, {}
    for step in range(-SB_LOOKAHEAD_QK, n):
        i_a, i_b = step + SB_LOOKAHEAD_QK, step + SB_LOOKAHEAD_CUM
        if i_a < n:
            staged_a[i_a] = gates(*items[i_a])
        if 0 <= i_b < n:
            staged_b[i_b] = within_block(*staged_a.pop(i_b))
        if step >= 0:
            d, c = items[step]
            state[c] = accumulate(d, c, *staged_b.pop(step), *state[c])
    for c in range(n_chunks):
        o_ref[0, c * RC:(c + 1) * RC, :] = state[c][1].astype(o_ref.dtype)


def _fox_kernel(q_ref, k_ref, v_ref, cq_ref, ck_ref, o_ref, *, seq):
    h = pl.program_id(1)
    lane = lax.broadcasted_iota(jnp.int32, (TK, LANES), 1)
    ones_col = (lane == 0).astype(BF16)
    n_blocks = seq // TK
    ck_blocks = [jnp.sum(jnp.where(lane == h, ck_ref[0, d * TK:(d + 1) * TK, :], 0.0),
                         axis=1, keepdims=True) * LOG2E for d in range(n_blocks)]
    v_blocks = [jnp.concatenate([v_ref[0, d * TK:(d + 1) * TK, :], ones_col], axis=1)
                for d in range(n_blocks)]

    def scores(d, c):
        k0, c0 = d * TK, c * QC
        q = q_ref[0, c0:c0 + QC, :]
        k = k_ref[0, k0:k0 + TK, :]
        y = _qk(k, q) - ck_blocks[d]
        if k0 + TK > c0:
            valid = (k0 + lax.broadcasted_iota(jnp.int32, (TK, QC), 0)
                     <= c0 + lax.broadcasted_iota(jnp.int32, (TK, QC), 1))
            y = jnp.where(valid, y, NEG_BIG)
        return y

    def accumulate(d, c, y, m, l, acc):
        c0 = c * QC
        cq = cq_ref[0, pl.ds(h, 1), c0:c0 + QC] * LOG2E
        m_new = jnp.maximum(m, cq + jnp.max(y, axis=0, keepdims=True))
        alpha = jnp.exp2(m - m_new)
        p = jnp.exp2(y - (m_new - cq))
        pv = lax.dot_general(v_blocks[d], p.astype(BF16), (((0,), (0,)), ((), ())),
                             preferred_element_type=F32)
        l = alpha * l + pv[HEAD_DIM:HEAD_DIM + 1]
        return m_new, l, alpha * acc + pv[:HEAD_DIM]

    n_chunks = seq // QC
    state = [(jnp.full((1, QC), NEG_BIG, F32), jnp.zeros((1, QC), F32),
              jnp.zeros((HEAD_DIM, QC), F32)) for _ in range(n_chunks)]
    items = [(d, c) for d in range(seq // TK) for c in range(d * TK // QC, n_chunks)]
    ys = [scores(*item) for item in items[:FOX_LOOKAHEAD]]
    for i, (d, c) in enumerate(items):
        if i + FOX_LOOKAHEAD < len(items):
            ys.append(scores(*items[i + FOX_LOOKAHEAD]))
        state[c] = accumulate(d, c, ys.pop(0), *state[c])
    for c in range(n_chunks):
        _, l, acc = state[c]
        o_ref[0, c * QC:(c + 1) * QC, :] = (acc / l).T.astype(o_ref.dtype)


def _attention(qkv, cum_col, cum_row, batch, seq):
    grid = (batch, N_HEADS)

    def head_spec(head0):
        return pl.BlockSpec((1, seq, HEAD_DIM), lambda b, h: (b, 0, head0 + h))

    out_shape = jax.ShapeDtypeStruct((batch, seq, D_BRANCH), BF16)
    sem = ("arbitrary", "arbitrary")

    o_sb = pl.pallas_call(
        functools.partial(_sb_kernel, seq=seq),
        grid=grid,
        in_specs=[head_spec(0), head_spec(N_HEADS), head_spec(2 * N_HEADS)],
        out_specs=head_spec(0),
        out_shape=out_shape,
        compiler_params=_params(sem),
        name="stickbreak_attn",
    )(qkv, qkv, qkv)

    o_fx = pl.pallas_call(
        functools.partial(_fox_kernel, seq=seq),
        grid=grid,
        in_specs=[
            head_spec(3 * N_HEADS), head_spec(4 * N_HEADS), head_spec(5 * N_HEADS),
            pl.BlockSpec((1, N_HEADS, seq), lambda b, h: (b, 0, 0)),
            pl.BlockSpec((1, seq, LANES), lambda b, h: (b, 0, 0)),
        ],
        out_specs=head_spec(0),
        out_shape=out_shape,
        compiler_params=_params(sem),
        name="forgetting_attn",
    )(qkv, qkv, qkv, cum_row, cum_col)
    return o_sb, o_fx


def _merge_kernel(osb_ref, ofx_ref, gsb_ref, gfx_ref, wsb_ref, wfx_ref, o_ref):
    a = jnp.dot(osb_ref[...], wsb_ref[...], preferred_element_type=F32)
    b = jnp.dot(ofx_ref[...], wfx_ref[...], preferred_element_type=F32)
    o_ref[...] = (gsb_ref[...].astype(F32) * a + gfx_ref[...].astype(F32) * b).astype(o_ref.dtype)


def _merge(o_sb, o_fx, gates, w_sb, w_fx):
    t, k = o_sb.shape
    d = w_sb.shape[1]
    tm, tn = TM_MERGE, TN_MERGE
    n_j = d // tn
    return pl.pallas_call(
        _merge_kernel,
        grid=(t // tm, n_j),
        in_specs=[
            pl.BlockSpec((tm, k), lambda i, j: (i, 0)),
            pl.BlockSpec((tm, k), lambda i, j: (i, 0)),
            pl.BlockSpec((tm, tn), lambda i, j: (i, j)),
            pl.BlockSpec((tm, tn), lambda i, j: (i, n_j + j)),
            pl.BlockSpec((k, tn), lambda i, j: (0, j)),
            pl.BlockSpec((k, tn), lambda i, j: (0, j)),
        ],
        out_specs=pl.BlockSpec((tm, tn), lambda i, j: (i, j)),
        out_shape=jax.ShapeDtypeStruct((t, d), BF16),
        compiler_params=_params(("arbitrary", "arbitrary")),
        name="gated_merge",
    )(o_sb, o_fx, gates, gates, w_sb, w_fx)


def _out_proj_kernel(m_ref, x_ref, wout_ref, npost_ref, npre_ref, h_ref, u_ref):
    tm = m_ref.shape[0]
    chunks = [slice(r0, r0 + OUT_ROWS) for r0 in range(0, tm, OUT_ROWS)]
    mixes = [jnp.dot(m_ref[rows, :], wout_ref[...], preferred_element_type=F32)
             for rows in chunks]
    for rows, mix in zip(chunks, mixes):
        h = x_ref[rows, :] + _rms_norm(mix, npost_ref[...])
        h_ref[rows, :] = h
        u_ref[rows, :] = _rms_norm(h, npre_ref[...]).astype(BF16)


def _const_spec(shape):
    return pl.BlockSpec(shape, lambda i: (0,) * len(shape), pipeline_mode=pl.Buffered(1))


def _out_proj(merged, x2, w_out, n_post, n_ffn_pre):
    t, d = x2.shape
    tm = TM_OUT
    return pl.pallas_call(
        _out_proj_kernel,
        grid=(t // tm,),
        in_specs=[
            pl.BlockSpec((tm, d), lambda i: (i, 0)),
            pl.BlockSpec((tm, d), lambda i: (i, 0)),
            _const_spec((d, d)),
            _const_spec((1, d)),
            _const_spec((1, d)),
        ],
        out_specs=[
            pl.BlockSpec((tm, d), lambda i: (i, 0)),
            pl.BlockSpec((tm, d), lambda i: (i, 0)),
        ],
        out_shape=[
            jax.ShapeDtypeStruct((t, d), F32),
            jax.ShapeDtypeStruct((t, d), BF16),
        ],
        compiler_params=_params(("arbitrary",)),
        name="out_proj",
    )(merged, x2, w_out, n_post, n_ffn_pre)


def _ffn_kernel(u_ref, h_hbm_ref, wg_ref, wu_ref, wd_ref, npost_ref, o_ref, h_buf, h_sem):
    i, j = pl.program_id(0), pl.program_id(1)
    tm = o_ref.shape[0]
    h_copy = pltpu.make_async_copy(h_hbm_ref.at[pl.ds(i * tm, tm), :], h_buf, h_sem)

    @pl.when(j == 0)
    def _():
        h_copy.start()
        o_ref[...] = jnp.zeros_like(o_ref)

    u = u_ref[...]
    g = jnp.dot(u, wg_ref[...], preferred_element_type=F32)
    up = jnp.dot(u, wu_ref[...], preferred_element_type=F32)
    hidden = ((g * _sigmoid(g)) * up).astype(BF16)
    for c0 in range(0, o_ref.shape[1], DOWN_CHUNK):
        cols = slice(c0, c0 + DOWN_CHUNK)
        o_ref[:, cols] += jnp.dot(hidden, wd_ref[:, cols], preferred_element_type=F32)

    @pl.when(j == pl.num_programs(1) - 1)
    def _():
        h_copy.wait()
        for r0 in range(0, tm, NORM_ROWS):
            rows = slice(r0, r0 + NORM_ROWS)
            o_ref[rows, :] = h_buf[rows, :] + _rms_norm(o_ref[rows, :], npost_ref[...])


def _ffn(u2, h1, w_gate, w_up, w_down, n_post):
    t, d = h1.shape
    f = w_gate.shape[1]
    tm, tf = TM_FFN, TF_FFN
    return pl.pallas_call(
        _ffn_kernel,
        grid=(t // tm, f // tf),
        in_specs=[
            pl.BlockSpec((tm, d), lambda i, j: (i, 0)),
            pl.BlockSpec(memory_space=pl.ANY),
            pl.BlockSpec((d, tf), lambda i, j: (0, j)),
            pl.BlockSpec((d, tf), lambda i, j: (0, j)),
            pl.BlockSpec((tf, d), lambda i, j: (j, 0)),
            pl.BlockSpec((1, d), lambda i, j: (0, 0)),
        ],
        out_specs=pl.BlockSpec((tm, d), lambda i, j: (i, 0)),
        out_shape=jax.ShapeDtypeStruct((t, d), F32),
        scratch_shapes=[pltpu.VMEM((tm, d), F32), pltpu.SemaphoreType.DMA(())],
        compiler_params=_params(("arbitrary", "arbitrary")),
        name="swiglu_ffn",
    )(u2, h1, w_gate, w_up, w_down, n_post)


def kernel(x, norm_mix_pre, norm_mix_post, w_in, b_forget, w_branch_sb, w_branch_fox, w_out,
           norm_ffn_pre, norm_ffn_post, w_ffn_gate, w_ffn_up, w_ffn_down):
    batch, seq, d = x.shape
    depth = w_in.shape[0]
    n_qkv = 6 * D_BRANCH
    assert n_qkv % LANES == 0 and seq % TM_PROJ == 0
    h = x.reshape(batch * seq, d)
    for l in range(depth):
        w_q = w_in[l][:, :n_qkv + LANES].astype(BF16)
        w_t = jnp.swapaxes(w_in[l], 0, 1)
        b_f = jnp.pad(b_forget[l], (0, LANES - N_HEADS)).reshape(1, LANES)
        g_pre = norm_mix_pre[l].reshape(1, d)

        q_scale = jnp.full((D_BRANCH,), ATTN_SCALE * LOG2E, F32)
        ones = jnp.ones((D_BRANCH,), F32)
        col_scale = jnp.concatenate([-q_scale, ones, ones, q_scale, ones, ones]).reshape(1, n_qkv)
        qkv, u1, wg_bf, wu_bf, wd_bf = _qkv_proj(
            h, g_pre, w_q, col_scale, [w_ffn_gate[l], w_ffn_up[l], w_ffn_down[l]])
        gates, cum_col, cum_row, wo_bf, wsb_bf, wfx_bf = _gates_proj(
            u1, w_t, n_qkv, 2 * d, b_f, seq, [w_out[l], w_branch_sb[l], w_branch_fox[l]])
        o_sb, o_fx = _attention(qkv.reshape(batch, seq, n_qkv),
                                cum_col.reshape(batch, seq, LANES), cum_row, batch, seq)
        merged = _merge(o_sb.reshape(batch * seq, D_BRANCH), o_fx.reshape(batch * seq, D_BRANCH),
                        gates, wsb_bf, wfx_bf)
        h1, u2 = _out_proj(merged, h, wo_bf,
                           norm_mix_post[l].reshape(1, d), norm_ffn_pre[l].reshape(1, d))
        h = _ffn(u2, h1, wg_bf, wu_bf, wd_bf, norm_ffn_post[l].reshape(1, d))
    return h.reshape(batch, seq, d)
```

```python
import functools

import jax
import jax.numpy as jnp
from jax import lax
from jax.experimental import pallas as pl
from jax.experimental.pallas import tpu as pltpu

F32 = jnp.float32
BF16 = jnp.bfloat16

D_MODEL = 2048
HEAD_DIM = 128
N_HEADS = 8
D_BRANCH = N_HEADS * HEAD_DIM
D_FF = 5632
RMS_EPS = 1e-6
ATTN_SCALE = HEAD_DIM ** -0.5
NEG_BIG = -1e30
LOG2E = 1.4426950408889634
MAX_NEG_LOGIT = 126.0

LANES = 128
BF16_SUBLANES = 16
PITCH_PAD = LANES
VMEM_LIMIT = 56 * 1024 * 1024

TM_PROJ = 1024
TN_PROJ = 1024
CUM_CHUNK = 256
TK = 256
RC = 256
QC = 256
SB_LOOKAHEAD_QK = 3
SB_LOOKAHEAD_CUM = 2
FOX_LOOKAHEAD = 3
TM_MERGE = 1024
TN_MERGE = 1024
TM_OUT = 512
OUT_ROWS = 256
TM_FFN = 1024
TF_FFN = 512
DOWN_CHUNK = 512
NORM_ROWS = 256


def _params(semantics):
    return pltpu.CompilerParams(dimension_semantics=semantics, vmem_limit_bytes=VMEM_LIMIT)


def _rms_norm(x, g):
    ms = jnp.mean(x * x, axis=-1, keepdims=True)
    return (x * lax.rsqrt(ms + RMS_EPS)) * g


def _log_sigmoid(x):
    return jnp.minimum(x, 0.0) - jnp.log1p(jnp.exp(-jnp.abs(x)))


def _sigmoid(x):
    return 1.0 / (1.0 + jnp.exp(-x))


def _dot_nt(a, b):
    return lax.dot_general(a, b, (((1,), (1,)), ((), ())), preferred_element_type=F32)


def _split3(x):
    hi = x.astype(BF16)
    r1 = x - hi.astype(F32)
    mid = r1.astype(BF16)
    lo = (r1 - mid.astype(F32)).astype(BF16)
    return hi, mid, lo


def _rider_plan(weights, steps_each):
    plan, s0 = [], 0
    for w, c in zip(weights, steps_each):
        assert w.shape[0] % c == 0 and (w.shape[0] // c) % BF16_SUBLANES == 0
        plan.append((w, s0, c))
        s0 += c
    return plan


def _rider_specs(plan, n_j, pad_cols):
    src_specs, dst_specs, shapes = [], [], []
    for (w, s0, c), pad in zip(plan, pad_cols):
        def index(i, j, s0=s0, c=c):
            return (jnp.clip(i * n_j + j - s0, 0, c - 1), 0)
        src_specs.append(pl.BlockSpec((w.shape[0] // c, w.shape[1]), index))
        dst_specs.append(pl.BlockSpec((w.shape[0] // c, w.shape[1] + pad), index))
        shapes.append(jax.ShapeDtypeStruct((w.shape[0], w.shape[1] + pad), BF16))
    return src_specs, dst_specs, shapes


def _run_riders(step, windows, src_refs, dst_refs):
    for (s0, c), src, dst in zip(windows, src_refs, dst_refs):
        @pl.when((step >= s0) & (step < s0 + c))
        def _(src=src, dst=dst):
            cols = src.shape[1]
            dst[:, :cols] = src[...].astype(BF16)
            if dst.shape[1] > cols:
                dst[:, cols:] = jnp.zeros((dst.shape[0], dst.shape[1] - cols), BF16)


def _qkv_kernel(x_ref, g_ref, w_ref, cs_ref, *refs, windows):
    k = len(windows)
    src_refs, (o_ref, u_ref), dst_refs = refs[:k], refs[k:k + 2], refs[k + 2:]
    _run_riders(pl.program_id(0) * pl.num_programs(1) + pl.program_id(1), windows,
                src_refs, dst_refs)

    @pl.when(pl.program_id(1) == 0)
    def _():
        u_ref[...] = _rms_norm(x_ref[...], g_ref[...]).astype(BF16)

    acc = _dot_nt(u_ref[...], w_ref[...])
    o_ref[...] = (acc * cs_ref[...]).astype(o_ref.dtype)


def _qkv_proj(x2, g, w_t, col_scale, ride):
    t, d = x2.shape
    n = col_scale.shape[1]
    n_i, n_j = t // TM_PROJ, n // TN_PROJ
    plan = _rider_plan(ride, [n_i * n_j // len(ride)] * len(ride))
    ride_src, ride_dst, ride_shapes = _rider_specs(plan, n_j, [0] * len(ride))
    return pl.pallas_call(
        functools.partial(_qkv_kernel, windows=[(s0, c) for _, s0, c in plan]),
        grid=(n_i, n_j),
        in_specs=[
            pl.BlockSpec((TM_PROJ, d), lambda i, j: (i, 0)),
            pl.BlockSpec((1, d), lambda i, j: (0, 0)),
            pl.BlockSpec((TN_PROJ, d), lambda i, j: (j, 0)),
            pl.BlockSpec((1, TN_PROJ), lambda i, j: (0, j)),
        ] + ride_src,
        out_specs=[
            pl.BlockSpec((TM_PROJ, TN_PROJ), lambda i, j: (i, j)),
            pl.BlockSpec((TM_PROJ, d), lambda i, j: (i, 0)),
        ] + ride_dst,
        out_shape=[
            jax.ShapeDtypeStruct((t, n), BF16),
            jax.ShapeDtypeStruct((t, d), BF16),
        ] + ride_shapes,
        compiler_params=_params(("arbitrary", "arbitrary")),
        name="qkv_proj",
    )(x2, g, w_t, col_scale, *ride)


def _gates_kernel(u_ref, wt_hbm_ref, wf_ref, bf_ref, *refs, tiles_per_seq, windows, row0):
    k = len(windows)
    src_refs, (o_ref, cumc_ref, cumr_ref), dst_refs = refs[:k], refs[k:k + 3], refs[k + 3:-3]
    carry_ref, w_buf, w_sem = refs[-3:]
    i, j = pl.program_id(0), pl.program_id(1)
    n_j = pl.num_programs(1)
    step = i * n_j + j
    _run_riders(step, windows, src_refs, dst_refs)

    def slab_copy(jj, slot):
        rows = pl.ds(row0 + jj * TN_PROJ, TN_PROJ)
        return pltpu.make_async_copy(wt_hbm_ref.at[rows, :], w_buf.at[slot], w_sem.at[slot])

    slot = step % 2

    @pl.when(step == 0)
    def _():
        slab_copy(0, 0).start()

    @pl.when(step + 1 < pl.num_programs(0) * n_j)
    def _():
        slab_copy((j + 1) % n_j, 1 - slot).start()

    @pl.when(pl.program_id(1) == 0)
    def _():
        f = _dot_nt(u_ref[...], wf_ref[...].astype(BF16))
        log_f = _log_sigmoid(f + bf_ref[...])

        @pl.when(i % tiles_per_seq == 0)
        def _():
            carry_ref[...] = jnp.zeros_like(carry_ref)

        r = lax.broadcasted_iota(jnp.int32, (CUM_CHUNK, CUM_CHUNK), 0)
        c = lax.broadcasted_iota(jnp.int32, (CUM_CHUNK, CUM_CHUNK), 1)
        lower = (c <= r).astype(BF16)
        carry = carry_ref[...]
        for ch in range(TM_PROJ // CUM_CHUNK):
            blk = log_f[ch * CUM_CHUNK:(ch + 1) * CUM_CHUNK]
            hi, mid, lo = _split3(blk)
            cs = (jnp.dot(lower, hi, preferred_element_type=F32)
                  + jnp.dot(lower, mid, preferred_element_type=F32)
                  + jnp.dot(lower, lo, preferred_element_type=F32)) + carry
            cumc_ref[ch * CUM_CHUNK:(ch + 1) * CUM_CHUNK, :] = cs
            carry = cs[CUM_CHUNK - 1:CUM_CHUNK, :]
        carry_ref[...] = carry
        cumr_ref[0] = cumc_ref[...].T[:N_HEADS, :]

    slab_copy(j, slot).wait()
    acc = _dot_nt(u_ref[...], w_buf[slot].astype(BF16))
    o_ref[...] = _sigmoid(acc).astype(o_ref.dtype)


def _gates_proj(u, w_t, f_row0, n, bf, seq, ride, ride_pad):
    t, d = u.shape
    assert f_row0 % LANES == 0
    tiles_per_seq = seq // TM_PROJ
    n_i, n_j = t // TM_PROJ, n // TN_PROJ
    ride_rows = sum(r.shape[0] for r in ride)
    plan = _rider_plan(ride, [n_i * n_j * r.shape[0] // ride_rows for r in ride])
    ride_src, ride_dst, ride_shapes = _rider_specs(plan, n_j, ride_pad)
    return pl.pallas_call(
        functools.partial(_gates_kernel, tiles_per_seq=tiles_per_seq,
                          windows=[(s0, c) for _, s0, c in plan], row0=f_row0 + N_HEADS),
        grid=(n_i, n_j),
        in_specs=[
            pl.BlockSpec((TM_PROJ, d), lambda i, j: (i, 0)),
            pl.BlockSpec(memory_space=pl.ANY),
            pl.BlockSpec((LANES, d), lambda i, j: (f_row0 // LANES, 0)),
            pl.BlockSpec((1, LANES), lambda i, j: (0, 0)),
        ] + ride_src,
        out_specs=[
            pl.BlockSpec((TM_PROJ, TN_PROJ), lambda i, j: (i, j)),
            pl.BlockSpec((TM_PROJ, LANES), lambda i, j: (i, 0)),
            pl.BlockSpec((1, N_HEADS, TM_PROJ),
                         lambda i, j: (i // tiles_per_seq, 0, i % tiles_per_seq)),
        ] + ride_dst,
        out_shape=[
            jax.ShapeDtypeStruct((t, n), BF16),
            jax.ShapeDtypeStruct((t, LANES), F32),
            jax.ShapeDtypeStruct((t // seq, N_HEADS, seq), F32),
        ] + ride_shapes,
        scratch_shapes=[pltpu.VMEM((1, LANES), F32), pltpu.VMEM((2, TN_PROJ, d), F32),
                        pltpu.SemaphoreType.DMA((2,))],
        compiler_params=_params(("arbitrary", "arbitrary")),
        name="gates_proj",
    )(u, w_t, w_t, bf, *ride)


def _qk(q, k):
    return lax.dot_general(q, k, (((1,), (1,)), ((), ())), preferred_element_type=F32)


def _sb_kernel(q_ref, k_ref, v_ref, o_ref, *, seq):
    r = lax.broadcasted_iota(jnp.int32, (TK, TK), 0)
    c = lax.broadcasted_iota(jnp.int32, (TK, TK), 1)
    later = (r > c).astype(BF16)

    def valid_mask(d, c):
        k0, r0 = d * TK, c * RC
        if k0 + TK <= r0:
            return None
        return (k0 + lax.broadcasted_iota(jnp.int32, (RC, TK), 1)
                < r0 + lax.broadcasted_iota(jnp.int32, (RC, TK), 0))

    def gates(d, c):
        q = q_ref[0, c * RC:(c + 1) * RC, :]
        k = k_ref[0, d * TK:(d + 1) * TK, :]
        nz = jnp.minimum(_qk(q, k), MAX_NEG_LOGIT)
        neg_log_beta = jnp.log2(1.0 + jnp.exp2(nz))
        log_keep = nz - neg_log_beta
        valid = valid_mask(d, c)
        if valid is not None:
            log_keep = jnp.where(valid, log_keep, 0.0)
        return neg_log_beta, log_keep.astype(BF16), jnp.sum(log_keep, axis=1, keepdims=True)

    def within_block(neg_log_beta, keep_bf, row_sum):
        return jnp.dot(keep_bf, later, preferred_element_type=F32) - neg_log_beta, row_sum

    def accumulate(d, c, log_w, row_sum, carry, acc):
        v = v_ref[0, d * TK:(d + 1) * TK, :]
        w = jnp.exp2(log_w + carry)
        valid = valid_mask(d, c)
        if valid is not None:
            w = jnp.where(valid, w, 0.0)
        return carry + row_sum, acc + jnp.dot(w.astype(BF16), v, preferred_element_type=F32)

    n_chunks = seq // RC
    state = [(jnp.zeros((RC, 1), F32), jnp.zeros((RC, HEAD_DIM), F32)) for _ in range(n_chunks)]
    items = [(d, c) for d in reversed(range(seq // TK)) for c in range(d * TK // RC, n_chunks)]
    n = len(items)
    staged_a, staged_b = {}, {}
    for step in range(-SB_LOOKAHEAD_QK, n):
        i_a, i_b = step + SB_LOOKAHEAD_QK, step + SB_LOOKAHEAD_CUM
        if i_a < n:
            staged_a[i_a] = gates(*items[i_a])
        if 0 <= i_b < n:
            staged_b[i_b] = within_block(*staged_a.pop(i_b))
        if step >= 0:
            d, c = items[step]
            state[c] = accumulate(d, c, *staged_b.pop(step), *state[c])
    for c in range(n_chunks):
        o_ref[0, c * RC:(c + 1) * RC, :] = state[c][1].astype(o_ref.dtype)


def _fox_kernel(q_ref, k_ref, v_ref, cq_ref, ck_ref, o_ref, *, seq):
    h = pl.program_id(1)
    lane = lax.broadcasted_iota(jnp.int32, (TK, LANES), 1)
    ones_col = (lane == 0).astype(BF16)
    n_blocks = seq // TK
    ck_blocks = [jnp.sum(jnp.where(lane == h, ck_ref[0, d * TK:(d + 1) * TK, :], 0.0),
                         axis=1, keepdims=True) * LOG2E for d in range(n_blocks)]
    v_blocks = [jnp.concatenate([v_ref[0, d * TK:(d + 1) * TK, :], ones_col], axis=1)
                for d in range(n_blocks)]

    def scores(d, c):
        k0, c0 = d * TK, c * QC
        q = q_ref[0, c0:c0 + QC, :]
        k = k_ref[0, k0:k0 + TK, :]
        y = _qk(k, q) - ck_blocks[d]
        if k0 + TK > c0:
            valid = (k0 + lax.broadcasted_iota(jnp.int32, (TK, QC), 0)
                     <= c0 + lax.broadcasted_iota(jnp.int32, (TK, QC), 1))
            y = jnp.where(valid, y, NEG_BIG)
        return y

    def accumulate(d, c, y, m, l, acc):
        c0 = c * QC
        cq = cq_ref[0, pl.ds(h, 1), c0:c0 + QC] * LOG2E
        m_new = jnp.maximum(m, cq + jnp.max(y, axis=0, keepdims=True))
        alpha = jnp.exp2(m - m_new)
        p = jnp.exp2(y - (m_new - cq))
        pv = lax.dot_general(v_blocks[d], p.astype(BF16), (((0,), (0,)), ((), ())),
                             preferred_element_type=F32)
        l = alpha * l + pv[HEAD_DIM:HEAD_DIM + 1]
        return m_new, l, alpha * acc + pv[:HEAD_DIM]

    n_chunks = seq // QC
    state = [(jnp.full((1, QC), NEG_BIG, F32), jnp.zeros((1, QC), F32),
              jnp.zeros((HEAD_DIM, QC), F32)) for _ in range(n_chunks)]
    items = [(d, c) for d in range(seq // TK) for c in range(d * TK // QC, n_chunks)]
    ys = [scores(*item) for item in items[:FOX_LOOKAHEAD]]
    for i, (d, c) in enumerate(items):
        if i + FOX_LOOKAHEAD < len(items):
            ys.append(scores(*items[i + FOX_LOOKAHEAD]))
        state[c] = accumulate(d, c, ys.pop(0), *state[c])
    for c in range(n_chunks):
        _, l, acc = state[c]
        o_ref[0, c * QC:(c + 1) * QC, :] = (acc / l).T.astype(o_ref.dtype)


def _attention(qkv, cum_col, cum_row, batch, seq):
    grid = (batch, N_HEADS)

    def head_spec(head0):
        return pl.BlockSpec((1, seq, HEAD_DIM), lambda b, h: (b, 0, head0 + h))

    out_shape = jax.ShapeDtypeStruct((batch, seq, D_BRANCH), BF16)
    sem = ("arbitrary", "arbitrary")

    o_sb = pl.pallas_call(
        functools.partial(_sb_kernel, seq=seq),
        grid=grid,
        in_specs=[head_spec(0), head_spec(N_HEADS), head_spec(2 * N_HEADS)],
        out_specs=head_spec(0),
        out_shape=out_shape,
        compiler_params=_params(sem),
        name="stickbreak_attn",
    )(qkv, qkv, qkv)

    o_fx = pl.pallas_call(
        functools.partial(_fox_kernel, seq=seq),
        grid=grid,
        in_specs=[
            head_spec(3 * N_HEADS), head_spec(4 * N_HEADS), head_spec(5 * N_HEADS),
            pl.BlockSpec((1, N_HEADS, seq), lambda b, h: (b, 0, 0)),
            pl.BlockSpec((1, seq, LANES), lambda b, h: (b, 0, 0)),
        ],
        out_specs=head_spec(0),
        out_shape=out_shape,
        compiler_params=_params(sem),
        name="forgetting_attn",
    )(qkv, qkv, qkv, cum_row, cum_col)
    return o_sb, o_fx


def _merge_kernel(osb_ref, ofx_ref, gsb_ref, gfx_ref, wsb_ref, wfx_ref, o_ref):
    a = jnp.dot(osb_ref[...], wsb_ref[...], preferred_element_type=F32)
    b = jnp.dot(ofx_ref[...], wfx_ref[...], preferred_element_type=F32)
    o_ref[...] = (gsb_ref[...].astype(F32) * a + gfx_ref[...].astype(F32) * b).astype(o_ref.dtype)


def _merge(o_sb, o_fx, gates, w_sb, w_fx):
    t, k = o_sb.shape
    d = w_sb.shape[1] - PITCH_PAD
    tm, tn = TM_MERGE, TN_MERGE
    n_j = d // tn
    return pl.pallas_call(
        _merge_kernel,
        grid=(t // tm, n_j),
        in_specs=[
            pl.BlockSpec((tm, k), lambda i, j: (i, 0)),
            pl.BlockSpec((tm, k), lambda i, j: (i, 0)),
            pl.BlockSpec((tm, tn), lambda i, j: (i, j)),
            pl.BlockSpec((tm, tn), lambda i, j: (i, n_j + j)),
            pl.BlockSpec((k, tn), lambda i, j: (0, j)),
            pl.BlockSpec((k, tn), lambda i, j: (0, j)),
        ],
        out_specs=pl.BlockSpec((tm, tn), lambda i, j: (i, j)),
        out_shape=jax.ShapeDtypeStruct((t, d), BF16),
        compiler_params=_params(("arbitrary", "arbitrary")),
        name="gated_merge",
    )(o_sb, o_fx, gates, gates, w_sb, w_fx)


def _out_proj_kernel(m_ref, x_ref, wout_ref, npost_ref, npre_ref, h_ref, u_ref):
    tm = m_ref.shape[0]
    chunks = [slice(r0, r0 + OUT_ROWS) for r0 in range(0, tm, OUT_ROWS)]
    mixes = [jnp.dot(m_ref[rows, :], wout_ref[...], preferred_element_type=F32)
             for rows in chunks]
    for rows, mix in zip(chunks, mixes):
        h = x_ref[rows, :] + _rms_norm(mix, npost_ref[...])
        h_ref[rows, :] = h
        u_ref[rows, :] = _rms_norm(h, npre_ref[...]).astype(BF16)


def _const_spec(shape):
    return pl.BlockSpec(shape, lambda i: (0,) * len(shape), pipeline_mode=pl.Buffered(1))


def _out_proj(merged, x2, w_out, n_post, n_ffn_pre):
    t, d = x2.shape
    tm = TM_OUT
    return pl.pallas_call(
        _out_proj_kernel,
        grid=(t // tm,),
        in_specs=[
            pl.BlockSpec((tm, d), lambda i: (i, 0)),
            pl.BlockSpec((tm, d), lambda i: (i, 0)),
            _const_spec((d, d)),
            _const_spec((1, d)),
            _const_spec((1, d)),
        ],
        out_specs=[
            pl.BlockSpec((tm, d), lambda i: (i, 0)),
            pl.BlockSpec((tm, d), lambda i: (i, 0)),
        ],
        out_shape=[
            jax.ShapeDtypeStruct((t, d), F32),
            jax.ShapeDtypeStruct((t, d), BF16),
        ],
        compiler_params=_params(("arbitrary",)),
        name="out_proj",
    )(merged, x2, w_out, n_post, n_ffn_pre)


def _ffn_kernel(u_ref, h_hbm_ref, wg_ref, wu_ref, wd_ref, npost_ref, o_ref, h_buf, h_sem):
    i, j = pl.program_id(0), pl.program_id(1)
    tm = o_ref.shape[0]
    h_copy = pltpu.make_async_copy(h_hbm_ref.at[pl.ds(i * tm, tm), :], h_buf, h_sem)

    @pl.when(j == 0)
    def _():
        h_copy.start()
        o_ref[...] = jnp.zeros_like(o_ref)

    u = u_ref[...]
    g = jnp.dot(u, wg_ref[...], preferred_element_type=F32)
    up = jnp.dot(u, wu_ref[...], preferred_element_type=F32)
    hidden = ((g * _sigmoid(g)) * up).astype(BF16)
    for c0 in range(0, o_ref.shape[1], DOWN_CHUNK):
        cols = slice(c0, c0 + DOWN_CHUNK)
        o_ref[:, cols] += jnp.dot(hidden, wd_ref[:, cols], preferred_element_type=F32)

    @pl.when(j == pl.num_programs(1) - 1)
    def _():
        h_copy.wait()
        for r0 in range(0, tm, NORM_ROWS):
            rows = slice(r0, r0 + NORM_ROWS)
            o_ref[rows, :] = h_buf[rows, :] + _rms_norm(o_ref[rows, :], npost_ref[...])


def _ffn(u2, h1, w_gate, w_up, w_down, n_post):
    t, d = h1.shape
    f = w_gate.shape[1]
    tm, tf = TM_FFN, TF_FFN
    return pl.pallas_call(
        _ffn_kernel,
        grid=(t // tm, f // tf),
        in_specs=[
            pl.BlockSpec((tm, d), lambda i, j: (i, 0)),
            pl.BlockSpec(memory_space=pl.ANY),
            pl.BlockSpec((d, tf), lambda i, j: (0, j)),
            pl.BlockSpec((d, tf), lambda i, j: (0, j)),
            pl.BlockSpec((tf, d), lambda i, j: (j, 0)),
            pl.BlockSpec((1, d), lambda i, j: (0, 0)),
        ],
        out_specs=pl.BlockSpec((tm, d), lambda i, j: (i, 0)),
        out_shape=jax.ShapeDtypeStruct((t, d), F32),
        scratch_shapes=[pltpu.VMEM((tm, d), F32), pltpu.SemaphoreType.DMA(())],
        compiler_params=_params(("arbitrary", "arbitrary")),
        name="swiglu_ffn",
    )(u2, h1, w_gate, w_up, w_down, n_post)


def kernel(x, norm_mix_pre, norm_mix_post, w_in, b_forget, w_branch_sb, w_branch_fox, w_out,
           norm_ffn_pre, norm_ffn_post, w_ffn_gate, w_ffn_up, w_ffn_down):
    batch, seq, d = x.shape
    depth = w_in.shape[0]
    n_qkv = 6 * D_BRANCH
    assert n_qkv % LANES == 0 and seq % TM_PROJ == 0
    h = x.reshape(batch * seq, d)
    for l in range(depth):
        w_t = jnp.swapaxes(w_in[l], 0, 1)
        wq_t = jnp.pad(w_t[:n_qkv].astype(BF16), ((0, 0), (0, PITCH_PAD)))
        b_f = jnp.pad(b_forget[l], (0, LANES - N_HEADS)).reshape(1, LANES)
        g_pre = norm_mix_pre[l].reshape(1, d)

        q_scale = jnp.full((D_BRANCH,), ATTN_SCALE * LOG2E, F32)
        ones = jnp.ones((D_BRANCH,), F32)
        col_scale = jnp.concatenate([-q_scale, ones, ones, q_scale, ones, ones]).reshape(1, n_qkv)
        qkv, u1, wg_bf, wu_bf, wd_bf = _qkv_proj(
            h, g_pre, wq_t, col_scale, [w_ffn_gate[l], w_ffn_up[l], w_ffn_down[l]])
        gates, cum_col, cum_row, wo_bf, wsb_bf, wfx_bf = _gates_proj(
            u1, w_t, n_qkv, 2 * d, b_f, seq, [w_out[l], w_branch_sb[l], w_branch_fox[l]],
            [0, PITCH_PAD, PITCH_PAD])
        o_sb, o_fx = _attention(qkv.reshape(batch, seq, n_qkv),
                                cum_col.reshape(batch, seq, LANES), cum_row, batch, seq)
        merged = _merge(o_sb.reshape(batch * seq, D_BRANCH), o_fx.reshape(batch * seq, D_BRANCH),
                        gates, wsb_bf, wfx_bf)
        h1, u2 = _out_proj(merged, h, wo_bf,
                           norm_mix_post[l].reshape(1, d), norm_ffn_pre[l].reshape(1, d))
        h = _ffn(u2, h1, wg_bf, wu_bf, wd_bf, norm_ffn_post[l].reshape(1, d))
    return h.reshape(batch, seq, d)
```

```python
import functools

import jax
import jax.numpy as jnp
from jax import lax
from jax.experimental import pallas as pl
from jax.experimental.pallas import tpu as pltpu

F32 = jnp.float32
BF16 = jnp.bfloat16

HEAD_DIM = 128
N_HEADS = 8
D_BRANCH = N_HEADS * HEAD_DIM
RMS_EPS = 1e-6
ATTN_SCALE = HEAD_DIM ** -0.5
NEG_BIG = -1e30
LOG2E = 1.4426950408889634
MAX_NEG_LOGIT = 126.0

LANES = 128
BF16_SUBLANES = 16
VMEM_LIMIT = 56 * 1024 * 1024

TM_PROJ = 1024
TN_PROJ = 1024
CUM_CHUNK = 256
TK = 256
RC = 256
QC = 256
SB_LOOKAHEAD_QK = 3
SB_LOOKAHEAD_CUM = 2
FOX_LOOKAHEAD = 3
TM_MERGE = 1024
TN_MERGE = 1024
TM_OUT = 512
OUT_ROWS = 256
TM_FFN = 1024
TF_FFN = 512
DOWN_CHUNK = 512
NORM_ROWS = 256


def _params(semantics):
    return pltpu.CompilerParams(dimension_semantics=semantics, vmem_limit_bytes=VMEM_LIMIT)


def _rms_norm(x, g):
    ms = jnp.mean(x * x, axis=-1, keepdims=True)
    return (x * lax.rsqrt(ms + RMS_EPS)) * g


def _log_sigmoid(x):
    return jnp.minimum(x, 0.0) - jnp.log1p(jnp.exp(-jnp.abs(x)))


def _sigmoid(x):
    return 1.0 / (1.0 + jnp.exp(-x))


def _dot_nt(a, b):
    return lax.dot_general(a, b, (((1,), (1,)), ((), ())), preferred_element_type=F32)


def _split3(x):
    hi = x.astype(BF16)
    r1 = x - hi.astype(F32)
    mid = r1.astype(BF16)
    lo = (r1 - mid.astype(F32)).astype(BF16)
    return hi, mid, lo


def _rider_plan(weights, steps_each):
    plan, s0 = [], 0
    for w, c in zip(weights, steps_each):
        assert w.shape[0] % c == 0 and (w.shape[0] // c) % BF16_SUBLANES == 0
        plan.append((w, s0, c))
        s0 += c
    return plan


def _rider_specs(plan, n_j):
    specs, shapes = [], []
    for w, s0, c in plan:
        def index(i, j, s0=s0, c=c):
            return (jnp.clip(i * n_j + j - s0, 0, c - 1), 0)
        specs.append(pl.BlockSpec((w.shape[0] // c, w.shape[1]), index))
        shapes.append(jax.ShapeDtypeStruct(w.shape, BF16))
    return specs, shapes


def _run_riders(step, windows, src_refs, dst_refs):
    for (s0, c), src, dst in zip(windows, src_refs, dst_refs):
        @pl.when((step >= s0) & (step < s0 + c))
        def _(src=src, dst=dst):
            dst[...] = src[...].astype(BF16)


def _qkv_kernel(x_ref, g_ref, w_ref, cs_ref, *refs, windows):
    k = len(windows)
    src_refs, (o_ref, u_ref), dst_refs = refs[:k], refs[k:k + 2], refs[k + 2:]
    _run_riders(pl.program_id(0) * pl.num_programs(1) + pl.program_id(1), windows,
                src_refs, dst_refs)

    @pl.when(pl.program_id(1) == 0)
    def _():
        u_ref[...] = _rms_norm(x_ref[...], g_ref[...]).astype(BF16)

    acc = _dot_nt(u_ref[...], w_ref[...])
    o_ref[...] = (acc * cs_ref[...]).astype(o_ref.dtype)


def _qkv_proj(x2, g, w_t, col_scale, ride):
    t, d = x2.shape
    n = col_scale.shape[1]
    n_i, n_j = t // TM_PROJ, n // TN_PROJ
    plan = _rider_plan(ride, [n_i * n_j // len(ride)] * len(ride))
    ride_specs, ride_shapes = _rider_specs(plan, n_j)
    return pl.pallas_call(
        functools.partial(_qkv_kernel, windows=[(s0, c) for _, s0, c in plan]),
        grid=(n_i, n_j),
        in_specs=[
            pl.BlockSpec((TM_PROJ, d), lambda i, j: (i, 0)),
            pl.BlockSpec((1, d), lambda i, j: (0, 0)),
            pl.BlockSpec((TN_PROJ, d), lambda i, j: (j, 0)),
            pl.BlockSpec((1, TN_PROJ), lambda i, j: (0, j)),
        ] + ride_specs,
        out_specs=[
            pl.BlockSpec((TM_PROJ, TN_PROJ), lambda i, j: (i, j)),
            pl.BlockSpec((TM_PROJ, d), lambda i, j: (i, 0)),
        ] + ride_specs,
        out_shape=[
            jax.ShapeDtypeStruct((t, n), BF16),
            jax.ShapeDtypeStruct((t, d), BF16),
        ] + ride_shapes,
        compiler_params=_params(("arbitrary", "arbitrary")),
        name="qkv_proj",
    )(x2, g, w_t, col_scale, *ride)


def _gates_kernel(u_ref, wt_hbm_ref, wf_ref, bf_ref, *refs, tiles_per_seq, windows, row0):
    k = len(windows)
    src_refs, (o_ref, cumc_ref, cumr_ref), dst_refs = refs[:k], refs[k:k + 3], refs[k + 3:-3]
    carry_ref, w_buf, w_sem = refs[-3:]
    i, j = pl.program_id(0), pl.program_id(1)
    n_j = pl.num_programs(1)
    step = i * n_j + j
    _run_riders(step, windows, src_refs, dst_refs)

    def slab_copy(jj, slot):
        rows = pl.ds(row0 + jj * TN_PROJ, TN_PROJ)
        return pltpu.make_async_copy(wt_hbm_ref.at[rows, :], w_buf.at[slot], w_sem.at[slot])

    slot = step % 2

    @pl.when(step == 0)
    def _():
        slab_copy(0, 0).start()

    @pl.when(step + 1 < pl.num_programs(0) * n_j)
    def _():
        slab_copy((j + 1) % n_j, 1 - slot).start()

    @pl.when(pl.program_id(1) == 0)
    def _():
        f = _dot_nt(u_ref[...], wf_ref[...].astype(BF16))
        log_f = _log_sigmoid(f + bf_ref[...])

        @pl.when(i % tiles_per_seq == 0)
        def _():
            carry_ref[...] = jnp.zeros_like(carry_ref)

        r = lax.broadcasted_iota(jnp.int32, (CUM_CHUNK, CUM_CHUNK), 0)
        c = lax.broadcasted_iota(jnp.int32, (CUM_CHUNK, CUM_CHUNK), 1)
        lower = (c <= r).astype(BF16)
        carry = carry_ref[...]
        for ch in range(TM_PROJ // CUM_CHUNK):
            blk = log_f[ch * CUM_CHUNK:(ch + 1) * CUM_CHUNK]
            hi, mid, lo = _split3(blk)
            cs = (jnp.dot(lower, hi, preferred_element_type=F32)
                  + jnp.dot(lower, mid, preferred_element_type=F32)
                  + jnp.dot(lower, lo, preferred_element_type=F32)) + carry
            cumc_ref[ch * CUM_CHUNK:(ch + 1) * CUM_CHUNK, :] = cs
            carry = cs[CUM_CHUNK - 1:CUM_CHUNK, :]
        carry_ref[...] = carry
        cumr_ref[0] = cumc_ref[...].T[:N_HEADS, :]

    slab_copy(j, slot).wait()
    acc = _dot_nt(u_ref[...], w_buf[slot].astype(BF16))
    o_ref[...] = _sigmoid(acc).astype(o_ref.dtype)


def _gates_proj(u, w_t, f_row0, n, bf, seq, ride):
    t, d = u.shape
    assert f_row0 % LANES == 0
    tiles_per_seq = seq // TM_PROJ
    n_i, n_j = t // TM_PROJ, n // TN_PROJ
    ride_rows = sum(r.shape[0] for r in ride)
    plan = _rider_plan(ride, [n_i * n_j * r.shape[0] // ride_rows for r in ride])
    ride_specs, ride_shapes = _rider_specs(plan, n_j)
    return pl.pallas_call(
        functools.partial(_gates_kernel, tiles_per_seq=tiles_per_seq,
                          windows=[(s0, c) for _, s0, c in plan], row0=f_row0 + N_HEADS),
        grid=(n_i, n_j),
        in_specs=[
            pl.BlockSpec((TM_PROJ, d), lambda i, j: (i, 0)),
            pl.BlockSpec(memory_space=pl.ANY),
            pl.BlockSpec((LANES, d), lambda i, j: (f_row0 // LANES, 0)),
            pl.BlockSpec((1, LANES), lambda i, j: (0, 0)),
        ] + ride_specs,
        out_specs=[
            pl.BlockSpec((TM_PROJ, TN_PROJ), lambda i, j: (i, j)),
            pl.BlockSpec((TM_PROJ, LANES), lambda i, j: (i, 0)),
            pl.BlockSpec((1, N_HEADS, TM_PROJ),
                         lambda i, j: (i // tiles_per_seq, 0, i % tiles_per_seq)),
        ] + ride_specs,
        out_shape=[
            jax.ShapeDtypeStruct((t, n), BF16),
            jax.ShapeDtypeStruct((t, LANES), F32),
            jax.ShapeDtypeStruct((t // seq, N_HEADS, seq), F32),
        ] + ride_shapes,
        scratch_shapes=[pltpu.VMEM((1, LANES), F32), pltpu.VMEM((2, TN_PROJ, d), F32),
                        pltpu.SemaphoreType.DMA((2,))],
        compiler_params=_params(("arbitrary", "arbitrary")),
        name="gates_proj",
    )(u, w_t, w_t, bf, *ride)


def _sb_kernel(q_ref, k_ref, v_ref, o_ref, *, seq):
    r = lax.broadcasted_iota(jnp.int32, (TK, TK), 0)
    c = lax.broadcasted_iota(jnp.int32, (TK, TK), 1)
    later = (r > c).astype(BF16)

    def valid_mask(d, c):
        k0, r0 = d * TK, c * RC
        if k0 + TK <= r0:
            return None
        return (k0 + lax.broadcasted_iota(jnp.int32, (RC, TK), 1)
                < r0 + lax.broadcasted_iota(jnp.int32, (RC, TK), 0))

    def gates(d, c):
        q = q_ref[0, c * RC:(c + 1) * RC, :]
        k = k_ref[0, d * TK:(d + 1) * TK, :]
        nz = jnp.minimum(_dot_nt(q, k), MAX_NEG_LOGIT)
        neg_log_beta = jnp.log2(1.0 + jnp.exp2(nz))
        log_keep = nz - neg_log_beta
        valid = valid_mask(d, c)
        if valid is not None:
            log_keep = jnp.where(valid, log_keep, 0.0)
        return neg_log_beta, log_keep.astype(BF16), jnp.sum(log_keep, axis=1, keepdims=True)

    def within_block(neg_log_beta, keep_bf, row_sum):
        return jnp.dot(keep_bf, later, preferred_element_type=F32) - neg_log_beta, row_sum

    def accumulate(d, c, log_w, row_sum, carry, acc):
        v = v_ref[0, d * TK:(d + 1) * TK, :]
        w = jnp.exp2(log_w + carry)
        valid = valid_mask(d, c)
        if valid is not None:
            w = jnp.where(valid, w, 0.0)
        return carry + row_sum, acc + jnp.dot(w.astype(BF16), v, preferred_element_type=F32)

    n_chunks = seq // RC
    state = [(jnp.zeros((RC, 1), F32), jnp.zeros((RC, HEAD_DIM), F32)) for _ in range(n_chunks)]
    items = [(d, c) for d in reversed(range(seq // TK)) for c in range(d * TK // RC, n_chunks)]
    n = len(items)
    staged_a, staged_b = {}, {}
    for step in range(-SB_LOOKAHEAD_QK, n):
        i_a, i_b = step + SB_LOOKAHEAD_QK, step + SB_LOOKAHEAD_CUM
        if i_a < n:
            staged_a[i_a] = gates(*items[i_a])
        if 0 <= i_b < n:
            staged_b[i_b] = within_block(*staged_a.pop(i_b))
        if step >= 0:
            d, c = items[step]
            state[c] = accumulate(d, c, *staged_b.pop(step), *state[c])
    for c in range(n_chunks):
        o_ref[0, c * RC:(c + 1) * RC, :] = state[c][1].astype(o_ref.dtype)


def _fox_kernel(q_ref, k_ref, v_ref, cq_ref, ck_ref, o_ref, *, seq):
    h = pl.program_id(1)
    lane = lax.broadcasted_iota(jnp.int32, (TK, LANES), 1)
    ones_col = (lane == 0).astype(BF16)
    n_blocks = seq // TK
    ck_blocks = [jnp.sum(jnp.where(lane == h, ck_ref[0, d * TK:(d + 1) * TK, :], 0.0),
                         axis=1, keepdims=True) * LOG2E for d in range(n_blocks)]
    v_blocks = [jnp.concatenate([v_ref[0, d * TK:(d + 1) * TK, :], ones_col], axis=1)
                for d in range(n_blocks)]

    def scores(d, c):
        k0, c0 = d * TK, c * QC
        q = q_ref[0, c0:c0 + QC, :]
        k = k_ref[0, k0:k0 + TK, :]
        y = _dot_nt(k, q) - ck_blocks[d]
        if k0 + TK > c0:
            valid = (k0 + lax.broadcasted_iota(jnp.int32, (TK, QC), 0)
                     <= c0 + lax.broadcasted_iota(jnp.int32, (TK, QC), 1))
            y = jnp.where(valid, y, NEG_BIG)
        return y

    def accumulate(d, c, y, m, l, acc):
        c0 = c * QC
        cq = cq_ref[0, pl.ds(h, 1), c0:c0 + QC] * LOG2E
        m_new = jnp.maximum(m, cq + jnp.max(y, axis=0, keepdims=True))
        alpha = jnp.exp2(m - m_new)
        p = jnp.exp2(y - (m_new - cq))
        pv = lax.dot_general(v_blocks[d], p.astype(BF16), (((0,), (0,)), ((), ())),
                             preferred_element_type=F32)
        l = alpha * l + pv[HEAD_DIM:HEAD_DIM + 1]
        return m_new, l, alpha * acc + pv[:HEAD_DIM]

    n_chunks = seq // QC
    state = [(jnp.full((1, QC), NEG_BIG, F32), jnp.zeros((1, QC), F32),
              jnp.zeros((HEAD_DIM, QC), F32)) for _ in range(n_chunks)]
    items = [(d, c) for d in range(seq // TK) for c in range(d * TK // QC, n_chunks)]
    ys = [scores(*item) for item in items[:FOX_LOOKAHEAD]]
    for i, (d, c) in enumerate(items):
        if i + FOX_LOOKAHEAD < len(items):
            ys.append(scores(*items[i + FOX_LOOKAHEAD]))
        state[c] = accumulate(d, c, ys.pop(0), *state[c])
    for c in range(n_chunks):
        _, l, acc = state[c]
        o_ref[0, c * QC:(c + 1) * QC, :] = (acc / l).T.astype(o_ref.dtype)


def _attention(qkv, cum_col, cum_row, batch, seq):
    grid = (batch, N_HEADS)

    def head_spec(head0):
        return pl.BlockSpec((1, seq, HEAD_DIM), lambda b, h: (b, 0, head0 + h))

    out_shape = jax.ShapeDtypeStruct((batch, seq, D_BRANCH), BF16)
    sem = ("arbitrary", "arbitrary")

    o_sb = pl.pallas_call(
        functools.partial(_sb_kernel, seq=seq),
        grid=grid,
        in_specs=[head_spec(0), head_spec(N_HEADS), head_spec(2 * N_HEADS)],
        out_specs=head_spec(0),
        out_shape=out_shape,
        compiler_params=_params(sem),
        name="stickbreak_attn",
    )(qkv, qkv, qkv)

    o_fx = pl.pallas_call(
        functools.partial(_fox_kernel, seq=seq),
        grid=grid,
        in_specs=[
            head_spec(3 * N_HEADS), head_spec(4 * N_HEADS), head_spec(5 * N_HEADS),
            pl.BlockSpec((1, N_HEADS, seq), lambda b, h: (b, 0, 0)),
            pl.BlockSpec((1, seq, LANES), lambda b, h: (b, 0, 0)),
        ],
        out_specs=head_spec(0),
        out_shape=out_shape,
        compiler_params=_params(sem),
        name="forgetting_attn",
    )(qkv, qkv, qkv, cum_row, cum_col)
    return o_sb, o_fx


def _merge_kernel(osb_ref, ofx_ref, gsb_ref, gfx_ref, wsb_ref, wfx_ref, o_ref):
    a = jnp.dot(osb_ref[...], wsb_ref[...], preferred_element_type=F32)
    b = jnp.dot(ofx_ref[...], wfx_ref[...], preferred_element_type=F32)
    o_ref[...] = (gsb_ref[...].astype(F32) * a + gfx_ref[...].astype(F32) * b).astype(o_ref.dtype)


def _merge(o_sb, o_fx, gates, w_sb, w_fx):
    t, k = o_sb.shape
    d = w_sb.shape[1]
    tm, tn = TM_MERGE, TN_MERGE
    n_j = d // tn
    return pl.pallas_call(
        _merge_kernel,
        grid=(t // tm, n_j),
        in_specs=[
            pl.BlockSpec((tm, k), lambda i, j: (i, 0)),
            pl.BlockSpec((tm, k), lambda i, j: (i, 0)),
            pl.BlockSpec((tm, tn), lambda i, j: (i, j)),
            pl.BlockSpec((tm, tn), lambda i, j: (i, n_j + j)),
            pl.BlockSpec((k, tn), lambda i, j: (0, j)),
            pl.BlockSpec((k, tn), lambda i, j: (0, j)),
        ],
        out_specs=pl.BlockSpec((tm, tn), lambda i, j: (i, j)),
        out_shape=jax.ShapeDtypeStruct((t, d), BF16),
        compiler_params=_params(("arbitrary", "arbitrary")),
        name="gated_merge",
    )(o_sb, o_fx, gates, gates, w_sb, w_fx)


def _out_proj_kernel(m_ref, x_ref, wout_ref, npost_ref, npre_ref, h_ref, u_ref):
    tm = m_ref.shape[0]
    chunks = [slice(r0, r0 + OUT_ROWS) for r0 in range(0, tm, OUT_ROWS)]
    mixes = [jnp.dot(m_ref[rows, :], wout_ref[...], preferred_element_type=F32)
             for rows in chunks]
    for rows, mix in zip(chunks, mixes):
        h = x_ref[rows, :] + _rms_norm(mix, npost_ref[...])
        h_ref[rows, :] = h
        u_ref[rows, :] = _rms_norm(h, npre_ref[...]).astype(BF16)


def _const_spec(shape):
    return pl.BlockSpec(shape, lambda i: (0,) * len(shape), pipeline_mode=pl.Buffered(1))


def _out_proj(merged, x2, w_out, n_post, n_ffn_pre):
    t, d = x2.shape
    tm = TM_OUT
    return pl.pallas_call(
        _out_proj_kernel,
        grid=(t // tm,),
        in_specs=[
            pl.BlockSpec((tm, d), lambda i: (i, 0)),
            pl.BlockSpec((tm, d), lambda i: (i, 0)),
            _const_spec((d, d)),
            _const_spec((1, d)),
            _const_spec((1, d)),
        ],
        out_specs=[
            pl.BlockSpec((tm, d), lambda i: (i, 0)),
            pl.BlockSpec((tm, d), lambda i: (i, 0)),
        ],
        out_shape=[
            jax.ShapeDtypeStruct((t, d), F32),
            jax.ShapeDtypeStruct((t, d), BF16),
        ],
        compiler_params=_params(("arbitrary",)),
        name="out_proj",
    )(merged, x2, w_out, n_post, n_ffn_pre)


def _ffn_kernel(u_ref, h_hbm_ref, wg_ref, wu_ref, wd_ref, npost_ref, o_ref, h_buf, h_sem):
    i, j = pl.program_id(0), pl.program_id(1)
    tm = o_ref.shape[0]
    h_copy = pltpu.make_async_copy(h_hbm_ref.at[pl.ds(i * tm, tm), :], h_buf, h_sem)

    @pl.when(j == 0)
    def _():
        h_copy.start()
        o_ref[...] = jnp.zeros_like(o_ref)

    u = u_ref[...]
    g = jnp.dot(u, wg_ref[...], preferred_element_type=F32)
    up = jnp.dot(u, wu_ref[...], preferred_element_type=F32)
    hidden = ((g * _sigmoid(g)) * up).astype(BF16)
    for c0 in range(0, o_ref.shape[1], DOWN_CHUNK):
        cols = slice(c0, c0 + DOWN_CHUNK)
        o_ref[:, cols] += jnp.dot(hidden, wd_ref[:, cols], preferred_element_type=F32)

    @pl.when(j == pl.num_programs(1) - 1)
    def _():
        h_copy.wait()
        for r0 in range(0, tm, NORM_ROWS):
            rows = slice(r0, r0 + NORM_ROWS)
            o_ref[rows, :] = h_buf[rows, :] + _rms_norm(o_ref[rows, :], npost_ref[...])


def _ffn(u2, h1, w_gate, w_up, w_down, n_post):
    t, d = h1.shape
    f = w_gate.shape[1]
    tm, tf = TM_FFN, TF_FFN
    return pl.pallas_call(
        _ffn_kernel,
        grid=(t // tm, f // tf),
        in_specs=[
            pl.BlockSpec((tm, d), lambda i, j: (i, 0)),
            pl.BlockSpec(memory_space=pl.ANY),
            pl.BlockSpec((d, tf), lambda i, j: (0, j)),
            pl.BlockSpec((d, tf), lambda i, j: (0, j)),
            pl.BlockSpec((tf, d), lambda i, j: (j, 0)),
            pl.BlockSpec((1, d), lambda i, j: (0, 0)),
        ],
        out_specs=pl.BlockSpec((tm, d), lambda i, j: (i, 0)),
        out_shape=jax.ShapeDtypeStruct((t, d), F32),
        scratch_shapes=[pltpu.VMEM((tm, d), F32), pltpu.SemaphoreType.DMA(())],
        compiler_params=_params(("arbitrary", "arbitrary")),
        name="swiglu_ffn",
    )(u2, h1, w_gate, w_up, w_down, n_post)


def kernel(x, norm_mix_pre, norm_mix_post, w_in, b_forget, w_branch_sb, w_branch_fox, w_out,
           norm_ffn_pre, norm_ffn_post, w_ffn_gate, w_ffn_up, w_ffn_down):
    batch, seq, d = x.shape
    depth = w_in.shape[0]
    n_qkv = 6 * D_BRANCH
    assert n_qkv % LANES == 0 and seq % TM_PROJ == 0
    h = x.reshape(batch * seq, d)
    for l in range(depth):
        w_t = jnp.swapaxes(w_in[l], 0, 1)
        wq_t = w_t[:n_qkv].astype(BF16)
        b_f = jnp.pad(b_forget[l], (0, LANES - N_HEADS)).reshape(1, LANES)
        g_pre = norm_mix_pre[l].reshape(1, d)

        q_scale = jnp.full((D_BRANCH,), ATTN_SCALE * LOG2E, F32)
        ones = jnp.ones((D_BRANCH,), F32)
        col_scale = jnp.concatenate([-q_scale, ones, ones, q_scale, ones, ones]).reshape(1, n_qkv)
        qkv, u1, wg_bf, wu_bf, wd_bf = _qkv_proj(
            h, g_pre, wq_t, col_scale, [w_ffn_gate[l], w_ffn_up[l], w_ffn_down[l]])
        gates, cum_col, cum_row, wo_bf, wsb_bf, wfx_bf = _gates_proj(
            u1, w_t, n_qkv, 2 * d, b_f, seq, [w_out[l], w_branch_sb[l], w_branch_fox[l]])
        o_sb, o_fx = _attention(qkv.reshape(batch, seq, n_qkv),
                                cum_col.reshape(batch, seq, LANES), cum_row, batch, seq)
        merged = _merge(o_sb.reshape(batch * seq, D_BRANCH), o_fx.reshape(batch * seq, D_BRANCH),
                        gates, wsb_bf, wfx_bf)
        h1, u2 = _out_proj(merged, h, wo_bf,
                           norm_mix_post[l].reshape(1, d), norm_ffn_pre[l].reshape(1, d))
        h = _ffn(u2, h1, wg_bf, wu_bf, wd_bf, norm_ffn_post[l].reshape(1, d))
    return h.reshape(batch, seq, d)
```

```python
import functools

import jax
import jax.numpy as jnp
from jax import lax
from jax.experimental import pallas as pl
from jax.experimental.pallas import tpu as pltpu

F32 = jnp.float32
BF16 = jnp.bfloat16

HEAD_DIM = 128
N_HEADS = 8
D_BRANCH = N_HEADS * HEAD_DIM
RMS_EPS = 1e-6
ATTN_SCALE = HEAD_DIM ** -0.5
NEG_BIG = -1e30
LOG2E = 1.4426950408889634
MAX_NEG_LOGIT = 126.0

LANES = 128
BF16_SUBLANES = 16
VMEM_LIMIT = 56 * 1024 * 1024

UNT_COLS = 256
TM_PROJ = 1024
TN_PROJ = 1024
CUM_CHUNK = 256
TK = 256
RC = 256
QC = 256
SB_LOOKAHEAD_QK = 3
SB_LOOKAHEAD_CUM = 2
FOX_LOOKAHEAD = 3
TM_MERGE = 1024
TN_MERGE = 1024
TM_OUT = 512
OUT_ROWS = 256
TM_FFN = 1024
TF_FFN = 512
DOWN_CHUNK = 512
NORM_ROWS = 256


def _params(semantics):
    return pltpu.CompilerParams(dimension_semantics=semantics, vmem_limit_bytes=VMEM_LIMIT)


def _rms_norm(x, g):
    ms = jnp.mean(x * x, axis=-1, keepdims=True)
    return (x * lax.rsqrt(ms + RMS_EPS)) * g


def _log_sigmoid(x):
    return jnp.minimum(x, 0.0) - jnp.log1p(jnp.exp(-jnp.abs(x)))


def _sigmoid(x):
    return 1.0 / (1.0 + jnp.exp(-x))


def _dot_nt(a, b):
    return lax.dot_general(a, b, (((1,), (1,)), ((), ())), preferred_element_type=F32)


def _split3(x):
    hi = x.astype(BF16)
    r1 = x - hi.astype(F32)
    mid = r1.astype(BF16)
    lo = (r1 - mid.astype(F32)).astype(BF16)
    return hi, mid, lo


def _rider_plan(weights, steps_each):
    plan, s0 = [], 0
    for w, c in zip(weights, steps_each):
        assert w.shape[0] % c == 0 and (w.shape[0] // c) % BF16_SUBLANES == 0
        plan.append((w, s0, c))
        s0 += c
    return plan


def _rider_specs(plan, n_j):
    specs, shapes = [], []
    for w, s0, c in plan:
        def index(i, j, s0=s0, c=c):
            return (jnp.clip(i * n_j + j - s0, 0, c - 1), 0)
        specs.append(pl.BlockSpec((w.shape[0] // c, w.shape[1]), index))
        shapes.append(jax.ShapeDtypeStruct(w.shape, BF16))
    return specs, shapes


def _run_riders(step, windows, src_refs, dst_refs):
    for (s0, c), src, dst in zip(windows, src_refs, dst_refs):
        @pl.when((step >= s0) & (step < s0 + c))
        def _(src=src, dst=dst):
            dst[...] = src[...].astype(BF16)


def _qkv_kernel(x_ref, g_ref, w_ref, cs_ref, *refs, windows):
    k = len(windows)
    src_refs, (o_ref, u_ref), dst_refs = refs[:k], refs[k:k + 2], refs[k + 2:]
    _run_riders(pl.program_id(0) * pl.num_programs(1) + pl.program_id(1), windows,
                src_refs, dst_refs)

    @pl.when(pl.program_id(1) == 0)
    def _():
        u_ref[...] = _rms_norm(x_ref[...], g_ref[...]).astype(BF16)

    acc = jnp.dot(u_ref[...], w_ref[...], preferred_element_type=F32)
    o_ref[...] = (acc * cs_ref[...]).astype(o_ref.dtype)


def _untranspose_kernel(src_ref, dst_ref, *, n_blocks):
    @pl.when(pl.program_id(0) < n_blocks)
    def _():
        dst_ref[...] = src_ref[...].T.astype(BF16)

    @pl.when(pl.program_id(0) == n_blocks)
    def _():
        dst_ref[...] = jnp.zeros_like(dst_ref)


def _untranspose_cast(w_t, n):
    d = w_t.shape[1]
    n_blocks = n // UNT_COLS
    return pl.pallas_call(
        functools.partial(_untranspose_kernel, n_blocks=n_blocks),
        grid=(n_blocks + 1,),
        in_specs=[pl.BlockSpec((UNT_COLS, d), lambda s: (jnp.minimum(s, n_blocks - 1), 0))],
        out_specs=pl.BlockSpec((d, UNT_COLS), lambda s: (0, s)),
        out_shape=jax.ShapeDtypeStruct((d, n + UNT_COLS), BF16),
        compiler_params=_params(("arbitrary",)),
        name="untranspose_cast",
    )(w_t)


def _qkv_proj(x2, g, w, col_scale, ride):
    t, d = x2.shape
    n = col_scale.shape[1]
    n_i, n_j = t // TM_PROJ, n // TN_PROJ
    plan = _rider_plan(ride, [n_i * n_j // len(ride)] * len(ride))
    ride_specs, ride_shapes = _rider_specs(plan, n_j)
    return pl.pallas_call(
        functools.partial(_qkv_kernel, windows=[(s0, c) for _, s0, c in plan]),
        grid=(n_i, n_j),
        in_specs=[
            pl.BlockSpec((TM_PROJ, d), lambda i, j: (i, 0)),
            pl.BlockSpec((1, d), lambda i, j: (0, 0)),
            pl.BlockSpec((d, TN_PROJ), lambda i, j: (0, j)),
            pl.BlockSpec((1, TN_PROJ), lambda i, j: (0, j)),
        ] + ride_specs,
        out_specs=[
            pl.BlockSpec((TM_PROJ, TN_PROJ), lambda i, j: (i, j)),
            pl.BlockSpec((TM_PROJ, d), lambda i, j: (i, 0)),
        ] + ride_specs,
        out_shape=[
            jax.ShapeDtypeStruct((t, n), BF16),
            jax.ShapeDtypeStruct((t, d), BF16),
        ] + ride_shapes,
        compiler_params=_params(("arbitrary", "arbitrary")),
        name="qkv_proj",
    )(x2, g, w, col_scale, *ride)


def _gates_kernel(u_ref, wt_hbm_ref, wf_ref, bf_ref, *refs, tiles_per_seq, windows, row0):
    k = len(windows)
    src_refs, (o_ref, cumc_ref, cumr_ref), dst_refs = refs[:k], refs[k:k + 3], refs[k + 3:-3]
    carry_ref, w_buf, w_sem = refs[-3:]
    i, j = pl.program_id(0), pl.program_id(1)
    n_j = pl.num_programs(1)
    step = i * n_j + j
    _run_riders(step, windows, src_refs, dst_refs)

    def slab_copy(jj, slot):
        rows = pl.ds(row0 + jj * TN_PROJ, TN_PROJ)
        return pltpu.make_async_copy(wt_hbm_ref.at[rows, :], w_buf.at[slot], w_sem.at[slot])

    slot = step % 2

    @pl.when(step == 0)
    def _():
        slab_copy(0, 0).start()

    @pl.when(step + 1 < pl.num_programs(0) * n_j)
    def _():
        slab_copy((j + 1) % n_j, 1 - slot).start()

    @pl.when(pl.program_id(1) == 0)
    def _():
        f = _dot_nt(u_ref[...], wf_ref[...].astype(BF16))
        log_f = _log_sigmoid(f + bf_ref[...])

        @pl.when(i % tiles_per_seq == 0)
        def _():
            carry_ref[...] = jnp.zeros_like(carry_ref)

        r = lax.broadcasted_iota(jnp.int32, (CUM_CHUNK, CUM_CHUNK), 0)
        c = lax.broadcasted_iota(jnp.int32, (CUM_CHUNK, CUM_CHUNK), 1)
        lower = (c <= r).astype(BF16)
        carry = carry_ref[...]
        for ch in range(TM_PROJ // CUM_CHUNK):
            blk = log_f[ch * CUM_CHUNK:(ch + 1) * CUM_CHUNK]
            hi, mid, lo = _split3(blk)
            cs = (jnp.dot(lower, hi, preferred_element_type=F32)
                  + jnp.dot(lower, mid, preferred_element_type=F32)
                  + jnp.dot(lower, lo, preferred_element_type=F32)) + carry
            cumc_ref[ch * CUM_CHUNK:(ch + 1) * CUM_CHUNK, :] = cs
            carry = cs[CUM_CHUNK - 1:CUM_CHUNK, :]
        carry_ref[...] = carry
        cumr_ref[0] = cumc_ref[...].T[:N_HEADS, :]

    slab_copy(j, slot).wait()
    acc = _dot_nt(u_ref[...], w_buf[slot].astype(BF16))
    o_ref[...] = _sigmoid(acc).astype(o_ref.dtype)


def _gates_proj(u, w_t, f_row0, n, bf, seq, ride):
    t, d = u.shape
    assert f_row0 % LANES == 0
    tiles_per_seq = seq // TM_PROJ
    n_i, n_j = t // TM_PROJ, n // TN_PROJ
    ride_rows = sum(r.shape[0] for r in ride)
    plan = _rider_plan(ride, [n_i * n_j * r.shape[0] // ride_rows for r in ride])
    ride_specs, ride_shapes = _rider_specs(plan, n_j)
    return pl.pallas_call(
        functools.partial(_gates_kernel, tiles_per_seq=tiles_per_seq,
                          windows=[(s0, c) for _, s0, c in plan], row0=f_row0 + N_HEADS),
        grid=(n_i, n_j),
        in_specs=[
            pl.BlockSpec((TM_PROJ, d), lambda i, j: (i, 0)),
            pl.BlockSpec(memory_space=pl.ANY),
            pl.BlockSpec((LANES, d), lambda i, j: (f_row0 // LANES, 0)),
            pl.BlockSpec((1, LANES), lambda i, j: (0, 0)),
        ] + ride_specs,
        out_specs=[
            pl.BlockSpec((TM_PROJ, TN_PROJ), lambda i, j: (i, j)),
            pl.BlockSpec((TM_PROJ, LANES), lambda i, j: (i, 0)),
            pl.BlockSpec((1, N_HEADS, TM_PROJ),
                         lambda i, j: (i // tiles_per_seq, 0, i % tiles_per_seq)),
        ] + ride_specs,
        out_shape=[
            jax.ShapeDtypeStruct((t, n), BF16),
            jax.ShapeDtypeStruct((t, LANES), F32),
            jax.ShapeDtypeStruct((t // seq, N_HEADS, seq), F32),
        ] + ride_shapes,
        scratch_shapes=[pltpu.VMEM((1, LANES), F32), pltpu.VMEM((2, TN_PROJ, d), F32),
                        pltpu.SemaphoreType.DMA((2,))],
        compiler_params=_params(("arbitrary", "arbitrary")),
        name="gates_proj",
    )(u, w_t, w_t, bf, *ride)


def _sb_kernel(q_ref, k_ref, v_ref, o_ref, *, seq):
    r = lax.broadcasted_iota(jnp.int32, (TK, TK), 0)
    c = lax.broadcasted_iota(jnp.int32, (TK, TK), 1)
    later = (r > c).astype(BF16)

    def valid_mask(d, c):
        k0, r0 = d * TK, c * RC
        if k0 + TK <= r0:
            return None
        return (k0 + lax.broadcasted_iota(jnp.int32, (RC, TK), 1)
                < r0 + lax.broadcasted_iota(jnp.int32, (RC, TK), 0))

    def gates(d, c):
        q = q_ref[0, c * RC:(c + 1) * RC, :]
        k = k_ref[0, d * TK:(d + 1) * TK, :]
        nz = jnp.minimum(_dot_nt(q, k), MAX_NEG_LOGIT)
        neg_log_beta = jnp.log2(1.0 + jnp.exp2(nz))
        log_keep = nz - neg_log_beta
        valid = valid_mask(d, c)
        if valid is not None:
            log_keep = jnp.where(valid, log_keep, 0.0)
        return neg_log_beta, log_keep.astype(BF16), jnp.sum(log_keep, axis=1, keepdims=True)

    def within_block(neg_log_beta, keep_bf, row_sum):
        return jnp.dot(keep_bf, later, preferred_element_type=F32) - neg_log_beta, row_sum

    def accumulate(d, c, log_w, row_sum, carry, acc):
        v = v_ref[0, d * TK:(d + 1) * TK, :]
        w = jnp.exp2(log_w + carry)
        valid = valid_mask(d, c)
        if valid is not None:
            w = jnp.where(valid, w, 0.0)
        return carry + row_sum, acc + jnp.dot(w.astype(BF16), v, preferred_element_type=F32)

    n_chunks = seq // RC
    state = [(jnp.zeros((RC, 1), F32), jnp.zeros((RC, HEAD_DIM), F32)) for _ in range(n_chunks)]
    items = [(d, c) for d in reversed(range(seq // TK)) for c in range(d * TK // RC, n_chunks)]
    n = len(items)
    staged_a, staged_b = {}, {}
    for step in range(-SB_LOOKAHEAD_QK, n):
        i_a, i_b = step + SB_LOOKAHEAD_QK, step + SB_LOOKAHEAD_CUM
        if i_a < n:
            staged_a[i_a] = gates(*items[i_a])
        if 0 <= i_b < n:
            staged_b[i_b] = within_block(*staged_a.pop(i_b))
        if step >= 0:
            d, c = items[step]
            state[c] = accumulate(d, c, *staged_b.pop(step), *state[c])
    for c in range(n_chunks):
        o_ref[0, c * RC:(c + 1) * RC, :] = state[c][1].astype(o_ref.dtype)


def _fox_kernel(q_ref, k_ref, v_ref, cq_ref, ck_ref, o_ref, *, seq):
    h = pl.program_id(1)
    lane = lax.broadcasted_iota(jnp.int32, (TK, LANES), 1)
    ones_col = (lane == 0).astype(BF16)
    n_blocks = seq // TK
    ck_blocks = [jnp.sum(jnp.where(lane == h, ck_ref[0, d * TK:(d + 1) * TK, :], 0.0),
                         axis=1, keepdims=True) * LOG2E for d in range(n_blocks)]
    v_blocks = [jnp.concatenate([v_ref[0, d * TK:(d + 1) * TK, :], ones_col], axis=1)
                for d in range(n_blocks)]

    def scores(d, c):
        k0, c0 = d * TK, c * QC
        q = q_ref[0, c0:c0 + QC, :]
        k = k_ref[0, k0:k0 + TK, :]
        y = _dot_nt(k, q) - ck_blocks[d]
        if k0 + TK > c0:
            valid = (k0 + lax.broadcasted_iota(jnp.int32, (TK, QC), 0)
                     <= c0 + lax.broadcasted_iota(jnp.int32, (TK, QC), 1))
            y = jnp.where(valid, y, NEG_BIG)
        return y

    def accumulate(d, c, y, m, l, acc):
        c0 = c * QC
        cq = cq_ref[0, pl.ds(h, 1), c0:c0 + QC] * LOG2E
        m_new = jnp.maximum(m, cq + jnp.max(y, axis=0, keepdims=True))
        alpha = jnp.exp2(m - m_new)
        p = jnp.exp2(y - (m_new - cq))
        pv = lax.dot_general(v_blocks[d], p.astype(BF16), (((0,), (0,)), ((), ())),
                             preferred_element_type=F32)
        l = alpha * l + pv[HEAD_DIM:HEAD_DIM + 1]
        return m_new, l, alpha * acc + pv[:HEAD_DIM]

    n_chunks = seq // QC
    state = [(jnp.full((1, QC), NEG_BIG, F32), jnp.zeros((1, QC), F32),
              jnp.zeros((HEAD_DIM, QC), F32)) for _ in range(n_chunks)]
    items = [(d, c) for d in range(seq // TK) for c in range(d * TK // QC, n_chunks)]
    ys = [scores(*item) for item in items[:FOX_LOOKAHEAD]]
    for i, (d, c) in enumerate(items):
        if i + FOX_LOOKAHEAD < len(items):
            ys.append(scores(*items[i + FOX_LOOKAHEAD]))
        state[c] = accumulate(d, c, ys.pop(0), *state[c])
    for c in range(n_chunks):
        _, l, acc = state[c]
        o_ref[0, c * QC:(c + 1) * QC, :] = (acc / l).T.astype(o_ref.dtype)


def _attention(qkv, cum_col, cum_row, batch, seq):
    grid = (batch, N_HEADS)

    def head_spec(head0):
        return pl.BlockSpec((1, seq, HEAD_DIM), lambda b, h: (b, 0, head0 + h))

    out_shape = jax.ShapeDtypeStruct((batch, seq, D_BRANCH), BF16)
    sem = ("arbitrary", "arbitrary")

    o_sb = pl.pallas_call(
        functools.partial(_sb_kernel, seq=seq),
        grid=grid,
        in_specs=[head_spec(0), head_spec(N_HEADS), head_spec(2 * N_HEADS)],
        out_specs=head_spec(0),
        out_shape=out_shape,
        compiler_params=_params(sem),
        name="stickbreak_attn",
    )(qkv, qkv, qkv)

    o_fx = pl.pallas_call(
        functools.partial(_fox_kernel, seq=seq),
        grid=grid,
        in_specs=[
            head_spec(3 * N_HEADS), head_spec(4 * N_HEADS), head_spec(5 * N_HEADS),
            pl.BlockSpec((1, N_HEADS, seq), lambda b, h: (b, 0, 0)),
            pl.BlockSpec((1, seq, LANES), lambda b, h: (b, 0, 0)),
        ],
        out_specs=head_spec(0),
        out_shape=out_shape,
        compiler_params=_params(sem),
        name="forgetting_attn",
    )(qkv, qkv, qkv, cum_row, cum_col)
    return o_sb, o_fx


def _merge_kernel(osb_ref, ofx_ref, gsb_ref, gfx_ref, wsb_ref, wfx_ref, o_ref):
    a = jnp.dot(osb_ref[...], wsb_ref[...], preferred_element_type=F32)
    b = jnp.dot(ofx_ref[...], wfx_ref[...], preferred_element_type=F32)
    o_ref[...] = (gsb_ref[...].astype(F32) * a + gfx_ref[...].astype(F32) * b).astype(o_ref.dtype)


def _merge(o_sb, o_fx, gates, w_sb, w_fx):
    t, k = o_sb.shape
    d = w_sb.shape[1]
    tm, tn = TM_MERGE, TN_MERGE
    n_j = d // tn
    return pl.pallas_call(
        _merge_kernel,
        grid=(t // tm, n_j),
        in_specs=[
            pl.BlockSpec((tm, k), lambda i, j: (i, 0)),
            pl.BlockSpec((tm, k), lambda i, j: (i, 0)),
            pl.BlockSpec((tm, tn), lambda i, j: (i, j)),
            pl.BlockSpec((tm, tn), lambda i, j: (i, n_j + j)),
            pl.BlockSpec((k, tn), lambda i, j: (0, j)),
            pl.BlockSpec((k, tn), lambda i, j: (0, j)),
        ],
        out_specs=pl.BlockSpec((tm, tn), lambda i, j: (i, j)),
        out_shape=jax.ShapeDtypeStruct((t, d), BF16),
        compiler_params=_params(("arbitrary", "arbitrary")),
        name="gated_merge",
    )(o_sb, o_fx, gates, gates, w_sb, w_fx)


def _out_proj_kernel(m_ref, x_ref, wout_ref, npost_ref, npre_ref, h_ref, u_ref):
    tm = m_ref.shape[0]
    chunks = [slice(r0, r0 + OUT_ROWS) for r0 in range(0, tm, OUT_ROWS)]
    mixes = [jnp.dot(m_ref[rows, :], wout_ref[...], preferred_element_type=F32)
             for rows in chunks]
    for rows, mix in zip(chunks, mixes):
        h = x_ref[rows, :] + _rms_norm(mix, npost_ref[...])
        h_ref[rows, :] = h
        u_ref[rows, :] = _rms_norm(h, npre_ref[...]).astype(BF16)


def _const_spec(shape):
    return pl.BlockSpec(shape, lambda i: (0,) * len(shape), pipeline_mode=pl.Buffered(1))


def _out_proj(merged, x2, w_out, n_post, n_ffn_pre):
    t, d = x2.shape
    tm = TM_OUT
    return pl.pallas_call(
        _out_proj_kernel,
        grid=(t // tm,),
        in_specs=[
            pl.BlockSpec((tm, d), lambda i: (i, 0)),
            pl.BlockSpec((tm, d), lambda i: (i, 0)),
            _const_spec((d, d)),
            _const_spec((1, d)),
            _const_spec((1, d)),
        ],
        out_specs=[
            pl.BlockSpec((tm, d), lambda i: (i, 0)),
            pl.BlockSpec((tm, d), lambda i: (i, 0)),
        ],
        out_shape=[
            jax.ShapeDtypeStruct((t, d), F32),
            jax.ShapeDtypeStruct((t, d), BF16),
        ],
        compiler_params=_params(("arbitrary",)),
        name="out_proj",
    )(merged, x2, w_out, n_post, n_ffn_pre)


def _ffn_kernel(u_ref, h_hbm_ref, wg_ref, wu_ref, wd_ref, npost_ref, o_ref, h_buf, h_sem):
    i, j = pl.program_id(0), pl.program_id(1)
    tm = o_ref.shape[0]
    h_copy = pltpu.make_async_copy(h_hbm_ref.at[pl.ds(i * tm, tm), :], h_buf, h_sem)

    @pl.when(j == 0)
    def _():
        h_copy.start()
        o_ref[...] = jnp.zeros_like(o_ref)

    u = u_ref[...]
    g = jnp.dot(u, wg_ref[...], preferred_element_type=F32)
    up = jnp.dot(u, wu_ref[...], preferred_element_type=F32)
    hidden = ((g * _sigmoid(g)) * up).astype(BF16)
    for c0 in range(0, o_ref.shape[1], DOWN_CHUNK):
        cols = slice(c0, c0 + DOWN_CHUNK)
        o_ref[:, cols] += jnp.dot(hidden, wd_ref[:, cols], preferred_element_type=F32)

    @pl.when(j == pl.num_programs(1) - 1)
    def _():
        h_copy.wait()
        for r0 in range(0, tm, NORM_ROWS):
            rows = slice(r0, r0 + NORM_ROWS)
            o_ref[rows, :] = h_buf[rows, :] + _rms_norm(o_ref[rows, :], npost_ref[...])


def _ffn(u2, h1, w_gate, w_up, w_down, n_post):
    t, d = h1.shape
    f = w_gate.shape[1]
    tm, tf = TM_FFN, TF_FFN
    return pl.pallas_call(
        _ffn_kernel,
        grid=(t // tm, f // tf),
        in_specs=[
            pl.BlockSpec((tm, d), lambda i, j: (i, 0)),
            pl.BlockSpec(memory_space=pl.ANY),
            pl.BlockSpec((d, tf), lambda i, j: (0, j)),
            pl.BlockSpec((d, tf), lambda i, j: (0, j)),
            pl.BlockSpec((tf, d), lambda i, j: (j, 0)),
            pl.BlockSpec((1, d), lambda i, j: (0, 0)),
        ],
        out_specs=pl.BlockSpec((tm, d), lambda i, j: (i, 0)),
        out_shape=jax.ShapeDtypeStruct((t, d), F32),
        scratch_shapes=[pltpu.VMEM((tm, d), F32), pltpu.SemaphoreType.DMA(())],
        compiler_params=_params(("arbitrary", "arbitrary")),
        name="swiglu_ffn",
    )(u2, h1, w_gate, w_up, w_down, n_post)


def kernel(x, norm_mix_pre, norm_mix_post, w_in, b_forget, w_branch_sb, w_branch_fox, w_out,
           norm_ffn_pre, norm_ffn_post, w_ffn_gate, w_ffn_up, w_ffn_down):
    batch, seq, d = x.shape
    depth = w_in.shape[0]
    n_qkv = 6 * D_BRANCH
    assert n_qkv % LANES == 0 and seq % TM_PROJ == 0
    h = x.reshape(batch * seq, d)
    for l in range(depth):
        w_t = jnp.swapaxes(w_in[l], 0, 1)
        w_q = _untranspose_cast(w_t, n_qkv)
        b_f = jnp.pad(b_forget[l], (0, LANES - N_HEADS)).reshape(1, LANES)
        g_pre = norm_mix_pre[l].reshape(1, d)

        q_scale = jnp.full((D_BRANCH,), ATTN_SCALE * LOG2E, F32)
        ones = jnp.ones((D_BRANCH,), F32)
        col_scale = jnp.concatenate([-q_scale, ones, ones, q_scale, ones, ones]).reshape(1, n_qkv)
        qkv, u1, wg_bf, wu_bf, wd_bf = _qkv_proj(
            h, g_pre, w_q, col_scale, [w_ffn_gate[l], w_ffn_up[l], w_ffn_down[l]])
        gates, cum_col, cum_row, wo_bf, wsb_bf, wfx_bf = _gates_proj(
            u1, w_t, n_qkv, 2 * d, b_f, seq, [w_out[l], w_branch_sb[l], w_branch_fox[l]])
        o_sb, o_fx = _attention(qkv.reshape(batch, seq, n_qkv),
                                cum_col.reshape(batch, seq, LANES), cum_row, batch, seq)
        merged = _merge(o_sb.reshape(batch * seq, D_BRANCH), o_fx.reshape(batch * seq, D_BRANCH),
                        gates, wsb_bf, wfx_bf)
        h1, u2 = _out_proj(merged, h, wo_bf,
                           norm_mix_post[l].reshape(1, d), norm_ffn_pre[l].reshape(1, d))
        h = _ffn(u2, h1, wg_bf, wu_bf, wd_bf, norm_ffn_post[l].reshape(1, d))
    return h.reshape(batch, seq, d)
```

```python
import functools

import jax
import jax.numpy as jnp
from jax import lax
from jax.experimental import pallas as pl
from jax.experimental.pallas import tpu as pltpu

F32 = jnp.float32
BF16 = jnp.bfloat16

HEAD_DIM = 128
N_HEADS = 8
D_BRANCH = N_HEADS * HEAD_DIM
RMS_EPS = 1e-6
ATTN_SCALE = HEAD_DIM ** -0.5
NEG_BIG = -1e30
LOG2E = 1.4426950408889634
MAX_NEG_LOGIT = 126.0

LANES = 128
BF16_SUBLANES = 16
VMEM_LIMIT = 56 * 1024 * 1024

UNT_COLS = 256
TM_PROJ = 1024
TN_PROJ = 1024
CUM_CHUNK = 256
TK = 256
RC = 256
QC = 256
SB_LOOKAHEAD_QK = 3
SB_LOOKAHEAD_CUM = 2
FOX_LOOKAHEAD = 3
HEADS_PER_STEP = 2
TM_MERGE = 1024
TN_MERGE = 1024
TM_OUT = 512
OUT_ROWS = 256
TM_FFN = 1024
TF_FFN = 512
DOWN_CHUNK = 512
NORM_ROWS = 256


def _params(semantics):
    return pltpu.CompilerParams(dimension_semantics=semantics, vmem_limit_bytes=VMEM_LIMIT)


def _rms_norm(x, g):
    ms = jnp.mean(x * x, axis=-1, keepdims=True)
    return (x * lax.rsqrt(ms + RMS_EPS)) * g


def _log_sigmoid(x):
    return jnp.minimum(x, 0.0) - jnp.log1p(jnp.exp(-jnp.abs(x)))


def _sigmoid(x):
    return 1.0 / (1.0 + jnp.exp(-x))


def _dot_nt(a, b):
    return lax.dot_general(a, b, (((1,), (1,)), ((), ())), preferred_element_type=F32)


def _split3(x):
    hi = x.astype(BF16)
    r1 = x - hi.astype(F32)
    mid = r1.astype(BF16)
    lo = (r1 - mid.astype(F32)).astype(BF16)
    return hi, mid, lo


def _rider_plan(weights, steps_each):
    plan, s0 = [], 0
    for w, c in zip(weights, steps_each):
        assert w.shape[0] % c == 0 and (w.shape[0] // c) % BF16_SUBLANES == 0
        plan.append((w, s0, c))
        s0 += c
    return plan


def _rider_specs(plan, n_j):
    specs, shapes = [], []
    for w, s0, c in plan:
        def index(i, j, s0=s0, c=c):
            return (jnp.clip(i * n_j + j - s0, 0, c - 1), 0)
        specs.append(pl.BlockSpec((w.shape[0] // c, w.shape[1]), index))
        shapes.append(jax.ShapeDtypeStruct(w.shape, BF16))
    return specs, shapes


def _run_riders(step, windows, src_refs, dst_refs):
    for (s0, c), src, dst in zip(windows, src_refs, dst_refs):
        @pl.when((step >= s0) & (step < s0 + c))
        def _(src=src, dst=dst):
            dst[...] = src[...].astype(BF16)


def _qkv_kernel(x_ref, g_ref, w_ref, cs_ref, *refs, windows):
    k = len(windows)
    src_refs, (o_ref, u_ref), dst_refs = refs[:k], refs[k:k + 2], refs[k + 2:]
    _run_riders(pl.program_id(0) * pl.num_programs(1) + pl.program_id(1), windows,
                src_refs, dst_refs)

    @pl.when(pl.program_id(1) == 0)
    def _():
        u_ref[...] = _rms_norm(x_ref[...], g_ref[...]).astype(BF16)

    acc = jnp.dot(u_ref[...], w_ref[...], preferred_element_type=F32)
    o_ref[...] = (acc * cs_ref[...]).astype(o_ref.dtype)


def _untranspose_kernel(src_ref, dst_ref, *, n_blocks):
    @pl.when(pl.program_id(0) < n_blocks)
    def _():
        dst_ref[...] = src_ref[...].T.astype(BF16)

    @pl.when(pl.program_id(0) == n_blocks)
    def _():
        dst_ref[...] = jnp.zeros_like(dst_ref)


def _untranspose_cast(w_t, n):
    d = w_t.shape[1]
    n_blocks = n // UNT_COLS
    return pl.pallas_call(
        functools.partial(_untranspose_kernel, n_blocks=n_blocks),
        grid=(n_blocks + 1,),
        in_specs=[pl.BlockSpec((UNT_COLS, d), lambda s: (jnp.minimum(s, n_blocks - 1), 0))],
        out_specs=pl.BlockSpec((d, UNT_COLS), lambda s: (0, s)),
        out_shape=jax.ShapeDtypeStruct((d, n + UNT_COLS), BF16),
        compiler_params=_params(("arbitrary",)),
        name="untranspose_cast",
    )(w_t)


def _qkv_proj(x2, g, w, col_scale, ride):
    t, d = x2.shape
    n = col_scale.shape[1]
    n_i, n_j = t // TM_PROJ, n // TN_PROJ
    plan = _rider_plan(ride, [n_i * n_j // len(ride)] * len(ride))
    ride_specs, ride_shapes = _rider_specs(plan, n_j)
    return pl.pallas_call(
        functools.partial(_qkv_kernel, windows=[(s0, c) for _, s0, c in plan]),
        grid=(n_i, n_j),
        in_specs=[
            pl.BlockSpec((TM_PROJ, d), lambda i, j: (i, 0)),
            pl.BlockSpec((1, d), lambda i, j: (0, 0)),
            pl.BlockSpec((d, TN_PROJ), lambda i, j: (0, j)),
            pl.BlockSpec((1, TN_PROJ), lambda i, j: (0, j)),
        ] + ride_specs,
        out_specs=[
            pl.BlockSpec((TM_PROJ, TN_PROJ), lambda i, j: (i, j)),
            pl.BlockSpec((TM_PROJ, d), lambda i, j: (i, 0)),
        ] + ride_specs,
        out_shape=[
            jax.ShapeDtypeStruct((t, n), BF16),
            jax.ShapeDtypeStruct((t, d), BF16),
        ] + ride_shapes,
        compiler_params=_params(("arbitrary", "arbitrary")),
        name="qkv_proj",
    )(x2, g, w, col_scale, *ride)


def _gates_kernel(u_ref, wt_hbm_ref, wf_ref, bf_ref, *refs, tiles_per_seq, windows, row0):
    k = len(windows)
    src_refs, (o_ref, cumc_ref, cumr_ref), dst_refs = refs[:k], refs[k:k + 3], refs[k + 3:-3]
    carry_ref, w_buf, w_sem = refs[-3:]
    i, j = pl.program_id(0), pl.program_id(1)
    n_j = pl.num_programs(1)
    step = i * n_j + j
    _run_riders(step, windows, src_refs, dst_refs)

    def slab_copy(jj, slot):
        rows = pl.ds(row0 + jj * TN_PROJ, TN_PROJ)
        return pltpu.make_async_copy(wt_hbm_ref.at[rows, :], w_buf.at[slot], w_sem.at[slot])

    slot = step % 2

    @pl.when(step == 0)
    def _():
        slab_copy(0, 0).start()

    @pl.when(step + 1 < pl.num_programs(0) * n_j)
    def _():
        slab_copy((j + 1) % n_j, 1 - slot).start()

    @pl.when(pl.program_id(1) == 0)
    def _():
        f = _dot_nt(u_ref[...], wf_ref[...].astype(BF16))
        log_f = _log_sigmoid(f + bf_ref[...])

        @pl.when(i % tiles_per_seq == 0)
        def _():
            carry_ref[...] = jnp.zeros_like(carry_ref)

        r = lax.broadcasted_iota(jnp.int32, (CUM_CHUNK, CUM_CHUNK), 0)
        c = lax.broadcasted_iota(jnp.int32, (CUM_CHUNK, CUM_CHUNK), 1)
        lower = (c <= r).astype(BF16)
        carry = carry_ref[...]
        for ch in range(TM_PROJ // CUM_CHUNK):
            blk = log_f[ch * CUM_CHUNK:(ch + 1) * CUM_CHUNK]
            hi, mid, lo = _split3(blk)
            cs = (jnp.dot(lower, hi, preferred_element_type=F32)
                  + jnp.dot(lower, mid, preferred_element_type=F32)
                  + jnp.dot(lower, lo, preferred_element_type=F32)) + carry
            cumc_ref[ch * CUM_CHUNK:(ch + 1) * CUM_CHUNK, :] = cs
            carry = cs[CUM_CHUNK - 1:CUM_CHUNK, :]
        carry_ref[...] = carry
        cumr_ref[0] = cumc_ref[...].T[:N_HEADS, :]

    slab_copy(j, slot).wait()
    acc = _dot_nt(u_ref[...], w_buf[slot].astype(BF16))
    o_ref[...] = _sigmoid(acc).astype(o_ref.dtype)


def _gates_proj(u, w_t, f_row0, n, bf, seq, ride):
    t, d = u.shape
    assert f_row0 % LANES == 0
    tiles_per_seq = seq // TM_PROJ
    n_i, n_j = t // TM_PROJ, n // TN_PROJ
    ride_rows = sum(r.shape[0] for r in ride)
    plan = _rider_plan(ride, [n_i * n_j * r.shape[0] // ride_rows for r in ride])
    ride_specs, ride_shapes = _rider_specs(plan, n_j)
    return pl.pallas_call(
        functools.partial(_gates_kernel, tiles_per_seq=tiles_per_seq,
                          windows=[(s0, c) for _, s0, c in plan], row0=f_row0 + N_HEADS),
        grid=(n_i, n_j),
        in_specs=[
            pl.BlockSpec((TM_PROJ, d), lambda i, j: (i, 0)),
            pl.BlockSpec(memory_space=pl.ANY),
            pl.BlockSpec((LANES, d), lambda i, j: (f_row0 // LANES, 0)),
            pl.BlockSpec((1, LANES), lambda i, j: (0, 0)),
        ] + ride_specs,
        out_specs=[
            pl.BlockSpec((TM_PROJ, TN_PROJ), lambda i, j: (i, j)),
            pl.BlockSpec((TM_PROJ, LANES), lambda i, j: (i, 0)),
            pl.BlockSpec((1, N_HEADS, TM_PROJ),
                         lambda i, j: (i // tiles_per_seq, 0, i % tiles_per_seq)),
        ] + ride_specs,
        out_shape=[
            jax.ShapeDtypeStruct((t, n), BF16),
            jax.ShapeDtypeStruct((t, LANES), F32),
            jax.ShapeDtypeStruct((t // seq, N_HEADS, seq), F32),
        ] + ride_shapes,
        scratch_shapes=[pltpu.VMEM((1, LANES), F32), pltpu.VMEM((2, TN_PROJ, d), F32),
                        pltpu.SemaphoreType.DMA((2,))],
        compiler_params=_params(("arbitrary", "arbitrary")),
        name="gates_proj",
    )(u, w_t, w_t, bf, *ride)


def _head_cols(hh):
    return slice(hh * HEAD_DIM, (hh + 1) * HEAD_DIM)


def _sb_kernel(q_ref, k_ref, v_ref, o_ref, *, seq):
    r = lax.broadcasted_iota(jnp.int32, (TK, TK), 0)
    c = lax.broadcasted_iota(jnp.int32, (TK, TK), 1)
    later = (r > c).astype(BF16)

    def valid_mask(d, c):
        k0, r0 = d * TK, c * RC
        if k0 + TK <= r0:
            return None
        return (k0 + lax.broadcasted_iota(jnp.int32, (RC, TK), 1)
                < r0 + lax.broadcasted_iota(jnp.int32, (RC, TK), 0))

    def gates(hh, d, c):
        q = q_ref[0, c * RC:(c + 1) * RC, _head_cols(hh)]
        k = k_ref[0, d * TK:(d + 1) * TK, _head_cols(hh)]
        nz = jnp.minimum(_dot_nt(q, k), MAX_NEG_LOGIT)
        neg_log_beta = jnp.log2(1.0 + jnp.exp2(nz))
        log_keep = nz - neg_log_beta
        valid = valid_mask(d, c)
        if valid is not None:
            log_keep = jnp.where(valid, log_keep, 0.0)
        return neg_log_beta, log_keep.astype(BF16), jnp.sum(log_keep, axis=1, keepdims=True)

    def within_block(neg_log_beta, keep_bf, row_sum):
        return jnp.dot(keep_bf, later, preferred_element_type=F32) - neg_log_beta, row_sum

    def accumulate(hh, d, c, log_w, row_sum, carry, acc):
        v = v_ref[0, d * TK:(d + 1) * TK, _head_cols(hh)]
        w = jnp.exp2(log_w + carry)
        valid = valid_mask(d, c)
        if valid is not None:
            w = jnp.where(valid, w, 0.0)
        return carry + row_sum, acc + jnp.dot(w.astype(BF16), v, preferred_element_type=F32)

    n_chunks = seq // RC
    heads = range(HEADS_PER_STEP)
    state = {(hh, c): (jnp.zeros((RC, 1), F32), jnp.zeros((RC, HEAD_DIM), F32))
             for hh in heads for c in range(n_chunks)}
    items = [(hh, d, c) for d in reversed(range(seq // TK))
             for c in range(d * TK // RC, n_chunks) for hh in heads]
    n = len(items)
    staged_a, staged_b = {}, {}
    for step in range(-SB_LOOKAHEAD_QK, n):
        i_a, i_b = step + SB_LOOKAHEAD_QK, step + SB_LOOKAHEAD_CUM
        if i_a < n:
            staged_a[i_a] = gates(*items[i_a])
        if 0 <= i_b < n:
            staged_b[i_b] = within_block(*staged_a.pop(i_b))
        if step >= 0:
            hh, d, c = items[step]
            state[hh, c] = accumulate(hh, d, c, *staged_b.pop(step), *state[hh, c])
    for (hh, c), (_, acc) in state.items():
        o_ref[0, c * RC:(c + 1) * RC, _head_cols(hh)] = acc.astype(o_ref.dtype)


def _fox_kernel(q_ref, k_ref, v_ref, cq_ref, ck_ref, o_ref, *, seq):
    h0 = pl.program_id(1) * HEADS_PER_STEP
    heads = range(HEADS_PER_STEP)
    lane = lax.broadcasted_iota(jnp.int32, (TK, LANES), 1)
    ones_col = (lane == 0).astype(BF16)
    n_blocks = seq // TK
    ck_blocks = {(hh, d): jnp.sum(jnp.where(lane == h0 + hh,
                                            ck_ref[0, d * TK:(d + 1) * TK, :], 0.0),
                                  axis=1, keepdims=True) * LOG2E
                 for hh in heads for d in range(n_blocks)}
    v_blocks = {(hh, d): jnp.concatenate([v_ref[0, d * TK:(d + 1) * TK, _head_cols(hh)],
                                          ones_col], axis=1)
                for hh in heads for d in range(n_blocks)}

    def scores(hh, d, c):
        k0, c0 = d * TK, c * QC
        q = q_ref[0, c0:c0 + QC, _head_cols(hh)]
        k = k_ref[0, k0:k0 + TK, _head_cols(hh)]
        y = _dot_nt(k, q) - ck_blocks[hh, d]
        if k0 + TK > c0:
            valid = (k0 + lax.broadcasted_iota(jnp.int32, (TK, QC), 0)
                     <= c0 + lax.broadcasted_iota(jnp.int32, (TK, QC), 1))
            y = jnp.where(valid, y, NEG_BIG)
        return y

    def accumulate(hh, d, c, y, m, l, acc):
        c0 = c * QC
        cq = cq_ref[0, pl.ds(h0 + hh, 1), c0:c0 + QC] * LOG2E
        m_new = jnp.maximum(m, cq + jnp.max(y, axis=0, keepdims=True))
        alpha = jnp.exp2(m - m_new)
        p = jnp.exp2(y - (m_new - cq))
        pv = lax.dot_general(v_blocks[hh, d], p.astype(BF16), (((0,), (0,)), ((), ())),
                             preferred_element_type=F32)
        l = alpha * l + pv[HEAD_DIM:HEAD_DIM + 1]
        return m_new, l, alpha * acc + pv[:HEAD_DIM]

    n_chunks = seq // QC
    state = {(hh, c): (jnp.full((1, QC), NEG_BIG, F32), jnp.zeros((1, QC), F32),
                       jnp.zeros((HEAD_DIM, QC), F32)) for hh in heads for c in range(n_chunks)}
    items = [(hh, d, c) for d in range(seq // TK) for c in range(d * TK // QC, n_chunks)
             for hh in heads]
    ys = [scores(*item) for item in items[:FOX_LOOKAHEAD]]
    for i, (hh, d, c) in enumerate(items):
        if i + FOX_LOOKAHEAD < len(items):
            ys.append(scores(*items[i + FOX_LOOKAHEAD]))
        state[hh, c] = accumulate(hh, d, c, ys.pop(0), *state[hh, c])
    for (hh, c), (_, l, acc) in state.items():
        o_ref[0, c * QC:(c + 1) * QC, _head_cols(hh)] = (acc / l).T.astype(o_ref.dtype)


def _attention(qkv, cum_col, cum_row, batch, seq):
    grid = (batch, N_HEADS // HEADS_PER_STEP)
    width = HEADS_PER_STEP * HEAD_DIM

    def head_spec(head0):
        return pl.BlockSpec((1, seq, width), lambda b, g: (b, 0, head0 // HEADS_PER_STEP + g))

    out_shape = jax.ShapeDtypeStruct((batch, seq, D_BRANCH), BF16)
    sem = ("arbitrary", "arbitrary")

    o_sb = pl.pallas_call(
        functools.partial(_sb_kernel, seq=seq),
        grid=grid,
        in_specs=[head_spec(0), head_spec(N_HEADS), head_spec(2 * N_HEADS)],
        out_specs=head_spec(0),
        out_shape=out_shape,
        compiler_params=_params(sem),
        name="stickbreak_attn",
    )(qkv, qkv, qkv)

    o_fx = pl.pallas_call(
        functools.partial(_fox_kernel, seq=seq),
        grid=grid,
        in_specs=[
            head_spec(3 * N_HEADS), head_spec(4 * N_HEADS), head_spec(5 * N_HEADS),
            pl.BlockSpec((1, N_HEADS, seq), lambda b, g: (b, 0, 0)),
            pl.BlockSpec((1, seq, LANES), lambda b, g: (b, 0, 0)),
        ],
        out_specs=head_spec(0),
        out_shape=out_shape,
        compiler_params=_params(sem),
        name="forgetting_attn",
    )(qkv, qkv, qkv, cum_row, cum_col)
    return o_sb, o_fx


def _merge_kernel(osb_ref, ofx_ref, gsb_ref, gfx_ref, wsb_ref, wfx_ref, o_ref):
    a = jnp.dot(osb_ref[...], wsb_ref[...], preferred_element_type=F32)
    b = jnp.dot(ofx_ref[...], wfx_ref[...], preferred_element_type=F32)
    o_ref[...] = (gsb_ref[...].astype(F32) * a + gfx_ref[...].astype(F32) * b).astype(o_ref.dtype)


def _merge(o_sb, o_fx, gates, w_sb, w_fx):
    t, k = o_sb.shape
    d = w_sb.shape[1]
    tm, tn = TM_MERGE, TN_MERGE
    n_j = d // tn
    return pl.pallas_call(
        _merge_kernel,
        grid=(t // tm, n_j),
        in_specs=[
            pl.BlockSpec((tm, k), lambda i, j: (i, 0)),
            pl.BlockSpec((tm, k), lambda i, j: (i, 0)),
            pl.BlockSpec((tm, tn), lambda i, j: (i, j)),
            pl.BlockSpec((tm, tn), lambda i, j: (i, n_j + j)),
            pl.BlockSpec((k, tn), lambda i, j: (0, j)),
            pl.BlockSpec((k, tn), lambda i, j: (0, j)),
        ],
        out_specs=pl.BlockSpec((tm, tn), lambda i, j: (i, j)),
        out_shape=jax.ShapeDtypeStruct((t, d), BF16),
        compiler_params=_params(("arbitrary", "arbitrary")),
        name="gated_merge",
    )(o_sb, o_fx, gates, gates, w_sb, w_fx)


def _out_proj_kernel(m_ref, x_ref, wout_ref, npost_ref, npre_ref, h_ref, u_ref):
    tm = m_ref.shape[0]
    chunks = [slice(r0, r0 + OUT_ROWS) for r0 in range(0, tm, OUT_ROWS)]
    mixes = [jnp.dot(m_ref[rows, :], wout_ref[...], preferred_element_type=F32)
             for rows in chunks]
    for rows, mix in zip(chunks, mixes):
        h = x_ref[rows, :] + _rms_norm(mix, npost_ref[...])
        h_ref[rows, :] = h
        u_ref[rows, :] = _rms_norm(h, npre_ref[...]).astype(BF16)


def _const_spec(shape):
    return pl.BlockSpec(shape, lambda i: (0,) * len(shape), pipeline_mode=pl.Buffered(1))


def _out_proj(merged, x2, w_out, n_post, n_ffn_pre):
    t, d = x2.shape
    tm = TM_OUT
    return pl.pallas_call(
        _out_proj_kernel,
        grid=(t // tm,),
        in_specs=[
            pl.BlockSpec((tm, d), lambda i: (i, 0)),
            pl.BlockSpec((tm, d), lambda i: (i, 0)),
            _const_spec((d, d)),
            _const_spec((1, d)),
            _const_spec((1, d)),
        ],
        out_specs=[
            pl.BlockSpec((tm, d), lambda i: (i, 0)),
            pl.BlockSpec((tm, d), lambda i: (i, 0)),
        ],
        out_shape=[
            jax.ShapeDtypeStruct((t, d), F32),
            jax.ShapeDtypeStruct((t, d), BF16),
        ],
        compiler_params=_params(("arbitrary",)),
        name="out_proj",
    )(merged, x2, w_out, n_post, n_ffn_pre)


def _ffn_kernel(u_ref, h_hbm_ref, wg_ref, wu_ref, wd_ref, npost_ref, o_ref, h_buf, h_sem):
    i, j = pl.program_id(0), pl.program_id(1)
    tm = o_ref.shape[0]
    h_copy = pltpu.make_async_copy(h_hbm_ref.at[pl.ds(i * tm, tm), :], h_buf, h_sem)

    @pl.when(j == 0)
    def _():
        h_copy.start()
        o_ref[...] = jnp.zeros_like(o_ref)

    u = u_ref[...]
    g = jnp.dot(u, wg_ref[...], preferred_element_type=F32)
    up = jnp.dot(u, wu_ref[...], preferred_element_type=F32)
    hidden = ((g * _sigmoid(g)) * up).astype(BF16)
    for c0 in range(0, o_ref.shape[1], DOWN_CHUNK):
        cols = slice(c0, c0 + DOWN_CHUNK)
        o_ref[:, cols] += jnp.dot(hidden, wd_ref[:, cols], preferred_element_type=F32)

    @pl.when(j == pl.num_programs(1) - 1)
    def _():
        h_copy.wait()
        for r0 in range(0, tm, NORM_ROWS):
            rows = slice(r0, r0 + NORM_ROWS)
            o_ref[rows, :] = h_buf[rows, :] + _rms_norm(o_ref[rows, :], npost_ref[...])


def _ffn(u2, h1, w_gate, w_up, w_down, n_post):
    t, d = h1.shape
    f = w_gate.shape[1]
    tm, tf = TM_FFN, TF_FFN
    return pl.pallas_call(
        _ffn_kernel,
        grid=(t // tm, f // tf),
        in_specs=[
            pl.BlockSpec((tm, d), lambda i, j: (i, 0)),
            pl.BlockSpec(memory_space=pl.ANY),
            pl.BlockSpec((d, tf), lambda i, j: (0, j)),
            pl.BlockSpec((d, tf), lambda i, j: (0, j)),
            pl.BlockSpec((tf, d), lambda i, j: (j, 0)),
            pl.BlockSpec((1, d), lambda i, j: (0, 0)),
        ],
        out_specs=pl.BlockSpec((tm, d), lambda i, j: (i, 0)),
        out_shape=jax.ShapeDtypeStruct((t, d), F32),
        scratch_shapes=[pltpu.VMEM((tm, d), F32), pltpu.SemaphoreType.DMA(())],
        compiler_params=_params(("arbitrary", "arbitrary")),
        name="swiglu_ffn",
    )(u2, h1, w_gate, w_up, w_down, n_post)


def kernel(x, norm_mix_pre, norm_mix_post, w_in, b_forget, w_branch_sb, w_branch_fox, w_out,
           norm_ffn_pre, norm_ffn_post, w_ffn_gate, w_ffn_up, w_ffn_down):
    batch, seq, d = x.shape
    depth = w_in.shape[0]
    n_qkv = 6 * D_BRANCH
    assert n_qkv % LANES == 0 and seq % TM_PROJ == 0
    h = x.reshape(batch * seq, d)
    for l in range(depth):
        w_t = jnp.swapaxes(w_in[l], 0, 1)
        w_q = _untranspose_cast(w_t, n_qkv)
        b_f = jnp.pad(b_forget[l], (0, LANES - N_HEADS)).reshape(1, LANES)
        g_pre = norm_mix_pre[l].reshape(1, d)

        q_scale = jnp.full((D_BRANCH,), ATTN_SCALE * LOG2E, F32)
        ones = jnp.ones((D_BRANCH,), F32)
        col_scale = jnp.concatenate([-q_scale, ones, ones, q_scale, ones, ones]).reshape(1, n_qkv)
        qkv, u1, wg_bf, wu_bf, wd_bf = _qkv_proj(
            h, g_pre, w_q, col_scale, [w_ffn_gate[l], w_ffn_up[l], w_ffn_down[l]])
        gates, cum_col, cum_row, wo_bf, wsb_bf, wfx_bf = _gates_proj(
            u1, w_t, n_qkv, 2 * d, b_f, seq, [w_out[l], w_branch_sb[l], w_branch_fox[l]])
        o_sb, o_fx = _attention(qkv.reshape(batch, seq, n_qkv),
                                cum_col.reshape(batch, seq, LANES), cum_row, batch, seq)
        merged = _merge(o_sb.reshape(batch * seq, D_BRANCH), o_fx.reshape(batch * seq, D_BRANCH),
                        gates, wsb_bf, wfx_bf)
        h1, u2 = _out_proj(merged, h, wo_bf,
                           norm_mix_post[l].reshape(1, d), norm_ffn_pre[l].reshape(1, d))
        h = _ffn(u2, h1, wg_bf, wu_bf, wd_bf, norm_ffn_post[l].reshape(1, d))
    return h.reshape(batch, seq, d)
```

```python
import functools

import jax
import jax.numpy as jnp
from jax import lax
from jax.experimental import pallas as pl
from jax.experimental.pallas import tpu as pltpu

F32 = jnp.float32
BF16 = jnp.bfloat16

HEAD_DIM = 128
N_HEADS = 8
D_BRANCH = N_HEADS * HEAD_DIM
RMS_EPS = 1e-6
ATTN_SCALE = HEAD_DIM ** -0.5
NEG_BIG = -1e30
LOG2E = 1.4426950408889634
MAX_NEG_LOGIT = 126.0

LANES = 128
BF16_SUBLANES = 16
VMEM_LIMIT = 56 * 1024 * 1024

UNT_COLS = 256
TM_PROJ = 1024
TN_PROJ = 1024
CUM_CHUNK = 256
TK = 256
RC = 256
QC = 256
SB_LOOKAHEAD_QK = 3
SB_LOOKAHEAD_CUM = 2
FOX_LOOKAHEAD = 3
HEADS_PER_STEP = 4
TM_MERGE = 1024
TN_MERGE = 1024
TM_OUT = 512
OUT_ROWS = 256
TM_FFN = 1024
TF_FFN = 512
DOWN_CHUNK = 512
NORM_ROWS = 256


def _params(semantics):
    return pltpu.CompilerParams(dimension_semantics=semantics, vmem_limit_bytes=VMEM_LIMIT)


def _rms_norm(x, g):
    ms = jnp.mean(x * x, axis=-1, keepdims=True)
    return (x * lax.rsqrt(ms + RMS_EPS)) * g


def _log_sigmoid(x):
    return jnp.minimum(x, 0.0) - jnp.log1p(jnp.exp(-jnp.abs(x)))


def _sigmoid(x):
    return 1.0 / (1.0 + jnp.exp(-x))


def _dot_nt(a, b):
    return lax.dot_general(a, b, (((1,), (1,)), ((), ())), preferred_element_type=F32)


def _split3(x):
    hi = x.astype(BF16)
    r1 = x - hi.astype(F32)
    mid = r1.astype(BF16)
    lo = (r1 - mid.astype(F32)).astype(BF16)
    return hi, mid, lo


def _rider_plan(weights, steps_each):
    plan, s0 = [], 0
    for w, c in zip(weights, steps_each):
        assert w.shape[0] % c == 0 and (w.shape[0] // c) % BF16_SUBLANES == 0
        plan.append((w, s0, c))
        s0 += c
    return plan


def _rider_specs(plan, n_j):
    specs, shapes = [], []
    for w, s0, c in plan:
        def index(i, j, s0=s0, c=c):
            return (jnp.clip(i * n_j + j - s0, 0, c - 1), 0)
        specs.append(pl.BlockSpec((w.shape[0] // c, w.shape[1]), index))
        shapes.append(jax.ShapeDtypeStruct(w.shape, BF16))
    return specs, shapes


def _run_riders(step, windows, src_refs, dst_refs):
    for (s0, c), src, dst in zip(windows, src_refs, dst_refs):
        @pl.when((step >= s0) & (step < s0 + c))
        def _(src=src, dst=dst):
            dst[...] = src[...].astype(BF16)


def _qkv_kernel(x_ref, g_ref, w_ref, cs_ref, *refs, windows):
    k = len(windows)
    src_refs, (o_ref, u_ref), dst_refs = refs[:k], refs[k:k + 2], refs[k + 2:]
    _run_riders(pl.program_id(0) * pl.num_programs(1) + pl.program_id(1), windows,
                src_refs, dst_refs)

    @pl.when(pl.program_id(1) == 0)
    def _():
        u_ref[...] = _rms_norm(x_ref[...], g_ref[...]).astype(BF16)

    acc = jnp.dot(u_ref[...], w_ref[...], preferred_element_type=F32)
    o_ref[...] = (acc * cs_ref[...]).astype(o_ref.dtype)


def _untranspose_kernel(src_ref, dst_ref, *, n_blocks):
    @pl.when(pl.program_id(0) < n_blocks)
    def _():
        dst_ref[...] = src_ref[...].T.astype(BF16)

    @pl.when(pl.program_id(0) == n_blocks)
    def _():
        dst_ref[...] = jnp.zeros_like(dst_ref)


def _untranspose_cast(w_t, n):
    d = w_t.shape[1]
    n_blocks = n // UNT_COLS
    return pl.pallas_call(
        functools.partial(_untranspose_kernel, n_blocks=n_blocks),
        grid=(n_blocks + 1,),
        in_specs=[pl.BlockSpec((UNT_COLS, d), lambda s: (jnp.minimum(s, n_blocks - 1), 0))],
        out_specs=pl.BlockSpec((d, UNT_COLS), lambda s: (0, s)),
        out_shape=jax.ShapeDtypeStruct((d, n + UNT_COLS), BF16),
        compiler_params=_params(("arbitrary",)),
        name="untranspose_cast",
    )(w_t)


def _qkv_proj(x2, g, w, col_scale, ride):
    t, d = x2.shape
    n = col_scale.shape[1]
    n_i, n_j = t // TM_PROJ, n // TN_PROJ
    plan = _rider_plan(ride, [n_i * n_j // len(ride)] * len(ride))
    ride_specs, ride_shapes = _rider_specs(plan, n_j)
    return pl.pallas_call(
        functools.partial(_qkv_kernel, windows=[(s0, c) for _, s0, c in plan]),
        grid=(n_i, n_j),
        in_specs=[
            pl.BlockSpec((TM_PROJ, d), lambda i, j: (i, 0)),
            pl.BlockSpec((1, d), lambda i, j: (0, 0)),
            pl.BlockSpec((d, TN_PROJ), lambda i, j: (0, j)),
            pl.BlockSpec((1, TN_PROJ), lambda i, j: (0, j)),
        ] + ride_specs,
        out_specs=[
            pl.BlockSpec((TM_PROJ, TN_PROJ), lambda i, j: (i, j)),
            pl.BlockSpec((TM_PROJ, d), lambda i, j: (i, 0)),
        ] + ride_specs,
        out_shape=[
            jax.ShapeDtypeStruct((t, n), BF16),
            jax.ShapeDtypeStruct((t, d), BF16),
        ] + ride_shapes,
        compiler_params=_params(("arbitrary", "arbitrary")),
        name="qkv_proj",
    )(x2, g, w, col_scale, *ride)


def _gates_kernel(u_ref, wt_hbm_ref, wf_ref, bf_ref, *refs, tiles_per_seq, windows, row0):
    k = len(windows)
    src_refs, (o_ref, cumc_ref, cumr_ref), dst_refs = refs[:k], refs[k:k + 3], refs[k + 3:-3]
    carry_ref, w_buf, w_sem = refs[-3:]
    i, j = pl.program_id(0), pl.program_id(1)
    n_j = pl.num_programs(1)
    step = i * n_j + j
    _run_riders(step, windows, src_refs, dst_refs)

    def slab_copy(jj, slot):
        rows = pl.ds(row0 + jj * TN_PROJ, TN_PROJ)
        return pltpu.make_async_copy(wt_hbm_ref.at[rows, :], w_buf.at[slot], w_sem.at[slot])

    slot = step % 2

    @pl.when(step == 0)
    def _():
        slab_copy(0, 0).start()

    @pl.when(step + 1 < pl.num_programs(0) * n_j)
    def _():
        slab_copy((j + 1) % n_j, 1 - slot).start()

    @pl.when(pl.program_id(1) == 0)
    def _():
        f = _dot_nt(u_ref[...], wf_ref[...].astype(BF16))
        log_f = _log_sigmoid(f + bf_ref[...])

        @pl.when(i % tiles_per_seq == 0)
        def _():
            carry_ref[...] = jnp.zeros_like(carry_ref)

        r = lax.broadcasted_iota(jnp.int32, (CUM_CHUNK, CUM_CHUNK), 0)
        c = lax.broadcasted_iota(jnp.int32, (CUM_CHUNK, CUM_CHUNK), 1)
        lower = (c <= r).astype(BF16)
        carry = carry_ref[...]
        for ch in range(TM_PROJ // CUM_CHUNK):
            blk = log_f[ch * CUM_CHUNK:(ch + 1) * CUM_CHUNK]
            hi, mid, lo = _split3(blk)
            cs = (jnp.dot(lower, hi, preferred_element_type=F32)
                  + jnp.dot(lower, mid, preferred_element_type=F32)
                  + jnp.dot(lower, lo, preferred_element_type=F32)) + carry
            cumc_ref[ch * CUM_CHUNK:(ch + 1) * CUM_CHUNK, :] = cs
            carry = cs[CUM_CHUNK - 1:CUM_CHUNK, :]
        carry_ref[...] = carry
        cumr_ref[0] = cumc_ref[...].T[:N_HEADS, :]

    slab_copy(j, slot).wait()
    acc = _dot_nt(u_ref[...], w_buf[slot].astype(BF16))
    o_ref[...] = _sigmoid(acc).astype(o_ref.dtype)


def _gates_proj(u, w_t, f_row0, n, bf, seq, ride):
    t, d = u.shape
    assert f_row0 % LANES == 0
    tiles_per_seq = seq // TM_PROJ
    n_i, n_j = t // TM_PROJ, n // TN_PROJ
    ride_rows = sum(r.shape[0] for r in ride)
    plan = _rider_plan(ride, [n_i * n_j * r.shape[0] // ride_rows for r in ride])
    ride_specs, ride_shapes = _rider_specs(plan, n_j)
    return pl.pallas_call(
        functools.partial(_gates_kernel, tiles_per_seq=tiles_per_seq,
                          windows=[(s0, c) for _, s0, c in plan], row0=f_row0 + N_HEADS),
        grid=(n_i, n_j),
        in_specs=[
            pl.BlockSpec((TM_PROJ, d), lambda i, j: (i, 0)),
            pl.BlockSpec(memory_space=pl.ANY),
            pl.BlockSpec((LANES, d), lambda i, j: (f_row0 // LANES, 0)),
            pl.BlockSpec((1, LANES), lambda i, j: (0, 0)),
        ] + ride_specs,
        out_specs=[
            pl.BlockSpec((TM_PROJ, TN_PROJ), lambda i, j: (i, j)),
            pl.BlockSpec((TM_PROJ, LANES), lambda i, j: (i, 0)),
            pl.BlockSpec((1, N_HEADS, TM_PROJ),
                         lambda i, j: (i // tiles_per_seq, 0, i % tiles_per_seq)),
        ] + ride_specs,
        out_shape=[
            jax.ShapeDtypeStruct((t, n), BF16),
            jax.ShapeDtypeStruct((t, LANES), F32),
            jax.ShapeDtypeStruct((t // seq, N_HEADS, seq), F32),
        ] + ride_shapes,
        scratch_shapes=[pltpu.VMEM((1, LANES), F32), pltpu.VMEM((2, TN_PROJ, d), F32),
                        pltpu.SemaphoreType.DMA((2,))],
        compiler_params=_params(("arbitrary", "arbitrary")),
        name="gates_proj",
    )(u, w_t, w_t, bf, *ride)


def _head_cols(hh):
    return slice(hh * HEAD_DIM, (hh + 1) * HEAD_DIM)


def _sb_kernel(q_ref, k_ref, v_ref, o_ref, *, seq):
    r = lax.broadcasted_iota(jnp.int32, (TK, TK), 0)
    c = lax.broadcasted_iota(jnp.int32, (TK, TK), 1)
    later = (r > c).astype(BF16)

    def valid_mask(d, c):
        k0, r0 = d * TK, c * RC
        if k0 + TK <= r0:
            return None
        return (k0 + lax.broadcasted_iota(jnp.int32, (RC, TK), 1)
                < r0 + lax.broadcasted_iota(jnp.int32, (RC, TK), 0))

    def gates(hh, d, c):
        q = q_ref[0, c * RC:(c + 1) * RC, _head_cols(hh)]
        k = k_ref[0, d * TK:(d + 1) * TK, _head_cols(hh)]
        nz = jnp.minimum(_dot_nt(q, k), MAX_NEG_LOGIT)
        neg_log_beta = jnp.log2(1.0 + jnp.exp2(nz))
        log_keep = nz - neg_log_beta
        valid = valid_mask(d, c)
        if valid is not None:
            log_keep = jnp.where(valid, log_keep, 0.0)
        return neg_log_beta, log_keep.astype(BF16), jnp.sum(log_keep, axis=1, keepdims=True)

    def within_block(neg_log_beta, keep_bf, row_sum):
        return jnp.dot(keep_bf, later, preferred_element_type=F32) - neg_log_beta, row_sum

    def accumulate(hh, d, c, log_w, row_sum, carry, acc):
        v = v_ref[0, d * TK:(d + 1) * TK, _head_cols(hh)]
        w = jnp.exp2(log_w + carry)
        valid = valid_mask(d, c)
        if valid is not None:
            w = jnp.where(valid, w, 0.0)
        return carry + row_sum, acc + jnp.dot(w.astype(BF16), v, preferred_element_type=F32)

    n_chunks = seq // RC
    heads = range(HEADS_PER_STEP)
    state = {(hh, c): (jnp.zeros((RC, 1), F32), jnp.zeros((RC, HEAD_DIM), F32))
             for hh in heads for c in range(n_chunks)}
    items = [(hh, d, c) for d in reversed(range(seq // TK))
             for c in range(d * TK // RC, n_chunks) for hh in heads]
    n = len(items)
    staged_a, staged_b = {}, {}
    for step in range(-SB_LOOKAHEAD_QK, n):
        i_a, i_b = step + SB_LOOKAHEAD_QK, step + SB_LOOKAHEAD_CUM
        if i_a < n:
            staged_a[i_a] = gates(*items[i_a])
        if 0 <= i_b < n:
            staged_b[i_b] = within_block(*staged_a.pop(i_b))
        if step >= 0:
            hh, d, c = items[step]
            state[hh, c] = accumulate(hh, d, c, *staged_b.pop(step), *state[hh, c])
    for (hh, c), (_, acc) in state.items():
        o_ref[0, c * RC:(c + 1) * RC, _head_cols(hh)] = acc.astype(o_ref.dtype)


def _fox_kernel(q_ref, k_ref, v_ref, cq_ref, ck_ref, o_ref, *, seq):
    h0 = pl.program_id(1) * HEADS_PER_STEP
    heads = range(HEADS_PER_STEP)
    lane = lax.broadcasted_iota(jnp.int32, (TK, LANES), 1)
    ones_col = (lane == 0).astype(BF16)
    n_blocks = seq // TK
    ck_blocks = {(hh, d): jnp.sum(jnp.where(lane == h0 + hh,
                                            ck_ref[0, d * TK:(d + 1) * TK, :], 0.0),
                                  axis=1, keepdims=True) * LOG2E
                 for hh in heads for d in range(n_blocks)}
    v_blocks = {(hh, d): jnp.concatenate([v_ref[0, d * TK:(d + 1) * TK, _head_cols(hh)],
                                          ones_col], axis=1)
                for hh in heads for d in range(n_blocks)}

    def scores(hh, d, c):
        k0, c0 = d * TK, c * QC
        q = q_ref[0, c0:c0 + QC, _head_cols(hh)]
        k = k_ref[0, k0:k0 + TK, _head_cols(hh)]
        y = _dot_nt(k, q) - ck_blocks[hh, d]
        if k0 + TK > c0:
            valid = (k0 + lax.broadcasted_iota(jnp.int32, (TK, QC), 0)
                     <= c0 + lax.broadcasted_iota(jnp.int32, (TK, QC), 1))
            y = jnp.where(valid, y, NEG_BIG)
        return y

    def accumulate(hh, d, c, y, m, l, acc):
        c0 = c * QC
        cq = cq_ref[0, pl.ds(h0 + hh, 1), c0:c0 + QC] * LOG2E
        m_new = jnp.maximum(m, cq + jnp.max(y, axis=0, keepdims=True))
        alpha = jnp.exp2(m - m_new)
        p = jnp.exp2(y - (m_new - cq))
        pv = lax.dot_general(v_blocks[hh, d], p.astype(BF16), (((0,), (0,)), ((), ())),
                             preferred_element_type=F32)
        l = alpha * l + pv[HEAD_DIM:HEAD_DIM + 1]
        return m_new, l, alpha * acc + pv[:HEAD_DIM]

    n_chunks = seq // QC
    state = {(hh, c): (jnp.full((1, QC), NEG_BIG, F32), jnp.zeros((1, QC), F32),
                       jnp.zeros((HEAD_DIM, QC), F32)) for hh in heads for c in range(n_chunks)}
    items = [(hh, d, c) for d in range(seq // TK) for c in range(d * TK // QC, n_chunks)
             for hh in heads]
    ys = [scores(*item) for item in items[:FOX_LOOKAHEAD]]
    for i, (hh, d, c) in enumerate(items):
        if i + FOX_LOOKAHEAD < len(items):
            ys.append(scores(*items[i + FOX_LOOKAHEAD]))
        state[hh, c] = accumulate(hh, d, c, ys.pop(0), *state[hh, c])
    for (hh, c), (_, l, acc) in state.items():
        o_ref[0, c * QC:(c + 1) * QC, _head_cols(hh)] = (acc / l).T.astype(o_ref.dtype)


def _attention(qkv, cum_col, cum_row, batch, seq):
    grid = (batch, N_HEADS // HEADS_PER_STEP)
    width = HEADS_PER_STEP * HEAD_DIM

    def head_spec(head0):
        return pl.BlockSpec((1, seq, width), lambda b, g: (b, 0, head0 // HEADS_PER_STEP + g))

    out_shape = jax.ShapeDtypeStruct((batch, seq, D_BRANCH), BF16)
    sem = ("arbitrary", "arbitrary")

    o_sb = pl.pallas_call(
        functools.partial(_sb_kernel, seq=seq),
        grid=grid,
        in_specs=[head_spec(0), head_spec(N_HEADS), head_spec(2 * N_HEADS)],
        out_specs=head_spec(0),
        out_shape=out_shape,
        compiler_params=_params(sem),
        name="stickbreak_attn",
    )(qkv, qkv, qkv)

    o_fx = pl.pallas_call(
        functools.partial(_fox_kernel, seq=seq),
        grid=grid,
        in_specs=[
            head_spec(3 * N_HEADS), head_spec(4 * N_HEADS), head_spec(5 * N_HEADS),
            pl.BlockSpec((1, N_HEADS, seq), lambda b, g: (b, 0, 0)),
            pl.BlockSpec((1, seq, LANES), lambda b, g: (b, 0, 0)),
        ],
        out_specs=head_spec(0),
        out_shape=out_shape,
        compiler_params=_params(sem),
        name="forgetting_attn",
    )(qkv, qkv, qkv, cum_row, cum_col)
    return o_sb, o_fx


def _merge_kernel(osb_ref, ofx_ref, gsb_ref, gfx_ref, wsb_ref, wfx_ref, o_ref):
    a = jnp.dot(osb_ref[...], wsb_ref[...], preferred_element_type=F32)
    b = jnp.dot(ofx_ref[...], wfx_ref[...], preferred_element_type=F32)
    o_ref[...] = (gsb_ref[...].astype(F32) * a + gfx_ref[...].astype(F32) * b).astype(o_ref.dtype)


def _merge(o_sb, o_fx, gates, w_sb, w_fx):
    t, k = o_sb.shape
    d = w_sb.shape[1]
    tm, tn = TM_MERGE, TN_MERGE
    n_j = d // tn
    return pl.pallas_call(
        _merge_kernel,
        grid=(t // tm, n_j),
        in_specs=[
            pl.BlockSpec((tm, k), lambda i, j: (i, 0)),
            pl.BlockSpec((tm, k), lambda i, j: (i, 0)),
            pl.BlockSpec((tm, tn), lambda i, j: (i, j)),
            pl.BlockSpec((tm, tn), lambda i, j: (i, n_j + j)),
            pl.BlockSpec((k, tn), lambda i, j: (0, j)),
            pl.BlockSpec((k, tn), lambda i, j: (0, j)),
        ],
        out_specs=pl.BlockSpec((tm, tn), lambda i, j: (i, j)),
        out_shape=jax.ShapeDtypeStruct((t, d), BF16),
        compiler_params=_params(("arbitrary", "arbitrary")),
        name="gated_merge",
    )(o_sb, o_fx, gates, gates, w_sb, w_fx)


def _out_proj_kernel(m_ref, x_ref, wout_ref, npost_ref, npre_ref, h_ref, u_ref):
    tm = m_ref.shape[0]
    chunks = [slice(r0, r0 + OUT_ROWS) for r0 in range(0, tm, OUT_ROWS)]
    mixes = [jnp.dot(m_ref[rows, :], wout_ref[...], preferred_element_type=F32)
             for rows in chunks]
    for rows, mix in zip(chunks, mixes):
        h = x_ref[rows, :] + _rms_norm(mix, npost_ref[...])
        h_ref[rows, :] = h
        u_ref[rows, :] = _rms_norm(h, npre_ref[...]).astype(BF16)


def _const_spec(shape):
    return pl.BlockSpec(shape, lambda i: (0,) * len(shape), pipeline_mode=pl.Buffered(1))


def _out_proj(merged, x2, w_out, n_post, n_ffn_pre):
    t, d = x2.shape
    tm = TM_OUT
    return pl.pallas_call(
        _out_proj_kernel,
        grid=(t // tm,),
        in_specs=[
            pl.BlockSpec((tm, d), lambda i: (i, 0)),
            pl.BlockSpec((tm, d), lambda i: (i, 0)),
            _const_spec((d, d)),
            _const_spec((1, d)),
            _const_spec((1, d)),
        ],
        out_specs=[
            pl.BlockSpec((tm, d), lambda i: (i, 0)),
            pl.BlockSpec((tm, d), lambda i: (i, 0)),
        ],
        out_shape=[
            jax.ShapeDtypeStruct((t, d), F32),
            jax.ShapeDtypeStruct((t, d), BF16),
        ],
        compiler_params=_params(("arbitrary",)),
        name="out_proj",
    )(merged, x2, w_out, n_post, n_ffn_pre)


def _ffn_kernel(u_ref, h_hbm_ref, wg_ref, wu_ref, wd_ref, npost_ref, o_ref, h_buf, h_sem):
    i, j = pl.program_id(0), pl.program_id(1)
    tm = o_ref.shape[0]
    h_copy = pltpu.make_async_copy(h_hbm_ref.at[pl.ds(i * tm, tm), :], h_buf, h_sem)

    @pl.when(j == 0)
    def _():
        h_copy.start()
        o_ref[...] = jnp.zeros_like(o_ref)

    u = u_ref[...]
    g = jnp.dot(u, wg_ref[...], preferred_element_type=F32)
    up = jnp.dot(u, wu_ref[...], preferred_element_type=F32)
    hidden = ((g * _sigmoid(g)) * up).astype(BF16)
    for c0 in range(0, o_ref.shape[1], DOWN_CHUNK):
        cols = slice(c0, c0 + DOWN_CHUNK)
        o_ref[:, cols] += jnp.dot(hidden, wd_ref[:, cols], preferred_element_type=F32)

    @pl.when(j == pl.num_programs(1) - 1)
    def _():
        h_copy.wait()
        for r0 in range(0, tm, NORM_ROWS):
            rows = slice(r0, r0 + NORM_ROWS)
            o_ref[rows, :] = h_buf[rows, :] + _rms_norm(o_ref[rows, :], npost_ref[...])


def _ffn(u2, h1, w_gate, w_up, w_down, n_post):
    t, d = h1.shape
    f = w_gate.shape[1]
    tm, tf = TM_FFN, TF_FFN
    return pl.pallas_call(
        _ffn_kernel,
        grid=(t // tm, f // tf),
        in_specs=[
            pl.BlockSpec((tm, d), lambda i, j: (i, 0)),
            pl.BlockSpec(memory_space=pl.ANY),
            pl.BlockSpec((d, tf), lambda i, j: (0, j)),
            pl.BlockSpec((d, tf), lambda i, j: (0, j)),
            pl.BlockSpec((tf, d), lambda i, j: (j, 0)),
            pl.BlockSpec((1, d), lambda i, j: (0, 0)),
        ],
        out_specs=pl.BlockSpec((tm, d), lambda i, j: (i, 0)),
        out_shape=jax.ShapeDtypeStruct((t, d), F32),
        scratch_shapes=[pltpu.VMEM((tm, d), F32), pltpu.SemaphoreType.DMA(())],
        compiler_params=_params(("arbitrary", "arbitrary")),
        name="swiglu_ffn",
    )(u2, h1, w_gate, w_up, w_down, n_post)


def kernel(x, norm_mix_pre, norm_mix_post, w_in, b_forget, w_branch_sb, w_branch_fox, w_out,
           norm_ffn_pre, norm_ffn_post, w_ffn_gate, w_ffn_up, w_ffn_down):
    batch, seq, d = x.shape
    depth = w_in.shape[0]
    n_qkv = 6 * D_BRANCH
    assert n_qkv % LANES == 0 and seq % TM_PROJ == 0
    h = x.reshape(batch * seq, d)
    for l in range(depth):
        w_t = jnp.swapaxes(w_in[l], 0, 1)
        w_q = _untranspose_cast(w_t, n_qkv)
        b_f = jnp.pad(b_forget[l], (0, LANES - N_HEADS)).reshape(1, LANES)
        g_pre = norm_mix_pre[l].reshape(1, d)

        q_scale = jnp.full((D_BRANCH,), ATTN_SCALE * LOG2E, F32)
        ones = jnp.ones((D_BRANCH,), F32)
        col_scale = jnp.concatenate([-q_scale, ones, ones, q_scale, ones, ones]).reshape(1, n_qkv)
        qkv, u1, wg_bf, wu_bf, wd_bf = _qkv_proj(
            h, g_pre, w_q, col_scale, [w_ffn_gate[l], w_ffn_up[l], w_ffn_down[l]])
        gates, cum_col, cum_row, wo_bf, wsb_bf, wfx_bf = _gates_proj(
            u1, w_t, n_qkv, 2 * d, b_f, seq, [w_out[l], w_branch_sb[l], w_branch_fox[l]])
        o_sb, o_fx = _attention(qkv.reshape(batch, seq, n_qkv),
                                cum_col.reshape(batch, seq, LANES), cum_row, batch, seq)
        merged = _merge(o_sb.reshape(batch * seq, D_BRANCH), o_fx.reshape(batch * seq, D_BRANCH),
                        gates, wsb_bf, wfx_bf)
        h1, u2 = _out_proj(merged, h, wo_bf,
                           norm_mix_post[l].reshape(1, d), norm_ffn_pre[l].reshape(1, d))
        h = _ffn(u2, h1, wg_bf, wu_bf, wd_bf, norm_ffn_post[l].reshape(1, d))
    return h.reshape(batch, seq, d)
```

```python
import functools

import jax
import jax.numpy as jnp
from jax import lax
from jax.experimental import pallas as pl
from jax.experimental.pallas import tpu as pltpu

F32 = jnp.float32
BF16 = jnp.bfloat16

HEAD_DIM = 128
N_HEADS = 8
D_BRANCH = N_HEADS * HEAD_DIM
RMS_EPS = 1e-6
ATTN_SCALE = HEAD_DIM ** -0.5
NEG_BIG = -1e30
LOG2E = 1.4426950408889634
MAX_NEG_LOGIT = 126.0

LANES = 128
BF16_SUBLANES = 16
VMEM_LIMIT = 56 * 1024 * 1024
HAND_DMA_PRIORITY = 1

UNT_COLS = 256
TM_PROJ = 1024
TN_PROJ = 1024
CUM_CHUNK = 256
TK = 256
RC = 256
QC = 256
SB_LOOKAHEAD_QK = 3
SB_LOOKAHEAD_CUM = 2
FOX_LOOKAHEAD = 3
HEADS_PER_STEP = 4
TM_MERGE = 1024
TN_MERGE = 1024
TM_OUT = 512
OUT_ROWS = 256
TM_FFN = 1024
TF_FFN = 512
DOWN_CHUNK = 512
NORM_ROWS = 256


def _params(semantics):
    return pltpu.CompilerParams(dimension_semantics=semantics, vmem_limit_bytes=VMEM_LIMIT)


def _rms_norm(x, g):
    ms = jnp.mean(x * x, axis=-1, keepdims=True)
    return (x * lax.rsqrt(ms + RMS_EPS)) * g


def _log_sigmoid(x):
    return jnp.minimum(x, 0.0) - jnp.log1p(jnp.exp(-jnp.abs(x)))


def _sigmoid(x):
    return 1.0 / (1.0 + jnp.exp(-x))


def _dot_nt(a, b):
    return lax.dot_general(a, b, (((1,), (1,)), ((), ())), preferred_element_type=F32)


def _split3(x):
    hi = x.astype(BF16)
    r1 = x - hi.astype(F32)
    mid = r1.astype(BF16)
    lo = (r1 - mid.astype(F32)).astype(BF16)
    return hi, mid, lo


def _rider_plan(weights, steps_each):
    plan, s0 = [], 0
    for w, c in zip(weights, steps_each):
        assert w.shape[0] % c == 0 and (w.shape[0] // c) % BF16_SUBLANES == 0
        plan.append((w, s0, c))
        s0 += c
    return plan


def _rider_specs(plan, n_j):
    specs, shapes = [], []
    for w, s0, c in plan:
        def index(i, j, s0=s0, c=c):
            return (jnp.clip(i * n_j + j - s0, 0, c - 1), 0)
        specs.append(pl.BlockSpec((w.shape[0] // c, w.shape[1]), index))
        shapes.append(jax.ShapeDtypeStruct(w.shape, BF16))
    return specs, shapes


def _run_riders(step, windows, src_refs, dst_refs):
    for (s0, c), src, dst in zip(windows, src_refs, dst_refs):
        @pl.when((step >= s0) & (step < s0 + c))
        def _(src=src, dst=dst):
            dst[...] = src[...].astype(BF16)


def _qkv_kernel(x_ref, g_ref, w_ref, cs_ref, *refs, windows):
    k = len(windows)
    src_refs, (o_ref, u_ref), dst_refs = refs[:k], refs[k:k + 2], refs[k + 2:]
    _run_riders(pl.program_id(0) * pl.num_programs(1) + pl.program_id(1), windows,
                src_refs, dst_refs)

    @pl.when(pl.program_id(1) == 0)
    def _():
        u_ref[...] = _rms_norm(x_ref[...], g_ref[...]).astype(BF16)

    acc = jnp.dot(u_ref[...], w_ref[...], preferred_element_type=F32)
    o_ref[...] = (acc * cs_ref[...]).astype(o_ref.dtype)


def _untranspose_kernel(src_ref, dst_ref, *, n_blocks):
    @pl.when(pl.program_id(0) < n_blocks)
    def _():
        dst_ref[...] = src_ref[...].T.astype(BF16)

    @pl.when(pl.program_id(0) == n_blocks)
    def _():
        dst_ref[...] = jnp.zeros_like(dst_ref)


def _untranspose_cast(w_t, n):
    d = w_t.shape[1]
    n_blocks = n // UNT_COLS
    return pl.pallas_call(
        functools.partial(_untranspose_kernel, n_blocks=n_blocks),
        grid=(n_blocks + 1,),
        in_specs=[pl.BlockSpec((UNT_COLS, d), lambda s: (jnp.minimum(s, n_blocks - 1), 0))],
        out_specs=pl.BlockSpec((d, UNT_COLS), lambda s: (0, s)),
        out_shape=jax.ShapeDtypeStruct((d, n + UNT_COLS), BF16),
        compiler_params=_params(("arbitrary",)),
        name="untranspose_cast",
    )(w_t)


def _qkv_proj(x2, g, w, col_scale, ride):
    t, d = x2.shape
    n = col_scale.shape[1]
    n_i, n_j = t // TM_PROJ, n // TN_PROJ
    plan = _rider_plan(ride, [n_i * n_j // len(ride)] * len(ride))
    ride_specs, ride_shapes = _rider_specs(plan, n_j)
    return pl.pallas_call(
        functools.partial(_qkv_kernel, windows=[(s0, c) for _, s0, c in plan]),
        grid=(n_i, n_j),
        in_specs=[
            pl.BlockSpec((TM_PROJ, d), lambda i, j: (i, 0)),
            pl.BlockSpec((1, d), lambda i, j: (0, 0)),
            pl.BlockSpec((d, TN_PROJ), lambda i, j: (0, j)),
            pl.BlockSpec((1, TN_PROJ), lambda i, j: (0, j)),
        ] + ride_specs,
        out_specs=[
            pl.BlockSpec((TM_PROJ, TN_PROJ), lambda i, j: (i, j)),
            pl.BlockSpec((TM_PROJ, d), lambda i, j: (i, 0)),
        ] + ride_specs,
        out_shape=[
            jax.ShapeDtypeStruct((t, n), BF16),
            jax.ShapeDtypeStruct((t, d), BF16),
        ] + ride_shapes,
        compiler_params=_params(("arbitrary", "arbitrary")),
        name="qkv_proj",
    )(x2, g, w, col_scale, *ride)


def _gates_kernel(u_ref, wt_hbm_ref, wf_ref, bf_ref, *refs, tiles_per_seq, windows, row0):
    k = len(windows)
    src_refs, (o_ref, cumc_ref, cumr_ref), dst_refs = refs[:k], refs[k:k + 3], refs[k + 3:-3]
    carry_ref, w_buf, w_sem = refs[-3:]
    i, j = pl.program_id(0), pl.program_id(1)
    n_j = pl.num_programs(1)
    step = i * n_j + j
    _run_riders(step, windows, src_refs, dst_refs)

    def slab_copy(jj, slot):
        rows = pl.ds(row0 + jj * TN_PROJ, TN_PROJ)
        return pltpu.make_async_copy(wt_hbm_ref.at[rows, :], w_buf.at[slot], w_sem.at[slot])

    slot = step % 2

    @pl.when(step == 0)
    def _():
        slab_copy(0, 0).start(priority=HAND_DMA_PRIORITY)

    @pl.when(step + 1 < pl.num_programs(0) * n_j)
    def _():
        slab_copy((j + 1) % n_j, 1 - slot).start(priority=HAND_DMA_PRIORITY)

    @pl.when(pl.program_id(1) == 0)
    def _():
        f = _dot_nt(u_ref[...], wf_ref[...].astype(BF16))
        log_f = _log_sigmoid(f + bf_ref[...])

        @pl.when(i % tiles_per_seq == 0)
        def _():
            carry_ref[...] = jnp.zeros_like(carry_ref)

        r = lax.broadcasted_iota(jnp.int32, (CUM_CHUNK, CUM_CHUNK), 0)
        c = lax.broadcasted_iota(jnp.int32, (CUM_CHUNK, CUM_CHUNK), 1)
        lower = (c <= r).astype(BF16)
        carry = carry_ref[...]
        for ch in range(TM_PROJ // CUM_CHUNK):
            blk = log_f[ch * CUM_CHUNK:(ch + 1) * CUM_CHUNK]
            hi, mid, lo = _split3(blk)
            cs = (jnp.dot(lower, hi, preferred_element_type=F32)
                  + jnp.dot(lower, mid, preferred_element_type=F32)
                  + jnp.dot(lower, lo, preferred_element_type=F32)) + carry
            cumc_ref[ch * CUM_CHUNK:(ch + 1) * CUM_CHUNK, :] = cs
            carry = cs[CUM_CHUNK - 1:CUM_CHUNK, :]
        carry_ref[...] = carry
        cumr_ref[0] = cumc_ref[...].T[:N_HEADS, :]

    slab_copy(j, slot).wait()
    acc = _dot_nt(u_ref[...], w_buf[slot].astype(BF16))
    o_ref[...] = _sigmoid(acc).astype(o_ref.dtype)


def _gates_proj(u, w_t, f_row0, n, bf, seq, ride):
    t, d = u.shape
    assert f_row0 % LANES == 0
    tiles_per_seq = seq // TM_PROJ
    n_i, n_j = t // TM_PROJ, n // TN_PROJ
    ride_rows = sum(r.shape[0] for r in ride)
    plan = _rider_plan(ride, [n_i * n_j * r.shape[0] // ride_rows for r in ride])
    ride_specs, ride_shapes = _rider_specs(plan, n_j)
    return pl.pallas_call(
        functools.partial(_gates_kernel, tiles_per_seq=tiles_per_seq,
                          windows=[(s0, c) for _, s0, c in plan], row0=f_row0 + N_HEADS),
        grid=(n_i, n_j),
        in_specs=[
            pl.BlockSpec((TM_PROJ, d), lambda i, j: (i, 0)),
            pl.BlockSpec(memory_space=pl.ANY),
            pl.BlockSpec((LANES, d), lambda i, j: (f_row0 // LANES, 0)),
            pl.BlockSpec((1, LANES), lambda i, j: (0, 0)),
        ] + ride_specs,
        out_specs=[
            pl.BlockSpec((TM_PROJ, TN_PROJ), lambda i, j: (i, j)),
            pl.BlockSpec((TM_PROJ, LANES), lambda i, j: (i, 0)),
            pl.BlockSpec((1, N_HEADS, TM_PROJ),
                         lambda i, j: (i // tiles_per_seq, 0, i % tiles_per_seq)),
        ] + ride_specs,
        out_shape=[
            jax.ShapeDtypeStruct((t, n), BF16),
            jax.ShapeDtypeStruct((t, LANES), F32),
            jax.ShapeDtypeStruct((t // seq, N_HEADS, seq), F32),
        ] + ride_shapes,
        scratch_shapes=[pltpu.VMEM((1, LANES), F32), pltpu.VMEM((2, TN_PROJ, d), F32),
                        pltpu.SemaphoreType.DMA((2,))],
        compiler_params=_params(("arbitrary", "arbitrary")),
        name="gates_proj",
    )(u, w_t, w_t, bf, *ride)


def _head_cols(hh):
    return slice(hh * HEAD_DIM, (hh + 1) * HEAD_DIM)


def _sb_kernel(q_ref, k_ref, v_ref, o_ref, *, seq):
    r = lax.broadcasted_iota(jnp.int32, (TK, TK), 0)
    c = lax.broadcasted_iota(jnp.int32, (TK, TK), 1)
    later = (r > c).astype(BF16)

    def valid_mask(d, c):
        k0, r0 = d * TK, c * RC
        if k0 + TK <= r0:
            return None
        return (k0 + lax.broadcasted_iota(jnp.int32, (RC, TK), 1)
                < r0 + lax.broadcasted_iota(jnp.int32, (RC, TK), 0))

    def gates(hh, d, c):
        q = q_ref[0, c * RC:(c + 1) * RC, _head_cols(hh)]
        k = k_ref[0, d * TK:(d + 1) * TK, _head_cols(hh)]
        nz = jnp.minimum(_dot_nt(q, k), MAX_NEG_LOGIT)
        neg_log_beta = jnp.log2(1.0 + jnp.exp2(nz))
        log_keep = nz - neg_log_beta
        valid = valid_mask(d, c)
        if valid is not None:
            log_keep = jnp.where(valid, log_keep, 0.0)
        return neg_log_beta, log_keep.astype(BF16), jnp.sum(log_keep, axis=1, keepdims=True)

    def within_block(neg_log_beta, keep_bf, row_sum):
        return jnp.dot(keep_bf, later, preferred_element_type=F32) - neg_log_beta, row_sum

    def accumulate(hh, d, c, log_w, row_sum, carry, acc):
        v = v_ref[0, d * TK:(d + 1) * TK, _head_cols(hh)]
        w = jnp.exp2(log_w + carry)
        valid = valid_mask(d, c)
        if valid is not None:
            w = jnp.where(valid, w, 0.0)
        return carry + row_sum, acc + jnp.dot(w.astype(BF16), v, preferred_element_type=F32)

    n_chunks = seq // RC
    heads = range(HEADS_PER_STEP)
    state = {(hh, c): (jnp.zeros((RC, 1), F32), jnp.zeros((RC, HEAD_DIM), F32))
             for hh in heads for c in range(n_chunks)}
    items = [(hh, d, c) for d in reversed(range(seq // TK))
             for c in range(d * TK // RC, n_chunks) for hh in heads]
    n = len(items)
    staged_a, staged_b = {}, {}
    for step in range(-SB_LOOKAHEAD_QK, n):
        i_a, i_b = step + SB_LOOKAHEAD_QK, step + SB_LOOKAHEAD_CUM
        if i_a < n:
            staged_a[i_a] = gates(*items[i_a])
        if 0 <= i_b < n:
            staged_b[i_b] = within_block(*staged_a.pop(i_b))
        if step >= 0:
            hh, d, c = items[step]
            state[hh, c] = accumulate(hh, d, c, *staged_b.pop(step), *state[hh, c])
    for (hh, c), (_, acc) in state.items():
        o_ref[0, c * RC:(c + 1) * RC, _head_cols(hh)] = acc.astype(o_ref.dtype)


def _fox_kernel(q_ref, k_ref, v_ref, cq_ref, ck_ref, o_ref, *, seq):
    h0 = pl.program_id(1) * HEADS_PER_STEP
    heads = range(HEADS_PER_STEP)
    lane = lax.broadcasted_iota(jnp.int32, (TK, LANES), 1)
    ones_col = (lane == 0).astype(BF16)
    n_blocks = seq // TK
    ck_blocks = {(hh, d): jnp.sum(jnp.where(lane == h0 + hh,
                                            ck_ref[0, d * TK:(d + 1) * TK, :], 0.0),
                                  axis=1, keepdims=True) * LOG2E
                 for hh in heads for d in range(n_blocks)}
    v_blocks = {(hh, d): jnp.concatenate([v_ref[0, d * TK:(d + 1) * TK, _head_cols(hh)],
                                          ones_col], axis=1)
                for hh in heads for d in range(n_blocks)}

    def scores(hh, d, c):
        k0, c0 = d * TK, c * QC
        q = q_ref[0, c0:c0 + QC, _head_cols(hh)]
        k = k_ref[0, k0:k0 + TK, _head_cols(hh)]
        y = _dot_nt(k, q) - ck_blocks[hh, d]
        if k0 + TK > c0:
            valid = (k0 + lax.broadcasted_iota(jnp.int32, (TK, QC), 0)
                     <= c0 + lax.broadcasted_iota(jnp.int32, (TK, QC), 1))
            y = jnp.where(valid, y, NEG_BIG)
        return y

    def accumulate(hh, d, c, y, m, l, acc):
        c0 = c * QC
        cq = cq_ref[0, pl.ds(h0 + hh, 1), c0:c0 + QC] * LOG2E
        m_new = jnp.maximum(m, cq + jnp.max(y, axis=0, keepdims=True))
        alpha = jnp.exp2(m - m_new)
        p = jnp.exp2(y - (m_new - cq))
        pv = lax.dot_general(v_blocks[hh, d], p.astype(BF16), (((0,), (0,)), ((), ())),
                             preferred_element_type=F32)
        l = alpha * l + pv[HEAD_DIM:HEAD_DIM + 1]
        return m_new, l, alpha * acc + pv[:HEAD_DIM]

    n_chunks = seq // QC
    state = {(hh, c): (jnp.full((1, QC), NEG_BIG, F32), jnp.zeros((1, QC), F32),
                       jnp.zeros((HEAD_DIM, QC), F32)) for hh in heads for c in range(n_chunks)}
    items = [(hh, d, c) for d in range(seq // TK) for c in range(d * TK // QC, n_chunks)
             for hh in heads]
    ys = [scores(*item) for item in items[:FOX_LOOKAHEAD]]
    for i, (hh, d, c) in enumerate(items):
        if i + FOX_LOOKAHEAD < len(items):
            ys.append(scores(*items[i + FOX_LOOKAHEAD]))
        state[hh, c] = accumulate(hh, d, c, ys.pop(0), *state[hh, c])
    for (hh, c), (_, l, acc) in state.items():
        o_ref[0, c * QC:(c + 1) * QC, _head_cols(hh)] = (acc / l).T.astype(o_ref.dtype)


def _attention(qkv, cum_col, cum_row, batch, seq):
    grid = (batch, N_HEADS // HEADS_PER_STEP)
    width = HEADS_PER_STEP * HEAD_DIM

    def head_spec(head0):
        return pl.BlockSpec((1, seq, width), lambda b, g: (b, 0, head0 // HEADS_PER_STEP + g))

    out_shape = jax.ShapeDtypeStruct((batch, seq, D_BRANCH), BF16)
    sem = ("arbitrary", "arbitrary")

    o_sb = pl.pallas_call(
        functools.partial(_sb_kernel, seq=seq),
        grid=grid,
        in_specs=[head_spec(0), head_spec(N_HEADS), head_spec(2 * N_HEADS)],
        out_specs=head_spec(0),
        out_shape=out_shape,
        compiler_params=_params(sem),
        name="stickbreak_attn",
    )(qkv, qkv, qkv)

    o_fx = pl.pallas_call(
        functools.partial(_fox_kernel, seq=seq),
        grid=grid,
        in_specs=[
            head_spec(3 * N_HEADS), head_spec(4 * N_HEADS), head_spec(5 * N_HEADS),
            pl.BlockSpec((1, N_HEADS, seq), lambda b, g: (b, 0, 0)),
            pl.BlockSpec((1, seq, LANES), lambda b, g: (b, 0, 0)),
        ],
        out_specs=head_spec(0),
        out_shape=out_shape,
        compiler_params=_params(sem),
        name="forgetting_attn",
    )(qkv, qkv, qkv, cum_row, cum_col)
    return o_sb, o_fx


def _merge_kernel(osb_ref, ofx_ref, gsb_ref, gfx_ref, wsb_ref, wfx_ref, o_ref):
    a = jnp.dot(osb_ref[...], wsb_ref[...], preferred_element_type=F32)
    b = jnp.dot(ofx_ref[...], wfx_ref[...], preferred_element_type=F32)
    o_ref[...] = (gsb_ref[...].astype(F32) * a + gfx_ref[...].astype(F32) * b).astype(o_ref.dtype)


def _merge(o_sb, o_fx, gates, w_sb, w_fx):
    t, k = o_sb.shape
    d = w_sb.shape[1]
    tm, tn = TM_MERGE, TN_MERGE
    n_j = d // tn
    return pl.pallas_call(
        _merge_kernel,
        grid=(t // tm, n_j),
        in_specs=[
            pl.BlockSpec((tm, k), lambda i, j: (i, 0)),
            pl.BlockSpec((tm, k), lambda i, j: (i, 0)),
            pl.BlockSpec((tm, tn), lambda i, j: (i, j)),
            pl.BlockSpec((tm, tn), lambda i, j: (i, n_j + j)),
            pl.BlockSpec((k, tn), lambda i, j: (0, j)),
            pl.BlockSpec((k, tn), lambda i, j: (0, j)),
        ],
        out_specs=pl.BlockSpec((tm, tn), lambda i, j: (i, j)),
        out_shape=jax.ShapeDtypeStruct((t, d), BF16),
        compiler_params=_params(("arbitrary", "arbitrary")),
        name="gated_merge",
    )(o_sb, o_fx, gates, gates, w_sb, w_fx)


def _out_proj_kernel(m_ref, x_ref, wout_ref, npost_ref, npre_ref, h_ref, u_ref):
    tm = m_ref.shape[0]
    chunks = [slice(r0, r0 + OUT_ROWS) for r0 in range(0, tm, OUT_ROWS)]
    mixes = [jnp.dot(m_ref[rows, :], wout_ref[...], preferred_element_type=F32)
             for rows in chunks]
    for rows, mix in zip(chunks, mixes):
        h = x_ref[rows, :] + _rms_norm(mix, npost_ref[...])
        h_ref[rows, :] = h
        u_ref[rows, :] = _rms_norm(h, npre_ref[...]).astype(BF16)


def _const_spec(shape):
    return pl.BlockSpec(shape, lambda i: (0,) * len(shape), pipeline_mode=pl.Buffered(1))


def _out_proj(merged, x2, w_out, n_post, n_ffn_pre):
    t, d = x2.shape
    tm = TM_OUT
    return pl.pallas_call(
        _out_proj_kernel,
        grid=(t // tm,),
        in_specs=[
            pl.BlockSpec((tm, d), lambda i: (i, 0)),
            pl.BlockSpec((tm, d), lambda i: (i, 0)),
            _const_spec((d, d)),
            _const_spec((1, d)),
            _const_spec((1, d)),
        ],
        out_specs=[
            pl.BlockSpec((tm, d), lambda i: (i, 0)),
            pl.BlockSpec((tm, d), lambda i: (i, 0)),
        ],
        out_shape=[
            jax.ShapeDtypeStruct((t, d), F32),
            jax.ShapeDtypeStruct((t, d), BF16),
        ],
        compiler_params=_params(("arbitrary",)),
        name="out_proj",
    )(merged, x2, w_out, n_post, n_ffn_pre)


def _ffn_kernel(u_ref, h_hbm_ref, wg_ref, wu_ref, wd_ref, npost_ref, o_ref, h_buf, h_sem):
    i, j = pl.program_id(0), pl.program_id(1)
    tm = o_ref.shape[0]
    h_copy = pltpu.make_async_copy(h_hbm_ref.at[pl.ds(i * tm, tm), :], h_buf, h_sem)

    @pl.when(j == 0)
    def _():
        h_copy.start(priority=HAND_DMA_PRIORITY)
        o_ref[...] = jnp.zeros_like(o_ref)

    u = u_ref[...]
    g = jnp.dot(u, wg_ref[...], preferred_element_type=F32)
    up = jnp.dot(u, wu_ref[...], preferred_element_type=F32)
    hidden = ((g * _sigmoid(g)) * up).astype(BF16)
    for c0 in range(0, o_ref.shape[1], DOWN_CHUNK):
        cols = slice(c0, c0 + DOWN_CHUNK)
        o_ref[:, cols] += jnp.dot(hidden, wd_ref[:, cols], preferred_element_type=F32)

    @pl.when(j == pl.num_programs(1) - 1)
    def _():
        h_copy.wait()
        for r0 in range(0, tm, NORM_ROWS):
            rows = slice(r0, r0 + NORM_ROWS)
            o_ref[rows, :] = h_buf[rows, :] + _rms_norm(o_ref[rows, :], npost_ref[...])


def _ffn(u2, h1, w_gate, w_up, w_down, n_post):
    t, d = h1.shape
    f = w_gate.shape[1]
    tm, tf = TM_FFN, TF_FFN
    return pl.pallas_call(
        _ffn_kernel,
        grid=(t // tm, f // tf),
        in_specs=[
            pl.BlockSpec((tm, d), lambda i, j: (i, 0)),
            pl.BlockSpec(memory_space=pl.ANY),
            pl.BlockSpec((d, tf), lambda i, j: (0, j)),
            pl.BlockSpec((d, tf), lambda i, j: (0, j)),
            pl.BlockSpec((tf, d), lambda i, j: (j, 0)),
            pl.BlockSpec((1, d), lambda i, j: (0, 0)),
        ],
        out_specs=pl.BlockSpec((tm, d), lambda i, j: (i, 0)),
        out_shape=jax.ShapeDtypeStruct((t, d), F32),
        scratch_shapes=[pltpu.VMEM((tm, d), F32), pltpu.SemaphoreType.DMA(())],
        compiler_params=_params(("arbitrary", "arbitrary")),
        name="swiglu_ffn",
    )(u2, h1, w_gate, w_up, w_down, n_post)


def kernel(x, norm_mix_pre, norm_mix_post, w_in, b_forget, w_branch_sb, w_branch_fox, w_out,
           norm_ffn_pre, norm_ffn_post, w_ffn_gate, w_ffn_up, w_ffn_down):
    batch, seq, d = x.shape
    depth = w_in.shape[0]
    n_qkv = 6 * D_BRANCH
    assert n_qkv % LANES == 0 and seq % TM_PROJ == 0
    h = x.reshape(batch * seq, d)
    for l in range(depth):
        w_t = jnp.swapaxes(w_in[l], 0, 1)
        w_q = _untranspose_cast(w_t, n_qkv)
        b_f = jnp.pad(b_forget[l], (0, LANES - N_HEADS)).reshape(1, LANES)
        g_pre = norm_mix_pre[l].reshape(1, d)

        q_scale = jnp.full((D_BRANCH,), ATTN_SCALE * LOG2E, F32)
        ones = jnp.ones((D_BRANCH,), F32)
        col_scale = jnp.concatenate([-q_scale, ones, ones, q_scale, ones, ones]).reshape(1, n_qkv)
        qkv, u1, wg_bf, wu_bf, wd_bf = _qkv_proj(
            h, g_pre, w_q, col_scale, [w_ffn_gate[l], w_ffn_up[l], w_ffn_down[l]])
        gates, cum_col, cum_row, wo_bf, wsb_bf, wfx_bf = _gates_proj(
            u1, w_t, n_qkv, 2 * d, b_f, seq, [w_out[l], w_branch_sb[l], w_branch_fox[l]])
        o_sb, o_fx = _attention(qkv.reshape(batch, seq, n_qkv),
                                cum_col.reshape(batch, seq, LANES), cum_row, batch, seq)
        merged = _merge(o_sb.reshape(batch * seq, D_BRANCH), o_fx.reshape(batch * seq, D_BRANCH),
                        gates, wsb_bf, wfx_bf)
        h1, u2 = _out_proj(merged, h, wo_bf,
                           norm_mix_post[l].reshape(1, d), norm_ffn_pre[l].reshape(1, d))
        h = _ffn(u2, h1, wg_bf, wu_bf, wd_bf, norm_ffn_post[l].reshape(1, d))
    return h.reshape(batch, seq, d)
```
